```python
import jax, jax.numpy as jnp
from jax import lax
import numpy as np

D_MODEL = 1024
BATCH = 8
SEQ = 4096
DEPTH = 2

EPS = 1e-6
HEAD_DIM = 64
ROPE_THETA = 10000.0
BLOCK = 128
SWA_HEADS = 8
SWA_KV_HEADS = 2
SWA_WINDOW = BLOCK
DSA_HEADS = 8
DSA_KV_HEADS = 2
IDX_HEADS = 4
IDX_DIM = 64
DSA_TOPK = 256
RET_HEADS = 4
RET_KEY_DIM = 64
RET_VAL_DIM = 128
RET_CHUNK = 128
POOL_WINDOWS = (2, 4, 8, 16)
POOL_GROUP_DIM = 128
N_BRANCH = 4
BRANCH_WIDTH = 512
MEM_LEN = 256
X_HEADS = 4
X_HEAD_DIM = D_MODEL // X_HEADS
D_FF = (7 * D_MODEL) // 2
N_EXPERTS = 8
TOP_K = 2

IN_SIZES = (
    SWA_HEADS * HEAD_DIM, SWA_KV_HEADS * HEAD_DIM, SWA_KV_HEADS * HEAD_DIM,
    DSA_HEADS * HEAD_DIM, DSA_KV_HEADS * HEAD_DIM, DSA_KV_HEADS * HEAD_DIM,
    IDX_HEADS * IDX_DIM, IDX_DIM, IDX_HEADS,
    RET_HEADS * RET_KEY_DIM, RET_HEADS * RET_KEY_DIM, RET_HEADS * RET_VAL_DIM, RET_HEADS * RET_VAL_DIM,
    len(POOL_WINDOWS) * POOL_GROUP_DIM,
    N_BRANCH * D_MODEL,
)
D_IN = sum(IN_SIZES)

kernel_name = "hybrid_gated_parallel_mixer_block"


def rms_norm(x, g):
    xf = x.astype(jnp.float32)
    y = xf * lax.rsqrt(jnp.mean(xf * xf, axis=-1, keepdims=True) + EPS)
    return (y * g.astype(jnp.float32)).astype(x.dtype)


def rope_tables(positions):
    inv = 1.0 / (ROPE_THETA ** (jnp.arange(0, HEAD_DIM, 2, dtype=jnp.float32) / HEAD_DIM))
    ang = positions.astype(jnp.float32)[..., None] * inv
    return jnp.cos(ang)[:, :, None, :], jnp.sin(ang)[:, :, None, :]


def apply_rope(x, cos, sin):
    x1, x2 = jnp.split(x.astype(jnp.float32), 2, axis=-1)
    return jnp.concatenate([x1 * cos - x2 * sin, x2 * cos + x1 * sin], axis=-1).astype(x.dtype)


def sliding_window_attention(q, k, v, sink):
    B, T, Hq, Dh = q.shape
    Hkv = k.shape[2]
    G = Hq // Hkv
    nb = T // BLOCK
    qb = q.reshape(B, nb, BLOCK, Hkv, G, Dh)
    kb = k.reshape(B, nb, BLOCK, Hkv, Dh)
    vb = v.reshape(B, nb, BLOCK, Hkv, Dh)
    prev = lambda z: jnp.concatenate([jnp.zeros_like(z[:, :1]), z[:, :-1]], axis=1)
    kw = jnp.concatenate([prev(kb), kb], axis=2)
    vw = jnp.concatenate([prev(vb), vb], axis=2)
    s = jnp.einsum("bnqhgd,bnkhd->bnhgqk", qb, kw).astype(jnp.float32) * (Dh ** -0.5)
    qi = jnp.arange(BLOCK)[:, None] + BLOCK
    ki = jnp.arange(2 * BLOCK)[None, :]
    rel = qi - ki
    band = (rel >= 0) & (rel < SWA_WINDOW)
    blk = jnp.arange(nb)[:, None, None]
    mask = band[None] & ((blk > 0) | (ki >= BLOCK)[None])
    s = jnp.where(mask[None, :, None, None], s, -jnp.inf)
    sk = sink.astype(jnp.float32).reshape(Hkv, G)[None, None, :, :, None, None]
    m = jnp.maximum(jnp.max(s, axis=-1, keepdims=True), sk)
    p = jnp.exp(s - m)
    p = p / (jnp.sum(p, axis=-1, keepdims=True) + jnp.exp(sk - m))
    o = jnp.einsum("bnhgqk,bnkhd->bnqhgd", p.astype(v.dtype), vw)
    return o.reshape(B, T, Hq * Dh)


def dsa_attention(q, k, v, q_idx, k_idx, w_idx):
    B, T, Hq, Dh = q.shape
    Hkv = k.shape[2]
    G = Hq // Hkv
    nb = T // BLOCK
    n_sel = min(DSA_TOPK, T // 4)
    to_blocks = lambda z: jnp.moveaxis(z.reshape(B, nb, BLOCK, *z.shape[2:]), 1, 0)
    gather = jax.vmap(lambda table, idx: table[idx])
    key_pos = jnp.arange(T)

    def one_block(args):
        q_b, qi_b, wi_b, start = args
        t = start + jnp.arange(BLOCK)
        logits = jnp.einsum("bqhd,bsd->bqhs", qi_b, k_idx).astype(jnp.float32) * (IDX_DIM ** -0.5)
        score = jnp.einsum("bqhs,bqh->bqs", jax.nn.relu(logits), wi_b.astype(jnp.float32))
        causal = key_pos[None, :] <= t[:, None]
        score = jnp.where(causal[None], score, -jnp.inf)
        _, idx = lax.top_k(score, n_sel)
        valid = idx <= t[None, :, None]
        k_sel = gather(k, idx)
        v_sel = gather(v, idx)
        qg = q_b.reshape(B, BLOCK, Hkv, G, Dh)
        s = jnp.einsum("bqhgd,bqkhd->bqhgk", qg, k_sel).astype(jnp.float32) * (Dh ** -0.5)
        s = jnp.where(valid[:, :, None, None, :], s, -jnp.inf)
        p = jax.nn.softmax(s, axis=-1)
        o = jnp.einsum("bqhgk,bqkhd->bqhgd", p.astype(v.dtype), v_sel)
        return o.reshape(B, BLOCK, Hq * Dh)

    starts = jnp.arange(nb, dtype=jnp.int32) * BLOCK
    out = lax.map(one_block, (to_blocks(q), to_blocks(q_idx), to_blocks(w_idx), starts))
    return jnp.moveaxis(out, 0, 1).reshape(B, T, Hq * Dh)


def retention(q, k, v, g):
    B, T, H, Dk = q.shape
    Dv = v.shape[-1]
    C = RET_CHUNK
    nc = T // C
    f32 = jnp.float32
    log_gamma = jnp.log1p(-(2.0 ** (-5.0 - jnp.arange(H, dtype=f32))))
    pos = jnp.arange(C, dtype=f32)
    rel = pos[:, None] - pos[None, :]
    decay = jnp.where(rel >= 0, jnp.exp(log_gamma[:, None, None] * jnp.maximum(rel, 0.0)), 0.0)
    q_decay = jnp.exp(log_gamma[:, None] * (pos + 1.0))[None, :, :, None]
    k_decay = jnp.exp(log_gamma[:, None] * (C - 1.0 - pos))[None, :, :, None]
    chunk_decay = jnp.exp(log_gamma * C)[None, :, None, None]
    to_chunks = lambda z: z.astype(f32).reshape(B, nc, C, H, z.shape[-1]).transpose(1, 0, 3, 2, 4)
    qc, kc, vc = to_chunks(q), to_chunks(k * (Dk ** -0.5)), to_chunks(v)

    def step(state, inp):
        qb, kb, vb = inp
        inner = jnp.einsum("bhnm,bhmv->bhnv", jnp.einsum("bhnd,bhmd->bhnm", qb, kb) * decay, vb)
        cross = jnp.einsum("bhnd,bhdv->bhnv", qb, state) * q_decay
        state = state * chunk_decay + jnp.einsum("bhmd,bhmv->bhdv", kb * k_decay, vb)
        return state, inner + cross

    state0 = jnp.zeros((B, H, Dk, Dv), f32)
    _, out = lax.scan(step, state0, (qc, kc, vc))
    out = out.transpose(1, 0, 3, 2, 4).reshape(B, T, H, Dv)
    out = out * lax.rsqrt(jnp.mean(out * out, axis=-1, keepdims=True) + EPS)
    return (jax.nn.silu(g.astype(f32)) * out.reshape(B, T, H * Dv)).astype(v.dtype)


def multiscale_pool(u, pool_w, pool_scale):
    B, T, _ = u.shape
    G = len(POOL_WINDOWS)
    uf = u.astype(jnp.float32).reshape(B, T, G, POOL_GROUP_DIM)
    csum = jnp.cumsum(uf, axis=1)
    t = jnp.arange(T)
    pooled = []
    for gi, w in enumerate(POOL_WINDOWS):
        c = csum[:, :, gi]
        lag = jnp.pad(c, ((0, 0), (w, 0), (0, 0)))[:, :T]
        cnt = jnp.minimum(t + 1, w).astype(jnp.float32)[None, :, None]
        pooled.append((c - lag) / cnt)
    pooled = jnp.stack(pooled, axis=2)
    y = jnp.einsum("btgc,gcd->btgd", pooled - uf, pool_w.astype(jnp.float32))
    return (y.reshape(B, T, G * POOL_GROUP_DIM) * pool_scale.astype(jnp.float32)).astype(u.dtype)


def token_mixer(h, cos, sin, w_in, sink, w_branch, w_out, pool_w, pool_scale):
    B, T, _ = h.shape
    proj = h @ w_in
    splits = np.cumsum(IN_SIZES)[:-1].tolist()
    (a_q, a_k, a_v, b_q, b_k, b_v, i_q, i_k, i_w,
     r_q, r_k, r_v, r_g, p_u, gates) = jnp.split(proj, splits, axis=-1)
    heads = lambda z, n: z.reshape(B, T, n, -1)
    rope = lambda z: apply_rope(z, cos, sin)
    o_a = sliding_window_attention(rope(heads(a_q, SWA_HEADS)), rope(heads(a_k, SWA_KV_HEADS)),
                                   heads(a_v, SWA_KV_HEADS), sink)
    o_b = dsa_attention(rope(heads(b_q, DSA_HEADS)), rope(heads(b_k, DSA_KV_HEADS)),
                        heads(b_v, DSA_KV_HEADS), rope(heads(i_q, IDX_HEADS)),
                        rope(i_k[:, :, None, :])[:, :, 0], i_w * (IDX_HEADS ** -0.5))
    o_c = retention(rope(heads(r_q, RET_HEADS)), rope(heads(r_k, RET_HEADS)),
                    heads(r_v, RET_HEADS), r_g)
    o_d = multiscale_pool(p_u, pool_w, pool_scale)
    branches = jnp.stack([o_a, o_b, o_c, o_d], axis=2)
    up = jnp.einsum("btnc,ncd->btnd", branches, w_branch)
    gate = jax.nn.sigmoid(gates.reshape(B, T, N_BRANCH, D_MODEL).astype(jnp.float32)).astype(h.dtype)
    merged = jnp.sum(gate * up, axis=2)
    return merged @ w_out


def memory_cross_attention(h, m, w_xq, w_xkv, w_xo):
    B, T, _ = h.shape
    M = m.shape[1]
    q = (h @ w_xq).reshape(B, T, X_HEADS, X_HEAD_DIM)
    k, v = jnp.split(m @ w_xkv, 2, axis=-1)
    k = k.reshape(B, M, X_HEADS, X_HEAD_DIM)
    v = v.reshape(B, M, X_HEADS, X_HEAD_DIM)
    s = jnp.einsum("bthd,bmhd->bhtm", q, k).astype(jnp.float32) * (X_HEAD_DIM ** -0.5)
    p = jax.nn.softmax(s, axis=-1)
    o = jnp.einsum("bhtm,bmhd->bthd", p.astype(v.dtype), v).reshape(B, T, X_HEADS * X_HEAD_DIM)
    return o @ w_xo


def swiglu(h, w_gu, w_down):
    g, u = jnp.split(h @ w_gu, 2, axis=-1)
    return (jax.nn.silu(g) * u) @ w_down


def moe_ffn(h, router_w, w_gu, w_down):
    logits = (h @ router_w).astype(jnp.float32)
    top_val, top_idx = lax.top_k(logits, TOP_K)
    top_w = jax.nn.softmax(top_val, axis=-1)
    gate = jnp.sum(jax.nn.one_hot(top_idx, N_EXPERTS, dtype=jnp.float32) * top_w[..., None], axis=-2)
    gate = gate.astype(h.dtype)
    out = jnp.zeros_like(h)
    for e in range(N_EXPERTS):
        out = out + gate[..., e:e + 1] * swiglu(h, w_gu[e], w_down[e])
    return out


def setup_inputs(seed: int = 0) -> dict:
    key = jax.random.key(seed)
    ks = jax.random.split(key, 24)
    f32 = jnp.float32
    n_dense = (DEPTH + 1) // 2
    n_moe = DEPTH // 2
    n_pool = len(POOL_WINDOWS)
    xw = X_HEADS * X_HEAD_DIM

    def dense(k, shape, fan_in):
        return jax.random.normal(k, shape, f32) * (fan_in ** -0.5)

    def gain(k, shape, scale=0.02):
        return 1.0 + scale * jax.random.normal(k, shape, f32)

    return {
        "x": jax.random.normal(ks[0], (BATCH, SEQ, D_MODEL), f32),
        "mem": jax.random.normal(ks[1], (BATCH, MEM_LEN, D_MODEL), f32),
        "positions": jnp.arange(SEQ, dtype=jnp.int32)[None, :]
        + jax.random.randint(ks[2], (BATCH, 1), 0, 1024, dtype=jnp.int32),
        "norm_mix": gain(ks[3], (DEPTH, D_MODEL)),
        "w_in": dense(ks[4], (DEPTH, D_MODEL, D_IN), D_MODEL),
        "attn_sink": 0.5 * jax.random.normal(ks[5], (DEPTH, SWA_HEADS), f32),
        "w_branch": dense(ks[6], (DEPTH, N_BRANCH, BRANCH_WIDTH, D_MODEL), BRANCH_WIDTH),
        "w_out": dense(ks[7], (DEPTH, D_MODEL, D_MODEL), D_MODEL),
        "pool_w": dense(ks[8], (DEPTH, n_pool, POOL_GROUP_DIM, POOL_GROUP_DIM), POOL_GROUP_DIM),
        "pool_scale": gain(ks[9], (DEPTH, n_pool * POOL_GROUP_DIM), 0.1),
        "norm_cross": gain(ks[10], (DEPTH, D_MODEL)),
        "norm_mem": gain(ks[11], (DEPTH, D_MODEL)),
        "w_xq": dense(ks[12], (DEPTH, D_MODEL, xw), D_MODEL),
        "w_xkv": dense(ks[13], (DEPTH, D_MODEL, 2 * xw), D_MODEL),
        "w_xo": dense(ks[14], (DEPTH, xw, D_MODEL), xw),
        "norm_ffn": gain(ks[15], (DEPTH, D_MODEL)),
        "dense_w_gu": dense(ks[16], (n_dense, D_MODEL, 2 * D_FF), D_MODEL),
        "dense_w_down": dense(ks[17], (n_dense, D_FF, D_MODEL), D_FF),
        "router_w": dense(ks[18], (n_moe, D_MODEL, N_EXPERTS), D_MODEL),
        "moe_w_gu": dense(ks[19], (n_moe, N_EXPERTS, D_MODEL, 2 * D_FF), D_MODEL),
        "moe_w_down": dense(ks[20], (n_moe, N_EXPERTS, D_FF, D_MODEL), D_FF),
        "norm_final": gain(ks[21], (D_MODEL,)),
    }


def reference(x, mem, positions, norm_mix, w_in, attn_sink, w_branch, w_out, pool_w, pool_scale,
              norm_cross, norm_mem, w_xq, w_xkv, w_xo, norm_ffn, dense_w_gu, dense_w_down,
              router_w, moe_w_gu, moe_w_down, norm_final):
    cos, sin = rope_tables(positions)
    for layer in range(DEPTH):
        h = rms_norm(x, norm_mix[layer])
        x = x + token_mixer(h, cos, sin, w_in[layer], attn_sink[layer], w_branch[layer],
                            w_out[layer], pool_w[layer], pool_scale[layer])
        h = rms_norm(x, norm_cross[layer])
        m = rms_norm(mem, norm_mem[layer])
        x = x + memory_cross_attention(h, m, w_xq[layer], w_xkv[layer], w_xo[layer])
        h = rms_norm(x, norm_ffn[layer])
        if layer % 2 == 0:
            x = x + swiglu(h, dense_w_gu[layer // 2], dense_w_down[layer // 2])
        else:
            x = x + moe_ffn(h, router_w[layer // 2], moe_w_gu[layer // 2], moe_w_down[layer // 2])
    return rms_norm(x, norm_final)
```

```python
import functools

import jax
import jax.numpy as jnp
from jax import lax
from jax.experimental import pallas as pl
from jax.experimental.pallas import tpu as pltpu

F32 = jnp.float32
BF16 = jnp.bfloat16
I32 = jnp.int32

EPS = 1e-6
HEAD_DIM = 64
ROPE_THETA = 10000.0
BLOCK = 128
SWA_HEADS, SWA_KV_HEADS = 8, 2
DSA_HEADS, DSA_KV_HEADS = 8, 2
IDX_HEADS, IDX_DIM = 4, 64
DSA_TOPK = 256
RET_HEADS, RET_KEY_DIM, RET_VAL_DIM, RET_CHUNK = 4, 64, 128, 128
POOL_WINDOWS = (2, 4, 8, 16)
POOL_GROUP_DIM = 128
N_BRANCH = 4
BRANCH_WIDTH = 512
X_HEADS = 4
TOP_K = 2

LANES = 128
VMEM_LIMIT = 56 * 1024 * 1024
NEG_BIG = -1e30
INT_MIN = -2 ** 31

PA_AQ, PA_BQ, PA_IQ, PA_RQ, PA_RK, PA_AK, PA_BK, PA_IK = 0, 512, 1024, 1280, 1536, 1792, 2048, 2304
PA_COLS, PA_TN = 2560, 640
PB_GATES, PB_RV, PB_RG, PB_PU, PB_AV, PB_BV, PB_IW = 0, 4096, 4608, 5120, 5632, 5888, 6144
PB_COLS, PB_TN = 6272, 896


def _cparams(sem):
    return pltpu.CompilerParams(dimension_semantics=sem, vmem_limit_bytes=VMEM_LIMIT)


def _dot(a, b):
    return jnp.dot(a, b, preferred_element_type=F32)


def _dot_nt(a, b):
    return lax.dot_general(a, b, (((1,), (1,)), ((), ())), preferred_element_type=F32)


def _rms(x, g):
    return x * lax.rsqrt(jnp.mean(x * x, axis=-1, keepdims=True) + EPS) * g


def _half_mask(hh):
    lane = lax.broadcasted_iota(I32, (1, LANES), 1)
    return (lane >= HEAD_DIM) if hh else (lane < HEAD_DIM)


def _keep_half(block, hh):
    return jnp.where(_half_mask(hh), block, jnp.zeros_like(block))


def _rope_kernel(pos_ref, inv_ref, cos_ref, sin_ref):
    ang = pos_ref[...].astype(F32) * inv_ref[...]
    cos_ref[...] = jnp.cos(ang)
    sin_ref[...] = jnp.sin(ang)


def rope_tables(pos_col, inv128):
    m = pos_col.shape[0]
    tm = min(m, 2048)
    return pl.pallas_call(
        _rope_kernel,
        out_shape=(jax.ShapeDtypeStruct((m, LANES), F32), jax.ShapeDtypeStruct((m, LANES), F32)),
        grid=(m // tm,),
        in_specs=[pl.BlockSpec((tm, 1), lambda i: (i, 0)), pl.BlockSpec((1, LANES), lambda i: (0, 0))],
        out_specs=(pl.BlockSpec((tm, LANES), lambda i: (i, 0)), pl.BlockSpec((tm, LANES), lambda i: (i, 0))),
        compiler_params=_cparams(("parallel",)),
        name="rope_tables",
    )(pos_col, inv128)


def _norm_matmul_kernel(x_ref, g_ref, w_ref, o_ref, h_ref):
    @pl.when(pl.program_id(1) == 0)
    def _():
        h_ref[...] = _rms(x_ref[...], g_ref[...]).astype(BF16)

    o_ref[...] = _dot(h_ref[...], w_ref[...]).astype(o_ref.dtype)


def norm_matmul(x, g, w, tm, tn, out_dtype=BF16, name="norm_matmul"):
    m, k = x.shape
    n = w.shape[1]
    tm = min(tm, m)
    return pl.pallas_call(
        _norm_matmul_kernel,
        out_shape=jax.ShapeDtypeStruct((m, n), out_dtype),
        grid=(m // tm, n // tn),
        in_specs=[pl.BlockSpec((tm, k), lambda i, j: (i, 0)),
                  pl.BlockSpec((1, k), lambda i, j: (0, 0)),
                  pl.BlockSpec((k, tn), lambda i, j: (0, j))],
        out_specs=pl.BlockSpec((tm, tn), lambda i, j: (i, j)),
        scratch_shapes=[pltpu.VMEM((tm, k), BF16)],
        compiler_params=_cparams(("parallel", "arbitrary")),
        name=name,
    )(x, g, w)


def _norm_matmul_rope_kernel(x_ref, g_ref, w_ref, cos_ref, sin_ref, o_ref, h_ref):
    @pl.when(pl.program_id(1) == 0)
    def _():
        h_ref[...] = _rms(x_ref[...], g_ref[...]).astype(BF16)

    tn = o_ref.shape[1]
    acc = _dot(h_ref[...], w_ref[...])
    cos = cos_ref[...]
    sin = sin_ref[...]
    for c in range(tn // LANES):
        lo = c * LANES
        o_ref[:, lo:lo + LANES] = (acc[:, lo:lo + LANES] * cos
                                   + acc[:, tn + lo:tn + lo + LANES] * sin).astype(o_ref.dtype)


def norm_matmul_rope(x, g, w, cos, sin, tm, tn):
    m, k = x.shape
    nt = w.shape[1] // (2 * tn)
    tm = min(tm, m)
    return pl.pallas_call(
        _norm_matmul_rope_kernel,
        out_shape=jax.ShapeDtypeStruct((m, nt * tn), BF16),
        grid=(m // tm, nt),
        in_specs=[pl.BlockSpec((tm, k), lambda i, j: (i, 0)),
                  pl.BlockSpec((1, k), lambda i, j: (0, 0)),
                  pl.BlockSpec((k, 2 * tn), lambda i, j: (0, j)),
                  pl.BlockSpec((tm, LANES), lambda i, j: (i, 0)),
                  pl.BlockSpec((tm, LANES), lambda i, j: (i, 0))],
        out_specs=pl.BlockSpec((tm, tn), lambda i, j: (i, j)),
        scratch_shapes=[pltpu.VMEM((tm, k), BF16)],
        compiler_params=_cparams(("parallel", "arbitrary")),
        name="in_proj_rope",
    )(x, g, w, cos, sin)


def _swa_kernel(sink_ref, q_ref, kc_ref, kp_ref, vc_ref, vp_ref, o_ref):
    i = pl.program_id(1)
    r = lax.broadcasted_iota(I32, (BLOCK, 1), 0)
    c = lax.broadcasted_iota(I32, (1, BLOCK), 1)
    mask_cur = c <= r
    mask_prev = jnp.logical_and(c > r, i > 0)
    ninf = jnp.float32(-jnp.inf)
    for p in range(SWA_HEADS // 2):
        g = (2 * p) // (SWA_HEADS // SWA_KV_HEADS)
        kc = kc_ref[:, g * LANES:(g + 1) * LANES]
        kp = kp_ref[:, g * LANES:(g + 1) * LANES]
        vc = vc_ref[:, g * LANES:(g + 1) * LANES]
        vp = vp_ref[:, g * LANES:(g + 1) * LANES]
        qb = q_ref[:, p * LANES:(p + 1) * LANES]
        outs = []
        for hh in range(2):
            qm = _keep_half(qb, hh)
            s_c = jnp.where(mask_cur, _dot_nt(qm, kc), ninf)
            s_p = jnp.where(mask_prev, _dot_nt(qm, kp), ninf)
            sk = sink_ref[2 * p + hh]
            m = jnp.maximum(jnp.maximum(jnp.max(s_c, axis=1, keepdims=True),
                                        jnp.max(s_p, axis=1, keepdims=True)), sk)
            p_c = jnp.exp(s_c - m)
            p_p = jnp.exp(s_p - m)
            den = (jnp.sum(p_c, axis=1, keepdims=True) + jnp.sum(p_p, axis=1, keepdims=True)
                   + jnp.exp(sk - m))
            o = _dot(p_c.astype(BF16), vc) + _dot(p_p.astype(BF16), vp)
            outs.append(o / den)
        o_ref[:, p * LANES:(p + 1) * LANES] = jnp.where(_half_mask(0), outs[0], outs[1]).astype(o_ref.dtype)


def swa_attention(pa, pb, sink, batch, seq):
    nb = seq // BLOCK
    cur = lambda col: (lambda b, i: (b * nb + i, col))
    prev = lambda col: (lambda b, i: (b * nb + jnp.maximum(i - 1, 0), col))
    return pl.pallas_call(
        _swa_kernel,
        out_shape=jax.ShapeDtypeStruct((batch * seq, BRANCH_WIDTH), BF16),
        grid=(batch, nb),
        in_specs=[pl.BlockSpec(memory_space=pltpu.SMEM),
                  pl.BlockSpec((BLOCK, 512), cur(PA_AQ // 512)),
                  pl.BlockSpec((BLOCK, 256), cur(PA_AK // 256)),
                  pl.BlockSpec((BLOCK, 256), prev(PA_AK // 256)),
                  pl.BlockSpec((BLOCK, 256), cur(PB_AV // 256)),
                  pl.BlockSpec((BLOCK, 256), prev(PB_AV // 256))],
        out_specs=pl.BlockSpec((BLOCK, BRANCH_WIDTH), cur(0)),
        compiler_params=_cparams(("parallel", "arbitrary")),
        name="swa_attention",
    )(sink, pa, pa, pa, pb, pb)


DSA_TQ = 256
DSA_SUB = 128


def _dsa_kernel(q_ref, iq_ref, iw_ref, k_ref, ik_ref, v_ref, o_ref,
                skey_ref, qm_ref, thr_ref, cut_ref, m_ref, l_ref, acc_ref, *, n_sel, seq):
    tq = DSA_TQ
    i = pl.program_id(1)
    n_chunks = i + 1
    row_t = i * tq + lax.broadcasted_iota(I32, (tq, 1), 0)
    lane_col = lax.broadcasted_iota(I32, (1, tq), 1)

    iq = iq_ref[...]
    iw = iw_ref[...].astype(F32)
    qidx = [_keep_half(iq[:, (h // 2) * LANES:(h // 2 + 1) * LANES], h % 2) for h in range(IDX_HEADS)]
    widx = [iw[:, h:h + 1] for h in range(IDX_HEADS)]

    def score_chunk(j, carry):
        ikc = ik_ref[pl.ds(pl.multiple_of(j * tq, tq), tq), :]
        sc = jnp.zeros((tq, tq), F32)
        for h in range(IDX_HEADS):
            sc = sc + jnp.maximum(_dot_nt(qidx[h], ikc), 0.0) * widx[h]
        sc = jnp.where(sc == 0.0, 0.0, sc)
        bits = lax.bitcast_convert_type(sc, I32)
        key = jnp.where(bits < 0, bits ^ jnp.int32(0x7FFFFFFF), bits)
        causal = (j * tq + lane_col) <= row_t
        skey_ref[j] = jnp.where(causal, key, jnp.int32(INT_MIN))
        return carry

    lax.fori_loop(0, n_chunks, score_chunk, 0)

    def count_rows(sub, pred):
        def body(j, acc):
            kc = skey_ref[j, sub * DSA_SUB:(sub + 1) * DSA_SUB, :]
            for cb in range(tq // LANES):
                acc = acc + pred(kc[:, cb * LANES:(cb + 1) * LANES], j, cb)
            return acc
        acc = lax.fori_loop(0, n_chunks, body, jnp.zeros((DSA_SUB, LANES), F32))
        return jnp.sum(acc, axis=1, keepdims=True)

    one = jnp.float32(1.0)
    zero = jnp.float32(0.0)
    lane128 = lax.broadcasted_iota(I32, (1, LANES), 1)
    k_f = jnp.float32(n_sel)
    cut_bits = max((seq - 1).bit_length(), 1)

    for sub in range(tq // DSA_SUB):
        def bit_step(b, res_u, sub=sub):
            cand_u = res_u | lax.shift_left(jnp.int32(1), 31 - b)
            cand_s = jnp.broadcast_to(cand_u ^ jnp.int32(INT_MIN), (DSA_SUB, LANES))
            cnt = count_rows(sub, lambda kb, j, cb: jnp.where(kb >= cand_s, one, zero))
            return jnp.where(cnt >= k_f, cand_u, res_u)

        res_u = lax.fori_loop(0, 32, bit_step, jnp.zeros((DSA_SUB, 1), I32))
        thr = res_u ^ jnp.int32(INT_MIN)
        thr_b = jnp.broadcast_to(thr, (DSA_SUB, LANES))
        cnt_gt = count_rows(sub, lambda kb, j, cb: jnp.where(kb > thr_b, one, zero))
        cnt_eq = count_rows(sub, lambda kb, j, cb: jnp.where(kb == thr_b, one, zero))
        need = k_f - cnt_gt
        short = thr == jnp.int32(INT_MIN)
        rows = slice(sub * DSA_SUB, (sub + 1) * DSA_SUB)
        thr_ref[rows, :] = thr_b
        cut_ref[rows, :] = jnp.broadcast_to(jnp.where(short, jnp.int32(-1), jnp.int32(seq)), (DSA_SUB, LANES))
        unresolved = jnp.where(jnp.logical_and(cnt_eq != need, jnp.logical_not(short)), one, zero)

        @pl.when(jnp.max(unresolved) > 0.0)
        def _(sub=sub, thr_b=thr_b, need=need, short=short, rows=rows):
            def cut_step(b, ans):
                cand = ans | lax.shift_left(jnp.int32(1), cut_bits - 1 - b)
                lim = jnp.broadcast_to(cand - 1, (DSA_SUB, LANES))

                def pred(kb, j, cb):
                    col = j * tq + cb * LANES + lane128
                    return jnp.where(kb == thr_b, jnp.where(col <= lim, one, zero), zero)

                return jnp.where(count_rows(sub, pred) < need, cand, ans)

            ans = lax.fori_loop(0, cut_bits, cut_step, jnp.zeros((DSA_SUB, 1), I32))
            cut_ref[rows, :] = jnp.broadcast_to(jnp.where(short, jnp.int32(-1), ans), (DSA_SUB, LANES))

    q = q_ref[...]
    for h in range(DSA_HEADS):
        qm_ref[h] = _keep_half(q[:, (h // 2) * LANES:(h // 2 + 1) * LANES], h % 2)
    m_ref[...] = jnp.full(m_ref.shape, NEG_BIG, F32)
    l_ref[...] = jnp.zeros(l_ref.shape, F32)
    acc_ref[...] = jnp.zeros(acc_ref.shape, F32)
    thr_all = thr_ref[...]
    cut_all = cut_ref[...]
    neg = jnp.float32(NEG_BIG)

    def attend(j, carry):
        kc = skey_ref[j]
        start = pl.multiple_of(j * tq, tq)
        biases = []
        for cb in range(tq // LANES):
            kb = kc[:, cb * LANES:(cb + 1) * LANES]
            col = j * tq + cb * LANES + lane128
            tie = jnp.where(kb == thr_all, jnp.where(col <= cut_all, zero, neg), neg)
            biases.append(jnp.where(kb > thr_all, zero, tie))
        bias = jnp.concatenate(biases, axis=1)
        for h in range(DSA_HEADS):
            g = h // (DSA_HEADS // DSA_KV_HEADS)
            kblk = k_ref[pl.ds(start, tq), g * LANES:(g + 1) * LANES]
            vblk = v_ref[pl.ds(start, tq), g * LANES:(g + 1) * LANES]
            s = _dot_nt(qm_ref[h], kblk) + bias
            m_old = m_ref[h]
            m_new = jnp.maximum(m_old, jnp.max(s, axis=1, keepdims=True))
            alpha = jnp.exp(m_old - m_new)
            p = jnp.exp(s - m_new)
            l_ref[h] = alpha * l_ref[h] + jnp.sum(p, axis=1, keepdims=True)
            acc_ref[h] = alpha * acc_ref[h] + _dot(p.astype(BF16), vblk)
            m_ref[h] = m_new
        return carry

    lax.fori_loop(0, n_chunks, attend, 0)

    for p in range(DSA_HEADS // 2):
        o0 = acc_ref[2 * p] / l_ref[2 * p]
        o1 = acc_ref[2 * p + 1] / l_ref[2 * p + 1]
        o_ref[:, p * LANES:(p + 1) * LANES] = jnp.where(_half_mask(0), o0, o1).astype(o_ref.dtype)


def dsa_attention(pa, pb, batch, seq):
    tq = DSA_TQ
    nq = seq // tq
    n_sel = min(DSA_TOPK, seq // 4)
    qmap = lambda col: (lambda b, i: (b * nq + i, col))
    smap = lambda col: (lambda b, i: (b, col))
    return pl.pallas_call(
        functools.partial(_dsa_kernel, n_sel=n_sel, seq=seq),
        out_shape=jax.ShapeDtypeStruct((batch * seq, BRANCH_WIDTH), BF16),
        grid=(batch, nq),
        in_specs=[pl.BlockSpec((tq, 512), qmap(PA_BQ // 512)),
                  pl.BlockSpec((tq, 256), qmap(PA_IQ // 256)),
                  pl.BlockSpec((tq, LANES), qmap(PB_IW // LANES)),
                  pl.BlockSpec((seq, 256), smap(PA_BK // 256)),
                  pl.BlockSpec((seq, LANES), smap(PA_IK // LANES)),
                  pl.BlockSpec((seq, 256), smap(PB_BV // 256))],
        out_specs=pl.BlockSpec((tq, BRANCH_WIDTH), qmap(0)),
        scratch_shapes=[pltpu.VMEM((nq, tq, tq), I32),
                        pltpu.VMEM((DSA_HEADS, tq, LANES), BF16),
                        pltpu.VMEM((tq, LANES), I32),
                        pltpu.VMEM((tq, LANES), I32),
                        pltpu.VMEM((DSA_HEADS, tq, 1), F32),
                        pltpu.VMEM((DSA_HEADS, tq, 1), F32),
                        pltpu.VMEM((DSA_HEADS, tq, LANES), F32)],
        compiler_params=_cparams(("parallel", "arbitrary")),
        name="dsa_attention",
    )(pa, pa, pb, pa, pa, pb)


def _retention_kernel(lg_ref, q_ref, k_ref, v_ref, g_ref, o_ref, state_ref):
    cdim = RET_CHUNK

    @pl.when(pl.program_id(1) == 0)
    def _():
        state_ref[...] = jnp.zeros(state_ref.shape, F32)

    n = lax.broadcasted_iota(I32, (cdim, 1), 0).astype(F32)
    mcol = lax.broadcasted_iota(I32, (1, cdim), 1).astype(F32)
    rel = n - mcol
    for h in range(RET_HEADS):
        lg = lg_ref[h]
        decay = jnp.where(rel >= 0, jnp.exp(lg * jnp.maximum(rel, 0.0)), 0.0)
        q_decay = jnp.exp(lg * (n + 1.0))
        k_decay = jnp.exp(lg * (cdim - 1.0 - n))
        chunk_decay = jnp.exp(jnp.full((1, LANES), lg * cdim, F32))
        blk = slice((h // 2) * LANES, (h // 2 + 1) * LANES)
        qm = _keep_half(q_ref[:, blk], h % 2)
        km = _keep_half(k_ref[:, blk], h % 2)
        v = v_ref[:, h * RET_VAL_DIM:(h + 1) * RET_VAL_DIM]
        state = state_ref[h]
        inner = _dot((_dot_nt(qm, km) * decay).astype(BF16), v)
        cross = _dot(qm, state.astype(BF16)) * q_decay
        kd_t = (km.astype(F32) * k_decay).T.astype(BF16)
        state_ref[h] = state * chunk_decay + _dot(kd_t, v)
        out = inner + cross
        out = out * lax.rsqrt(jnp.mean(out * out, axis=-1, keepdims=True) + EPS)
        gate = g_ref[:, h * RET_VAL_DIM:(h + 1) * RET_VAL_DIM].astype(F32)
        silu = gate * (1.0 / (1.0 + jnp.exp(-gate)))
        o_ref[:, h * RET_VAL_DIM:(h + 1) * RET_VAL_DIM] = (silu * out).astype(o_ref.dtype)


def retention(pa, pb, log_gamma, batch, seq):
    nc = seq // RET_CHUNK
    cmap = lambda col: (lambda b, c: (b * nc + c, col))
    return pl.pallas_call(
        _retention_kernel,
        out_shape=jax.ShapeDtypeStruct((batch * seq, BRANCH_WIDTH), BF16),
        grid=(batch, nc),
        in_specs=[pl.BlockSpec(memory_space=pltpu.SMEM),
                  pl.BlockSpec((RET_CHUNK, 256), cmap(PA_RQ // 256)),
                  pl.BlockSpec((RET_CHUNK, 256), cmap(PA_RK // 256)),
                  pl.BlockSpec((RET_CHUNK, 512), cmap(PB_RV // 512)),
                  pl.BlockSpec((RET_CHUNK, 512), cmap(PB_RG // 512))],
        out_specs=pl.BlockSpec((RET_CHUNK, BRANCH_WIDTH), cmap(0)),
        scratch_shapes=[pltpu.VMEM((RET_HEADS, LANES, RET_VAL_DIM), F32)],
        compiler_params=_cparams(("parallel", "arbitrary")),
        name="retention",
    )(log_gamma, pa, pa, pb, pb)


POOL_HALO = 16


def _pool_kernel(cur_ref, prev_ref, w_ref, scale_ref, o_ref, *, tiles_per_seq):
    tm = cur_ref.shape[0]
    it = pl.program_id(0) % tiles_per_seq
    cur = cur_ref[...].astype(F32)
    prev = jnp.where(it > 0, prev_ref[...].astype(F32), 0.0)
    ext = jnp.concatenate([prev, cur], axis=0)
    t = it * tm + lax.broadcasted_iota(I32, (tm, 1), 0)
    for gi, win in enumerate(POOL_WINDOWS):
        cols = slice(gi * POOL_GROUP_DIM, (gi + 1) * POOL_GROUP_DIM)
        s = ext[:, cols]
        sh = 1
        while sh < win:
            s = s + pltpu.roll(s, sh, axis=0)
            sh *= 2
        cnt = jnp.minimum(t + 1, win).astype(F32)
        pooled = s[POOL_HALO:, :] / cnt
        y = _dot((pooled - cur[:, cols]).astype(BF16), w_ref[gi])
        o_ref[:, cols] = (y * scale_ref[:, cols]).astype(o_ref.dtype)


def multiscale_pool(pb, pool_w, pool_scale, batch, seq):
    m = batch * seq
    tm = min(seq, 512)
    tps = seq // tm
    ratio = tm // POOL_HALO
    return pl.pallas_call(
        functools.partial(_pool_kernel, tiles_per_seq=tps),
        out_shape=jax.ShapeDtypeStruct((m, BRANCH_WIDTH), BF16),
        grid=(m // tm,),
        in_specs=[pl.BlockSpec((tm, 512), lambda i: (i, PB_PU // 512)),
                  pl.BlockSpec((POOL_HALO, 512), lambda i: (jnp.maximum(i * ratio - 1, 0), PB_PU // 512)),
                  pl.BlockSpec((len(POOL_WINDOWS), POOL_GROUP_DIM, POOL_GROUP_DIM), lambda i: (0, 0, 0)),
                  pl.BlockSpec((1, 512), lambda i: (0, 0))],
        out_specs=pl.BlockSpec((tm, BRANCH_WIDTH), lambda i: (i, 0)),
        compiler_params=_cparams(("parallel",)),
        name="multiscale_pool",
    )(pb, pb, pool_w, pool_scale)


def _merge_kernel(x_ref, oa_ref, ob_ref, oc_ref, od_ref, gates_ref, wb_ref, wo_ref, o_ref):
    d = x_ref.shape[1]
    merged = jnp.zeros(x_ref.shape, F32)
    for bi, br in enumerate((oa_ref, ob_ref, oc_ref, od_ref)):
        gate = gates_ref[:, bi * d:(bi + 1) * d].astype(F32)
        merged = merged + (1.0 / (1.0 + jnp.exp(-gate))) * _dot(br[...], wb_ref[bi])
    o_ref[...] = x_ref[...] + _dot(merged.astype(BF16), wo_ref[...])


def merge_branches(x, branches, pb, w_branch, w_out):
    m, d = x.shape
    tm = min(m, 512)
    row = lambda i: (i, 0)
    return pl.pallas_call(
        _merge_kernel,
        out_shape=jax.ShapeDtypeStruct((m, d), F32),
        grid=(m // tm,),
        in_specs=[pl.BlockSpec((tm, d), row)]
        + [pl.BlockSpec((tm, BRANCH_WIDTH), row)] * N_BRANCH
        + [pl.BlockSpec((tm, N_BRANCH * d), lambda i: (i, PB_GATES // (N_BRANCH * d))),
           pl.BlockSpec((N_BRANCH, BRANCH_WIDTH, d), lambda i: (0, 0, 0)),
           pl.BlockSpec((d, d), lambda i: (0, 0))],
        out_specs=pl.BlockSpec((tm, d), row),
        compiler_params=_cparams(("parallel",)),
        name="merge_branches",
    )(x, *branches, pb, w_branch, w_out)


def _cross_kernel(x_ref, g_ref, wq_ref, k_ref, v_ref, wo_ref, o_ref):
    x = x_ref[...]
    d = x.shape[1]
    hd = d // X_HEADS
    q = _dot(_rms(x, g_ref[...]).astype(BF16), wq_ref[...]).astype(BF16)
    outs = []
    for h in range(X_HEADS):
        cols = slice(h * hd, (h + 1) * hd)
        s = _dot_nt(q[:, cols], k_ref[:, cols])
        p = jnp.exp(s - jnp.max(s, axis=1, keepdims=True))
        o = _dot(p.astype(BF16), v_ref[:, cols]) / jnp.sum(p, axis=1, keepdims=True)
        outs.append(o.astype(BF16))
    o_ref[...] = x + _dot(jnp.concatenate(outs, axis=1), wo_ref[...])


def cross_attention(x, g, wq, kv, wo, batch, seq):
    m, d = x.shape
    mem_len = kv.shape[0] // batch
    tm = min(seq, 512)
    nt = seq // tm
    return pl.pallas_call(
        _cross_kernel,
        out_shape=jax.ShapeDtypeStruct((m, d), F32),
        grid=(batch, nt),
        in_specs=[pl.BlockSpec((tm, d), lambda b, i: (b * nt + i, 0)),
                  pl.BlockSpec((1, d), lambda b, i: (0, 0)),
                  pl.BlockSpec((d, d), lambda b, i: (0, 0)),
                  pl.BlockSpec((mem_len, d), lambda b, i: (b, 0)),
                  pl.BlockSpec((mem_len, d), lambda b, i: (b, 1)),
                  pl.BlockSpec((d, d), lambda b, i: (0, 0))],
        out_specs=pl.BlockSpec((tm, d), lambda b, i: (b * nt + i, 0)),
        compiler_params=_cparams(("parallel", "arbitrary")),
        name="cross_attention",
    )(x, g, wq, kv, kv, wo)


def _ffn_kernel(x_ref, g_ref, rw_ref, wg_ref, wu_ref, wd_ref, o_ref, h_ref, acc_ref, gate_ref,
                *, routed, n_experts):
    e = pl.program_id(1)
    f = pl.program_id(2)

    @pl.when(jnp.logical_and(e == 0, f == 0))
    def _():
        x = x_ref[...]
        hn = _rms(x, g_ref[...])
        h_ref[...] = hn.astype(BF16)
        acc_ref[...] = x
        if routed:
            logits = jnp.dot(hn, rw_ref[...], precision=lax.Precision.HIGHEST, preferred_element_type=F32)
            lane = lax.broadcasted_iota(I32, (1, LANES), 1).astype(F32)
            ninf = jnp.float32(-jnp.inf)
            lg = jnp.where(lane < n_experts, logits, ninf)
            m1 = jnp.max(lg, axis=1, keepdims=True)
            i1 = jnp.min(jnp.where(lg == m1, lane, float(LANES)), axis=1, keepdims=True)
            lg2 = jnp.where(lane == i1, ninf, lg)
            m2 = jnp.max(lg2, axis=1, keepdims=True)
            i2 = jnp.min(jnp.where(lg2 == m2, lane, float(LANES)), axis=1, keepdims=True)
            e2 = jnp.exp(m2 - m1)
            den = 1.0 + e2
            gate_ref[...] = jnp.where(lane == i1, 1.0 / den, 0.0) + jnp.where(lane == i2, e2 / den, 0.0)

    h = h_ref[...]
    a = _dot(h, wg_ref[0])
    u = _dot(h, wu_ref[0])
    act = a * (1.0 / (1.0 + jnp.exp(-a))) * u
    if routed:
        lane_i = lax.broadcasted_iota(I32, (1, LANES), 1)
        act = act * jnp.sum(jnp.where(lane_i == e, gate_ref[...], 0.0), axis=1, keepdims=True)
    acc_ref[...] += _dot(act.astype(BF16), wd_ref[0])

    @pl.when(jnp.logical_and(e == pl.num_programs(1) - 1, f == pl.num_programs(2) - 1))
    def _():
        o_ref[...] = acc_ref[...]


def ffn(x, g, router_w, w_gu, w_down, routed):
    m, d = x.shape
    n_experts, _, two_f = w_gu.shape
    dff = two_f // 2
    tm = min(m, 1024)
    tf = 512
    nf = dff // tf
    return pl.pallas_call(
        functools.partial(_ffn_kernel, routed=routed, n_experts=n_experts),
        out_shape=jax.ShapeDtypeStruct((m, d), F32),
        grid=(m // tm, n_experts, nf),
        in_specs=[pl.BlockSpec((tm, d), lambda i, e, f: (i, 0)),
                  pl.BlockSpec((1, d), lambda i, e, f: (0, 0)),
                  pl.BlockSpec((d, LANES), lambda i, e, f: (0, 0)),
                  pl.BlockSpec((1, d, tf), lambda i, e, f: (e, 0, f)),
                  pl.BlockSpec((1, d, tf), lambda i, e, f: (e, 0, nf + f)),
                  pl.BlockSpec((1, tf, d), lambda i, e, f: (e, f, 0))],
        out_specs=pl.BlockSpec((tm, d), lambda i, e, f: (i, 0)),
        scratch_shapes=[pltpu.VMEM((tm, d), BF16), pltpu.VMEM((tm, d), F32), pltpu.VMEM((tm, LANES), F32)],
        compiler_params=_cparams(("parallel", "arbitrary", "arbitrary")),
        name="ffn_routed" if routed else "ffn_dense",
    )(x, g, router_w, w_gu, w_gu, w_down)


def _final_norm_kernel(x_ref, g_ref, o_ref):
    o_ref[...] = _rms(x_ref[...], g_ref[...])


def final_norm(x, g):
    m, d = x.shape
    tm = min(m, 1024)
    return pl.pallas_call(
        _final_norm_kernel,
        out_shape=jax.ShapeDtypeStruct((m, d), F32),
        grid=(m // tm,),
        in_specs=[pl.BlockSpec((tm, d), lambda i: (i, 0)), pl.BlockSpec((1, d), lambda i: (0, 0))],
        out_specs=pl.BlockSpec((tm, d), lambda i: (i, 0)),
        compiler_params=_cparams(("parallel",)),
        name="final_norm",
    )(x, g)


def _rotate_half_cols(w):
    d = w.shape[0]
    w4 = w.reshape(d, -1, 2, HEAD_DIM // 2)
    return jnp.stack([-w4[:, :, 1], w4[:, :, 0]], axis=2).reshape(d, -1)


def _dup_heads(w):
    d = w.shape[0]
    w3 = w.reshape(d, -1, 1, HEAD_DIM)
    return jnp.concatenate([w3, w3], axis=2).reshape(d, -1)


def _split_w_in(w_in):
    sizes = (SWA_HEADS * HEAD_DIM, SWA_KV_HEADS * HEAD_DIM, SWA_KV_HEADS * HEAD_DIM,
             DSA_HEADS * HEAD_DIM, DSA_KV_HEADS * HEAD_DIM, DSA_KV_HEADS * HEAD_DIM,
             IDX_HEADS * IDX_DIM, IDX_DIM, IDX_HEADS,
             RET_HEADS * RET_KEY_DIM, RET_HEADS * RET_KEY_DIM, RET_HEADS * RET_VAL_DIM, RET_HEADS * RET_VAL_DIM,
             len(POOL_WINDOWS) * POOL_GROUP_DIM, N_BRANCH * w_in.shape[0])
    parts, off = [], 0
    for s in sizes:
        parts.append(w_in[:, off:off + s])
        off += s
    return parts


def _layout_w_in(w_in):
    d = w_in.shape[0]
    (a_q, a_k, a_v, b_q, b_k, b_v, i_q, i_k, i_w, r_q, r_k, r_v, r_g, p_u, gates) = _split_w_in(w_in)
    qs = HEAD_DIM ** -0.5
    rope_cols = [a_q * qs, b_q * qs, i_q * (IDX_DIM ** -0.5), r_q, r_k * (RET_KEY_DIM ** -0.5),
                 _dup_heads(a_k), _dup_heads(b_k), _dup_heads(i_k)]
    wx = jnp.concatenate(rope_cols, axis=1)
    wp = jnp.concatenate([_rotate_half_cols(c) for c in rope_cols], axis=1)
    pad = PA_COLS - wx.shape[1]
    wx = jnp.pad(wx, ((0, 0), (0, pad)))
    wp = jnp.pad(wp, ((0, 0), (0, pad)))
    nt = PA_COLS // PA_TN
    w_rope = jnp.concatenate([wx.reshape(d, nt, 1, PA_TN), wp.reshape(d, nt, 1, PA_TN)], axis=2)
    w_rope = w_rope.reshape(d, 2 * PA_COLS).astype(BF16)
    iw_pad = jnp.pad(i_w * (IDX_HEADS ** -0.5), ((0, 0), (0, LANES - IDX_HEADS)))
    w_plain = jnp.concatenate([gates, r_v, r_g, p_u, _dup_heads(a_v), _dup_heads(b_v), iw_pad], axis=1)
    return w_rope, w_plain.astype(BF16)


def kernel(x, mem, positions, norm_mix, w_in, attn_sink, w_branch, w_out, pool_w, pool_scale, norm_cross, norm_mem, w_xq, w_xkv, w_xo, norm_ffn, dense_w_gu, dense_w_down, router_w, moe_w_gu, moe_w_down, norm_final):
    batch, seq, d = x.shape
    depth = norm_mix.shape[0]
    m = batch * seq
    mem_len = mem.shape[1]
    xs = x.reshape(m, d)
    mem2 = mem.reshape(batch * mem_len, d)

    inv = 1.0 / (ROPE_THETA ** (jnp.arange(0, HEAD_DIM, 2, dtype=F32) / HEAD_DIM))
    inv128 = jnp.tile(inv, LANES // (HEAD_DIM // 2))[None, :]
    cos, sin = rope_tables(positions.reshape(m, 1), inv128)
    log_gamma = jnp.log1p(-(2.0 ** (-5.0 - jnp.arange(RET_HEADS, dtype=F32))))
    x_scale = (d // X_HEADS) ** -0.5

    for layer in range(depth):
        w_rope, w_plain = _layout_w_in(w_in[layer])
        pa = norm_matmul_rope(xs, norm_mix[layer][None, :], w_rope, cos, sin, tm=512, tn=PA_TN)
        pb = norm_matmul(xs, norm_mix[layer][None, :], w_plain, tm=512, tn=PB_TN, name="in_proj_plain")
        o_a = swa_attention(pa, pb, attn_sink[layer], batch, seq)
        o_b = dsa_attention(pa, pb, batch, seq)
        o_c = retention(pa, pb, log_gamma, batch, seq)
        o_d = multiscale_pool(pb, pool_w[layer].astype(BF16), pool_scale[layer][None, :], batch, seq)
        xs = merge_branches(xs, (o_a, o_b, o_c, o_d), pb, w_branch[layer].astype(BF16), w_out[layer].astype(BF16))

        kv = norm_matmul(mem2, norm_mem[layer][None, :], w_xkv[layer].astype(BF16),
                         tm=batch * mem_len, tn=512, name="mem_kv_proj")
        xs = cross_attention(xs, norm_cross[layer][None, :], (w_xq[layer] * x_scale).astype(BF16), kv,
                             w_xo[layer].astype(BF16), batch, seq)

        if layer % 2 == 0:
            li = layer // 2
            xs = ffn(xs, norm_ffn[layer][None, :], jnp.zeros((d, LANES), F32),
                     dense_w_gu[li][None].astype(BF16), dense_w_down[li][None].astype(BF16), routed=False)
        else:
            li = layer // 2
            n_experts = router_w.shape[-1]
            rw = jnp.pad(router_w[li], ((0, 0), (0, LANES - n_experts)))
            xs = ffn(xs, norm_ffn[layer][None, :], rw,
                     moe_w_gu[li].astype(BF16), moe_w_down[li].astype(BF16), routed=True)

    return final_norm(xs, norm_final[None, :]).reshape(batch, seq, d)
```

```python
import functools

import jax
import jax.numpy as jnp
from jax import lax
from jax.experimental import pallas as pl
from jax.experimental.pallas import tpu as pltpu

F32 = jnp.float32
BF16 = jnp.bfloat16
I32 = jnp.int32

EPS = 1e-6
HEAD_DIM = 64
ROPE_THETA = 10000.0
BLOCK = 128
SWA_HEADS, SWA_KV_HEADS = 8, 2
DSA_HEADS, DSA_KV_HEADS = 8, 2
IDX_HEADS, IDX_DIM = 4, 64
DSA_TOPK = 256
RET_HEADS, RET_KEY_DIM, RET_VAL_DIM, RET_CHUNK = 4, 64, 128, 128
POOL_WINDOWS = (2, 4, 8, 16)
POOL_GROUP_DIM = 128
N_BRANCH = 4
BRANCH_WIDTH = 512
X_HEADS = 4
TOP_K = 2

LANES = 128
VMEM_LIMIT = 56 * 1024 * 1024
NEG_BIG = -1e30
INT_MIN = -2 ** 31

PA_AQ, PA_BQ, PA_IQ, PA_RQ, PA_RK, PA_AK, PA_BK, PA_IK = 0, 512, 1024, 1280, 1536, 1792, 2048, 2304
PA_COLS, PA_TN = 2560, 640
PB_GATES, PB_RV, PB_RG, PB_PU, PB_AV, PB_BV, PB_IW = 0, 4096, 4608, 5120, 5632, 5888, 6144
PB_COLS, PB_TN = 6272, 896


def _cparams(sem):
    return pltpu.CompilerParams(dimension_semantics=sem, vmem_limit_bytes=VMEM_LIMIT)


def _dot(a, b):
    return jnp.dot(a, b, preferred_element_type=F32)


def _dot_nt(a, b):
    return lax.dot_general(a, b, (((1,), (1,)), ((), ())), preferred_element_type=F32)


def _rms(x, g):
    return x * lax.rsqrt(jnp.mean(x * x, axis=-1, keepdims=True) + EPS) * g


def _half_mask(hh):
    lane = lax.broadcasted_iota(I32, (1, LANES), 1)
    return (lane >= HEAD_DIM) if hh else (lane < HEAD_DIM)


def _keep_half(block, hh):
    return jnp.where(_half_mask(hh), block, jnp.zeros_like(block))


def _rope_kernel(pos_ref, inv_ref, cos_ref, sin_ref):
    ang = pos_ref[...].astype(F32) * inv_ref[...]
    cos_ref[...] = jnp.cos(ang)
    sin_ref[...] = jnp.sin(ang)


def rope_tables(pos_col, inv128):
    m = pos_col.shape[0]
    tm = min(m, 2048)
    return pl.pallas_call(
        _rope_kernel,
        out_shape=(jax.ShapeDtypeStruct((m, LANES), F32), jax.ShapeDtypeStruct((m, LANES), F32)),
        grid=(m // tm,),
        in_specs=[pl.BlockSpec((tm, 1), lambda i: (i, 0)), pl.BlockSpec((1, LANES), lambda i: (0, 0))],
        out_specs=(pl.BlockSpec((tm, LANES), lambda i: (i, 0)), pl.BlockSpec((tm, LANES), lambda i: (i, 0))),
        compiler_params=_cparams(("parallel",)),
        name="rope_tables",
    )(pos_col, inv128)


def _norm_matmul_kernel(x_ref, g_ref, w_ref, o_ref, h_ref):
    @pl.when(pl.program_id(1) == 0)
    def _():
        h_ref[...] = _rms(x_ref[...], g_ref[...]).astype(BF16)

    o_ref[...] = _dot(h_ref[...], w_ref[...]).astype(o_ref.dtype)


def norm_matmul(x, g, w, tm, tn, out_dtype=BF16, name="norm_matmul"):
    m, k = x.shape
    n = w.shape[1]
    tm = min(tm, m)
    return pl.pallas_call(
        _norm_matmul_kernel,
        out_shape=jax.ShapeDtypeStruct((m, n), out_dtype),
        grid=(m // tm, n // tn),
        in_specs=[pl.BlockSpec((tm, k), lambda i, j: (i, 0)),
                  pl.BlockSpec((1, k), lambda i, j: (0, 0)),
                  pl.BlockSpec((k, tn), lambda i, j: (0, j))],
        out_specs=pl.BlockSpec((tm, tn), lambda i, j: (i, j)),
        scratch_shapes=[pltpu.VMEM((tm, k), BF16)],
        compiler_params=_cparams(("parallel", "arbitrary")),
        name=name,
    )(x, g, w)


def _norm_matmul_rope_kernel(x_ref, g_ref, w_ref, cos_ref, sin_ref, o_ref, h_ref):
    @pl.when(pl.program_id(1) == 0)
    def _():
        h_ref[...] = _rms(x_ref[...], g_ref[...]).astype(BF16)

    tn = o_ref.shape[1]
    acc = _dot(h_ref[...], w_ref[...])
    cos = cos_ref[...]
    sin = sin_ref[...]
    for c in range(tn // LANES):
        lo = c * LANES
        o_ref[:, lo:lo + LANES] = (acc[:, lo:lo + LANES] * cos
                                   + acc[:, tn + lo:tn + lo + LANES] * sin).astype(o_ref.dtype)


def norm_matmul_rope(x, g, w, cos, sin, tm, tn):
    m, k = x.shape
    nt = w.shape[1] // (2 * tn)
    tm = min(tm, m)
    return pl.pallas_call(
        _norm_matmul_rope_kernel,
        out_shape=jax.ShapeDtypeStruct((m, nt * tn), BF16),
        grid=(m // tm, nt),
        in_specs=[pl.BlockSpec((tm, k), lambda i, j: (i, 0)),
                  pl.BlockSpec((1, k), lambda i, j: (0, 0)),
                  pl.BlockSpec((k, 2 * tn), lambda i, j: (0, j)),
                  pl.BlockSpec((tm, LANES), lambda i, j: (i, 0)),
                  pl.BlockSpec((tm, LANES), lambda i, j: (i, 0))],
        out_specs=pl.BlockSpec((tm, tn), lambda i, j: (i, j)),
        scratch_shapes=[pltpu.VMEM((tm, k), BF16)],
        compiler_params=_cparams(("parallel", "arbitrary")),
        name="in_proj_rope",
    )(x, g, w, cos, sin)


def _swa_kernel(sink_ref, q_ref, kc_ref, kp_ref, vc_ref, vp_ref, o_ref):
    i = pl.program_id(1)
    r = lax.broadcasted_iota(I32, (BLOCK, 1), 0)
    c = lax.broadcasted_iota(I32, (1, BLOCK), 1)
    mask_cur = c <= r
    mask_prev = jnp.logical_and(c > r, i > 0)
    ninf = jnp.float32(-jnp.inf)
    for p in range(SWA_HEADS // 2):
        g = (2 * p) // (SWA_HEADS // SWA_KV_HEADS)
        kc = kc_ref[:, g * LANES:(g + 1) * LANES]
        kp = kp_ref[:, g * LANES:(g + 1) * LANES]
        vc = vc_ref[:, g * LANES:(g + 1) * LANES]
        vp = vp_ref[:, g * LANES:(g + 1) * LANES]
        qb = q_ref[:, p * LANES:(p + 1) * LANES]
        outs = []
        for hh in range(2):
            qm = _keep_half(qb, hh)
            s_c = jnp.where(mask_cur, _dot_nt(qm, kc), ninf)
            s_p = jnp.where(mask_prev, _dot_nt(qm, kp), ninf)
            sk = sink_ref[2 * p + hh]
            m = jnp.maximum(jnp.maximum(jnp.max(s_c, axis=1, keepdims=True),
                                        jnp.max(s_p, axis=1, keepdims=True)), sk)
            p_c = jnp.exp(s_c - m)
            p_p = jnp.exp(s_p - m)
            den = (jnp.sum(p_c, axis=1, keepdims=True) + jnp.sum(p_p, axis=1, keepdims=True)
                   + jnp.exp(sk - m))
            o = _dot(p_c.astype(BF16), vc) + _dot(p_p.astype(BF16), vp)
            outs.append(o / den)
        o_ref[:, p * LANES:(p + 1) * LANES] = jnp.where(_half_mask(0), outs[0], outs[1]).astype(o_ref.dtype)


def swa_attention(pa, pb, sink, batch, seq):
    nb = seq // BLOCK
    cur = lambda col: (lambda b, i: (b * nb + i, col))
    prev = lambda col: (lambda b, i: (b * nb + jnp.maximum(i - 1, 0), col))
    return pl.pallas_call(
        _swa_kernel,
        out_shape=jax.ShapeDtypeStruct((batch * seq, BRANCH_WIDTH), BF16),
        grid=(batch, nb),
        in_specs=[pl.BlockSpec(memory_space=pltpu.SMEM),
                  pl.BlockSpec((BLOCK, 512), cur(PA_AQ // 512)),
                  pl.BlockSpec((BLOCK, 256), cur(PA_AK // 256)),
                  pl.BlockSpec((BLOCK, 256), prev(PA_AK // 256)),
                  pl.BlockSpec((BLOCK, 256), cur(PB_AV // 256)),
                  pl.BlockSpec((BLOCK, 256), prev(PB_AV // 256))],
        out_specs=pl.BlockSpec((BLOCK, BRANCH_WIDTH), cur(0)),
        compiler_params=_cparams(("parallel", "arbitrary")),
        name="swa_attention",
    )(sink, pa, pa, pa, pb, pb)


DSA_TQ = 256


def _dsa_kernel(q_ref, iq_ref, iw_ref, k_ref, ik_ref, v_ref, o_ref,
                skey_ref, vt_ref, qt_ref, cut_ref, m_ref, l_ref, acc_ref, alpha_ref, bias_ref, s_ref, p_ref,
                *, n_sel, seq):
    tq = DSA_TQ
    nq = seq // tq
    i = pl.program_id(1)
    n_chunks = i + 1
    q_pos = i * tq + lax.broadcasted_iota(I32, (1, tq), 1)
    key_off = lax.broadcasted_iota(I32, (tq, 1), 0)

    @pl.when(i == 0)
    def _():
        for jj in range(nq):
            vt = v_ref[jj * tq:(jj + 1) * tq, :].astype(F32).T
            vt_ref[jj, 0:HEAD_DIM, :] = vt[0:HEAD_DIM].astype(BF16)
            vt_ref[jj, HEAD_DIM:2 * HEAD_DIM, :] = vt[2 * HEAD_DIM:3 * HEAD_DIM].astype(BF16)

    iq_t = iq_ref[...].astype(F32).T.astype(BF16)
    w_t = iw_ref[...].astype(F32).T

    def score_chunk(j, carry):
        ikc = ik_ref[pl.ds(pl.multiple_of(j * tq, tq), tq), 0:IDX_DIM]
        sc = jnp.zeros((tq, tq), F32)
        for h in range(IDX_HEADS):
            sc = sc + jnp.maximum(_dot(ikc, iq_t[h * IDX_DIM:(h + 1) * IDX_DIM, :]), 0.0) * w_t[h:h + 1, :]
        sc = jnp.where(sc == 0.0, 0.0, sc)
        bits = lax.bitcast_convert_type(sc, I32)
        key = jnp.where(bits < 0, bits ^ jnp.int32(0x7FFFFFFF), bits)
        causal = (j * tq + key_off) <= q_pos
        skey_ref[j] = jnp.where(causal, key, jnp.int32(INT_MIN))
        return carry

    lax.fori_loop(0, n_chunks, score_chunk, 0)

    def count_keys(pred):
        def body(j, acc):
            x = pred(skey_ref[j], j)
            return acc + jnp.sum(x.reshape(tq // 8, 8, tq), axis=0)
        acc = lax.fori_loop(0, n_chunks, body, jnp.zeros((8, tq), F32))
        return jnp.sum(acc, axis=0, keepdims=True)

    one = jnp.float32(1.0)
    zero = jnp.float32(0.0)
    k_f = jnp.float32(n_sel)
    cut_bits = max((seq - 1).bit_length(), 1)

    def bit_step(b, res_u):
        cand_u = res_u | lax.shift_left(jnp.int32(1), 31 - b)
        cand_s = cand_u ^ jnp.int32(INT_MIN)
        cnt = count_keys(lambda kc, j: jnp.where(kc >= cand_s, one, zero))
        return jnp.where(cnt >= k_f, cand_u, res_u)

    res_u = lax.fori_loop(0, 32, bit_step, jnp.zeros((1, tq), I32))
    thr = res_u ^ jnp.int32(INT_MIN)
    cnt_gt = count_keys(lambda kc, j: jnp.where(kc > thr, one, zero))
    cnt_eq = count_keys(lambda kc, j: jnp.where(kc == thr, one, zero))
    need = k_f - cnt_gt
    short = thr == jnp.int32(INT_MIN)
    cut_ref[...] = jnp.where(short, jnp.int32(-1), jnp.int32(seq))
    unresolved = jnp.where(jnp.logical_and(cnt_eq != need, jnp.logical_not(short)), one, zero)

    @pl.when(jnp.max(unresolved) > 0.0)
    def _():
        def cut_step(b, ans):
            cand = ans | lax.shift_left(jnp.int32(1), cut_bits - 1 - b)
            lim = cand - 1

            def pred(kc, j):
                return jnp.where(kc == thr, jnp.where((j * tq + key_off) <= lim, one, zero), zero)

            return jnp.where(count_keys(pred) < need, cand, ans)

        ans = lax.fori_loop(0, cut_bits, cut_step, jnp.zeros((1, tq), I32))
        cut_ref[...] = jnp.where(short, jnp.int32(-1), ans)

    qt_ref[...] = q_ref[...].astype(F32).T.astype(BF16)
    m_ref[...] = jnp.full(m_ref.shape, NEG_BIG, F32)
    l_ref[...] = jnp.zeros(l_ref.shape, F32)
    acc_ref[...] = jnp.zeros(acc_ref.shape, F32)
    cut = cut_ref[...]
    neg = jnp.float32(NEG_BIG)
    group = DSA_HEADS // DSA_KV_HEADS

    def attend(j, carry):
        kc = skey_ref[j]
        start = pl.multiple_of(j * tq, tq)
        tie = jnp.where(kc == thr, jnp.where((j * tq + key_off) <= cut, zero, neg), neg)
        bias_ref[...] = jnp.where(kc > thr, zero, tie)
        for h in range(DSA_HEADS):
            g = h // group
            kblk = k_ref[pl.ds(start, tq), g * LANES:g * LANES + HEAD_DIM]
            s = _dot(kblk, qt_ref[h * HEAD_DIM:(h + 1) * HEAD_DIM, :]) + bias_ref[...]
            s_ref[h] = s
            m_old = m_ref[h]
            m_new = jnp.maximum(m_old, jnp.max(s, axis=0, keepdims=True))
            alpha_ref[h] = jnp.exp(m_old - m_new)
            m_ref[h] = m_new
        for h in range(DSA_HEADS):
            p = jnp.exp(s_ref[h] - m_ref[h])
            l_ref[h] = alpha_ref[h] * l_ref[h] + jnp.sum(p, axis=0, keepdims=True)
            p_ref[h] = p.astype(BF16)
        for h in range(DSA_HEADS):
            g = h // group
            acc_ref[h] = alpha_ref[h] * acc_ref[h] + _dot(vt_ref[j, g * HEAD_DIM:(g + 1) * HEAD_DIM, :], p_ref[h])
        return carry

    lax.fori_loop(0, n_chunks, attend, 0)

    for p in range(DSA_HEADS // 2):
        o_t = jnp.concatenate([acc_ref[2 * p] / l_ref[2 * p], acc_ref[2 * p + 1] / l_ref[2 * p + 1]], axis=0)
        o_ref[:, p * LANES:(p + 1) * LANES] = o_t.T.astype(o_ref.dtype)


def dsa_attention(pa, pb, batch, seq):
    tq = DSA_TQ
    nq = seq // tq
    n_sel = min(DSA_TOPK, seq // 4)
    qmap = lambda col: (lambda b, i: (b * nq + i, col))
    smap = lambda col: (lambda b, i: (b, col))
    return pl.pallas_call(
        functools.partial(_dsa_kernel, n_sel=n_sel, seq=seq),
        out_shape=jax.ShapeDtypeStruct((batch * seq, BRANCH_WIDTH), BF16),
        grid=(batch, nq),
        in_specs=[pl.BlockSpec((tq, 512), qmap(PA_BQ // 512)),
                  pl.BlockSpec((tq, 256), qmap(PA_IQ // 256)),
                  pl.BlockSpec((tq, LANES), qmap(PB_IW // LANES)),
                  pl.BlockSpec((seq, 256), smap(PA_BK // 256)),
                  pl.BlockSpec((seq, LANES), smap(PA_IK // LANES)),
                  pl.BlockSpec((seq, 256), smap(PB_BV // 256))],
        out_specs=pl.BlockSpec((tq, BRANCH_WIDTH), qmap(0)),
        scratch_shapes=[pltpu.VMEM((nq, tq, tq), I32),
                        pltpu.VMEM((nq, 2 * HEAD_DIM, tq), BF16),
                        pltpu.VMEM((DSA_HEADS * HEAD_DIM, tq), BF16),
                        pltpu.VMEM((1, tq), I32),
                        pltpu.VMEM((DSA_HEADS, 1, tq), F32),
                        pltpu.VMEM((DSA_HEADS, 1, tq), F32),
                        pltpu.VMEM((DSA_HEADS, HEAD_DIM, tq), F32),
                        pltpu.VMEM((DSA_HEADS, 1, tq), F32),
                        pltpu.VMEM((tq, tq), F32),
                        pltpu.VMEM((DSA_HEADS, tq, tq), F32),
                        pltpu.VMEM((DSA_HEADS, tq, tq), BF16)],
        compiler_params=_cparams(("parallel", "arbitrary")),
        name="dsa_attention",
    )(pa, pa, pb, pa, pa, pb)


def _retention_kernel(lg_ref, q_ref, k_ref, v_ref, g_ref, o_ref, state_ref):
    cdim = RET_CHUNK

    @pl.when(pl.program_id(1) == 0)
    def _():
        state_ref[...] = jnp.zeros(state_ref.shape, F32)

    n = lax.broadcasted_iota(I32, (cdim, 1), 0).astype(F32)
    mcol = lax.broadcasted_iota(I32, (1, cdim), 1).astype(F32)
    rel = n - mcol
    for h in range(RET_HEADS):
        lg = lg_ref[h]
        decay = jnp.where(rel >= 0, jnp.exp(lg * jnp.maximum(rel, 0.0)), 0.0)
        q_decay = jnp.exp(lg * (n + 1.0))
        k_decay = jnp.exp(lg * (cdim - 1.0 - n))
        chunk_decay = jnp.exp(jnp.full((1, LANES), lg * cdim, F32))
        blk = slice((h // 2) * LANES, (h // 2 + 1) * LANES)
        qm = _keep_half(q_ref[:, blk], h % 2)
        km = _keep_half(k_ref[:, blk], h % 2)
        v = v_ref[:, h * RET_VAL_DIM:(h + 1) * RET_VAL_DIM]
        state = state_ref[h]
        inner = _dot((_dot_nt(qm, km) * decay).astype(BF16), v)
        cross = _dot(qm, state.astype(BF16)) * q_decay
        kd_t = (km.astype(F32) * k_decay).T.astype(BF16)
        state_ref[h] = state * chunk_decay + _dot(kd_t, v)
        out = inner + cross
        out = out * lax.rsqrt(jnp.mean(out * out, axis=-1, keepdims=True) + EPS)
        gate = g_ref[:, h * RET_VAL_DIM:(h + 1) * RET_VAL_DIM].astype(F32)
        silu = gate * (1.0 / (1.0 + jnp.exp(-gate)))
        o_ref[:, h * RET_VAL_DIM:(h + 1) * RET_VAL_DIM] = (silu * out).astype(o_ref.dtype)


def retention(pa, pb, log_gamma, batch, seq):
    nc = seq // RET_CHUNK
    cmap = lambda col: (lambda b, c: (b * nc + c, col))
    return pl.pallas_call(
        _retention_kernel,
        out_shape=jax.ShapeDtypeStruct((batch * seq, BRANCH_WIDTH), BF16),
        grid=(batch, nc),
        in_specs=[pl.BlockSpec(memory_space=pltpu.SMEM),
                  pl.BlockSpec((RET_CHUNK, 256), cmap(PA_RQ // 256)),
                  pl.BlockSpec((RET_CHUNK, 256), cmap(PA_RK // 256)),
                  pl.BlockSpec((RET_CHUNK, 512), cmap(PB_RV // 512)),
                  pl.BlockSpec((RET_CHUNK, 512), cmap(PB_RG // 512))],
        out_specs=pl.BlockSpec((RET_CHUNK, BRANCH_WIDTH), cmap(0)),
        scratch_shapes=[pltpu.VMEM((RET_HEADS, LANES, RET_VAL_DIM), F32)],
        compiler_params=_cparams(("parallel", "arbitrary")),
        name="retention",
    )(log_gamma, pa, pa, pb, pb)


POOL_HALO = 16


def _pool_kernel(cur_ref, prev_ref, w_ref, scale_ref, o_ref, *, tiles_per_seq):
    tm = cur_ref.shape[0]
    it = pl.program_id(0) % tiles_per_seq
    cur = cur_ref[...].astype(F32)
    prev = jnp.where(it > 0, prev_ref[...].astype(F32), 0.0)
    ext = jnp.concatenate([prev, cur], axis=0)
    t = it * tm + lax.broadcasted_iota(I32, (tm, 1), 0)
    for gi, win in enumerate(POOL_WINDOWS):
        cols = slice(gi * POOL_GROUP_DIM, (gi + 1) * POOL_GROUP_DIM)
        s = ext[:, cols]
        sh = 1
        while sh < win:
            s = s + pltpu.roll(s, sh, axis=0)
            sh *= 2
        cnt = jnp.minimum(t + 1, win).astype(F32)
        pooled = s[POOL_HALO:, :] / cnt
        y = _dot((pooled - cur[:, cols]).astype(BF16), w_ref[gi])
        o_ref[:, cols] = (y * scale_ref[:, cols]).astype(o_ref.dtype)


def multiscale_pool(pb, pool_w, pool_scale, batch, seq):
    m = batch * seq
    tm = min(seq, 512)
    tps = seq // tm
    ratio = tm // POOL_HALO
    return pl.pallas_call(
        functools.partial(_pool_kernel, tiles_per_seq=tps),
        out_shape=jax.ShapeDtypeStruct((m, BRANCH_WIDTH), BF16),
        grid=(m // tm,),
        in_specs=[pl.BlockSpec((tm, 512), lambda i: (i, PB_PU // 512)),
                  pl.BlockSpec((POOL_HALO, 512), lambda i: (jnp.maximum(i * ratio - 1, 0), PB_PU // 512)),
                  pl.BlockSpec((len(POOL_WINDOWS), POOL_GROUP_DIM, POOL_GROUP_DIM), lambda i: (0, 0, 0)),
                  pl.BlockSpec((1, 512), lambda i: (0, 0))],
        out_specs=pl.BlockSpec((tm, BRANCH_WIDTH), lambda i: (i, 0)),
        compiler_params=_cparams(("parallel",)),
        name="multiscale_pool",
    )(pb, pb, pool_w, pool_scale)


def _merge_kernel(x_ref, oa_ref, ob_ref, oc_ref, od_ref, gates_ref, wb_ref, wo_ref, o_ref):
    d = x_ref.shape[1]
    merged = jnp.zeros(x_ref.shape, F32)
    for bi, br in enumerate((oa_ref, ob_ref, oc_ref, od_ref)):
        gate = gates_ref[:, bi * d:(bi + 1) * d].astype(F32)
        merged = merged + (1.0 / (1.0 + jnp.exp(-gate))) * _dot(br[...], wb_ref[bi])
    o_ref[...] = x_ref[...] + _dot(merged.astype(BF16), wo_ref[...])


def merge_branches(x, branches, pb, w_branch, w_out):
    m, d = x.shape
    tm = min(m, 512)
    row = lambda i: (i, 0)
    return pl.pallas_call(
        _merge_kernel,
        out_shape=jax.ShapeDtypeStruct((m, d), F32),
        grid=(m // tm,),
        in_specs=[pl.BlockSpec((tm, d), row)]
        + [pl.BlockSpec((tm, BRANCH_WIDTH), row)] * N_BRANCH
        + [pl.BlockSpec((tm, N_BRANCH * d), lambda i: (i, PB_GATES // (N_BRANCH * d))),
           pl.BlockSpec((N_BRANCH, BRANCH_WIDTH, d), lambda i: (0, 0, 0)),
           pl.BlockSpec((d, d), lambda i: (0, 0))],
        out_specs=pl.BlockSpec((tm, d), row),
        compiler_params=_cparams(("parallel",)),
        name="merge_branches",
    )(x, *branches, pb, w_branch, w_out)


def _cross_kernel(x_ref, g_ref, wq_ref, k_ref, v_ref, wo_ref, o_ref):
    x = x_ref[...]
    d = x.shape[1]
    hd = d // X_HEADS
    q = _dot(_rms(x, g_ref[...]).astype(BF16), wq_ref[...]).astype(BF16)
    outs = []
    for h in range(X_HEADS):
        cols = slice(h * hd, (h + 1) * hd)
        s = _dot_nt(q[:, cols], k_ref[:, cols])
        p = jnp.exp(s - jnp.max(s, axis=1, keepdims=True))
        o = _dot(p.astype(BF16), v_ref[:, cols]) / jnp.sum(p, axis=1, keepdims=True)
        outs.append(o.astype(BF16))
    o_ref[...] = x + _dot(jnp.concatenate(outs, axis=1), wo_ref[...])


def cross_attention(x, g, wq, kv, wo, batch, seq):
    m, d = x.shape
    mem_len = kv.shape[0] // batch
    tm = min(seq, 512)
    nt = seq // tm
    return pl.pallas_call(
        _cross_kernel,
        out_shape=jax.ShapeDtypeStruct((m, d), F32),
        grid=(batch, nt),
        in_specs=[pl.BlockSpec((tm, d), lambda b, i: (b * nt + i, 0)),
                  pl.BlockSpec((1, d), lambda b, i: (0, 0)),
                  pl.BlockSpec((d, d), lambda b, i: (0, 0)),
                  pl.BlockSpec((mem_len, d), lambda b, i: (b, 0)),
                  pl.BlockSpec((mem_len, d), lambda b, i: (b, 1)),
                  pl.BlockSpec((d, d), lambda b, i: (0, 0))],
        out_specs=pl.BlockSpec((tm, d), lambda b, i: (b * nt + i, 0)),
        compiler_params=_cparams(("parallel", "arbitrary")),
        name="cross_attention",
    )(x, g, wq, kv, kv, wo)


def _ffn_kernel(x_ref, g_ref, rw_ref, wg_ref, wu_ref, wd_ref, o_ref, h_ref, acc_ref, gate_ref,
                *, routed, n_experts):
    e = pl.program_id(1)
    f = pl.program_id(2)

    @pl.when(jnp.logical_and(e == 0, f == 0))
    def _():
        x = x_ref[...]
        hn = _rms(x, g_ref[...])
        h_ref[...] = hn.astype(BF16)
        acc_ref[...] = x
        if routed:
            logits = jnp.dot(hn, rw_ref[...], precision=lax.Precision.HIGHEST, preferred_element_type=F32)
            lane = lax.broadcasted_iota(I32, (1, LANES), 1).astype(F32)
            ninf = jnp.float32(-jnp.inf)
            lg = jnp.where(lane < n_experts, logits, ninf)
            m1 = jnp.max(lg, axis=1, keepdims=True)
            i1 = jnp.min(jnp.where(lg == m1, lane, float(LANES)), axis=1, keepdims=True)
            lg2 = jnp.where(lane == i1, ninf, lg)
            m2 = jnp.max(lg2, axis=1, keepdims=True)
            i2 = jnp.min(jnp.where(lg2 == m2, lane, float(LANES)), axis=1, keepdims=True)
            e2 = jnp.exp(m2 - m1)
            den = 1.0 + e2
            gate_ref[...] = jnp.where(lane == i1, 1.0 / den, 0.0) + jnp.where(lane == i2, e2 / den, 0.0)

    h = h_ref[...]
    a = _dot(h, wg_ref[0])
    u = _dot(h, wu_ref[0])
    act = a * (1.0 / (1.0 + jnp.exp(-a))) * u
    if routed:
        lane_i = lax.broadcasted_iota(I32, (1, LANES), 1)
        act = act * jnp.sum(jnp.where(lane_i == e, gate_ref[...], 0.0), axis=1, keepdims=True)
    acc_ref[...] += _dot(act.astype(BF16), wd_ref[0])

    @pl.when(jnp.logical_and(e == pl.num_programs(1) - 1, f == pl.num_programs(2) - 1))
    def _():
        o_ref[...] = acc_ref[...]


def ffn(x, g, router_w, w_gu, w_down, routed):
    m, d = x.shape
    n_experts, _, two_f = w_gu.shape
    dff = two_f // 2
    tm = min(m, 1024)
    tf = 512
    nf = dff // tf
    return pl.pallas_call(
        functools.partial(_ffn_kernel, routed=routed, n_experts=n_experts),
        out_shape=jax.ShapeDtypeStruct((m, d), F32),
        grid=(m // tm, n_experts, nf),
        in_specs=[pl.BlockSpec((tm, d), lambda i, e, f: (i, 0)),
                  pl.BlockSpec((1, d), lambda i, e, f: (0, 0)),
                  pl.BlockSpec((d, LANES), lambda i, e, f: (0, 0)),
                  pl.BlockSpec((1, d, tf), lambda i, e, f: (e, 0, f)),
                  pl.BlockSpec((1, d, tf), lambda i, e, f: (e, 0, nf + f)),
                  pl.BlockSpec((1, tf, d), lambda i, e, f: (e, f, 0))],
        out_specs=pl.BlockSpec((tm, d), lambda i, e, f: (i, 0)),
        scratch_shapes=[pltpu.VMEM((tm, d), BF16), pltpu.VMEM((tm, d), F32), pltpu.VMEM((tm, LANES), F32)],
        compiler_params=_cparams(("parallel", "arbitrary", "arbitrary")),
        name="ffn_routed" if routed else "ffn_dense",
    )(x, g, router_w, w_gu, w_gu, w_down)


def _final_norm_kernel(x_ref, g_ref, o_ref):
    o_ref[...] = _rms(x_ref[...], g_ref[...])


def final_norm(x, g):
    m, d = x.shape
    tm = min(m, 1024)
    return pl.pallas_call(
        _final_norm_kernel,
        out_shape=jax.ShapeDtypeStruct((m, d), F32),
        grid=(m // tm,),
        in_specs=[pl.BlockSpec((tm, d), lambda i: (i, 0)), pl.BlockSpec((1, d), lambda i: (0, 0))],
        out_specs=pl.BlockSpec((tm, d), lambda i: (i, 0)),
        compiler_params=_cparams(("parallel",)),
        name="final_norm",
    )(x, g)


def _rotate_half_cols(w):
    d = w.shape[0]
    w4 = w.reshape(d, -1, 2, HEAD_DIM // 2)
    return jnp.stack([-w4[:, :, 1], w4[:, :, 0]], axis=2).reshape(d, -1)


def _dup_heads(w):
    d = w.shape[0]
    w3 = w.reshape(d, -1, 1, HEAD_DIM)
    return jnp.concatenate([w3, w3], axis=2).reshape(d, -1)


def _split_w_in(w_in):
    sizes = (SWA_HEADS * HEAD_DIM, SWA_KV_HEADS * HEAD_DIM, SWA_KV_HEADS * HEAD_DIM,
             DSA_HEADS * HEAD_DIM, DSA_KV_HEADS * HEAD_DIM, DSA_KV_HEADS * HEAD_DIM,
             IDX_HEADS * IDX_DIM, IDX_DIM, IDX_HEADS,
             RET_HEADS * RET_KEY_DIM, RET_HEADS * RET_KEY_DIM, RET_HEADS * RET_VAL_DIM, RET_HEADS * RET_VAL_DIM,
             len(POOL_WINDOWS) * POOL_GROUP_DIM, N_BRANCH * w_in.shape[0])
    parts, off = [], 0
    for s in sizes:
        parts.append(w_in[:, off:off + s])
        off += s
    return parts


def _layout_w_in(w_in):
    d = w_in.shape[0]
    (a_q, a_k, a_v, b_q, b_k, b_v, i_q, i_k, i_w, r_q, r_k, r_v, r_g, p_u, gates) = _split_w_in(w_in)
    qs = HEAD_DIM ** -0.5
    rope_cols = [a_q * qs, b_q * qs, i_q * (IDX_DIM ** -0.5), r_q, r_k * (RET_KEY_DIM ** -0.5),
                 _dup_heads(a_k), _dup_heads(b_k), _dup_heads(i_k)]
    wx = jnp.concatenate(rope_cols, axis=1)
    wp = jnp.concatenate([_rotate_half_cols(c) for c in rope_cols], axis=1)
    pad = PA_COLS - wx.shape[1]
    wx = jnp.pad(wx, ((0, 0), (0, pad)))
    wp = jnp.pad(wp, ((0, 0), (0, pad)))
    nt = PA_COLS // PA_TN
    w_rope = jnp.concatenate([wx.reshape(d, nt, 1, PA_TN), wp.reshape(d, nt, 1, PA_TN)], axis=2)
    w_rope = w_rope.reshape(d, 2 * PA_COLS).astype(BF16)
    iw_pad = jnp.pad(i_w * (IDX_HEADS ** -0.5), ((0, 0), (0, LANES - IDX_HEADS)))
    w_plain = jnp.concatenate([gates, r_v, r_g, p_u, _dup_heads(a_v), _dup_heads(b_v), iw_pad], axis=1)
    return w_rope, w_plain.astype(BF16)


def kernel(x, mem, positions, norm_mix, w_in, attn_sink, w_branch, w_out, pool_w, pool_scale, norm_cross, norm_mem, w_xq, w_xkv, w_xo, norm_ffn, dense_w_gu, dense_w_down, router_w, moe_w_gu, moe_w_down, norm_final):
    batch, seq, d = x.shape
    depth = norm_mix.shape[0]
    m = batch * seq
    mem_len = mem.shape[1]
    xs = x.reshape(m, d)
    mem2 = mem.reshape(batch * mem_len, d)

    inv = 1.0 / (ROPE_THETA ** (jnp.arange(0, HEAD_DIM, 2, dtype=F32) / HEAD_DIM))
    inv128 = jnp.tile(inv, LANES // (HEAD_DIM // 2))[None, :]
    cos, sin = rope_tables(positions.reshape(m, 1), inv128)
    log_gamma = jnp.log1p(-(2.0 ** (-5.0 - jnp.arange(RET_HEADS, dtype=F32))))
    x_scale = (d // X_HEADS) ** -0.5

    for layer in range(depth):
        w_rope, w_plain = _layout_w_in(w_in[layer])
        pa = norm_matmul_rope(xs, norm_mix[layer][None, :], w_rope, cos, sin, tm=512, tn=PA_TN)
        pb = norm_matmul(xs, norm_mix[layer][None, :], w_plain, tm=512, tn=PB_TN, name="in_proj_plain")
        o_a = swa_attention(pa, pb, attn_sink[layer], batch, seq)
        o_b = dsa_attention(pa, pb, batch, seq)
        o_c = retention(pa, pb, log_gamma, batch, seq)
        o_d = multiscale_pool(pb, pool_w[layer].astype(BF16), pool_scale[layer][None, :], batch, seq)
        xs = merge_branches(xs, (o_a, o_b, o_c, o_d), pb, w_branch[layer].astype(BF16), w_out[layer].astype(BF16))

        kv = norm_matmul(mem2, norm_mem[layer][None, :], w_xkv[layer].astype(BF16),
                         tm=batch * mem_len, tn=512, name="mem_kv_proj")
        xs = cross_attention(xs, norm_cross[layer][None, :], (w_xq[layer] * x_scale).astype(BF16), kv,
                             w_xo[layer].astype(BF16), batch, seq)

        if layer % 2 == 0:
            li = layer // 2
            xs = ffn(xs, norm_ffn[layer][None, :], jnp.zeros((d, LANES), F32),
                     dense_w_gu[li][None].astype(BF16), dense_w_down[li][None].astype(BF16), routed=False)
        else:
            li = layer // 2
            n_experts = router_w.shape[-1]
            rw = jnp.pad(router_w[li], ((0, 0), (0, LANES - n_experts)))
            xs = ffn(xs, norm_ffn[layer][None, :], rw,
                     moe_w_gu[li].astype(BF16), moe_w_down[li].astype(BF16), routed=True)

    return final_norm(xs, norm_final[None, :]).reshape(batch, seq, d)
```

```python
import functools

import jax
import jax.numpy as jnp
from jax import lax
from jax.experimental import pallas as pl
from jax.experimental.pallas import tpu as pltpu

F32 = jnp.float32
BF16 = jnp.bfloat16
I32 = jnp.int32

EPS = 1e-6
HEAD_DIM = 64
ROPE_THETA = 10000.0
BLOCK = 128
SWA_HEADS, SWA_KV_HEADS = 8, 2
DSA_HEADS, DSA_KV_HEADS = 8, 2
IDX_HEADS, IDX_DIM = 4, 64
DSA_TOPK = 256
RET_HEADS, RET_KEY_DIM, RET_VAL_DIM, RET_CHUNK = 4, 64, 128, 128
POOL_WINDOWS = (2, 4, 8, 16)
POOL_GROUP_DIM = 128
N_BRANCH = 4
BRANCH_WIDTH = 512
X_HEADS = 4
TOP_K = 2

LANES = 128
VMEM_LIMIT = 56 * 1024 * 1024
NEG_BIG = -1e30
INT_MIN = -2 ** 31

PA_AQ, PA_BQ, PA_IQ, PA_RQ, PA_RK, PA_AK, PA_BK, PA_IK = 0, 512, 1024, 1280, 1536, 1792, 2048, 2304
PA_COLS, PA_TN = 2560, 640
PB_GATES, PB_RV, PB_RG, PB_PU, PB_AV, PB_BV, PB_IW = 0, 4096, 4608, 5120, 5632, 5888, 6144
PB_COLS, PB_TN = 6272, 896


def _cparams(sem):
    return pltpu.CompilerParams(dimension_semantics=sem, vmem_limit_bytes=VMEM_LIMIT)


def _dot(a, b):
    return jnp.dot(a, b, preferred_element_type=F32)


def _dot_nt(a, b):
    return lax.dot_general(a, b, (((1,), (1,)), ((), ())), preferred_element_type=F32)


def _rms(x, g):
    return x * lax.rsqrt(jnp.mean(x * x, axis=-1, keepdims=True) + EPS) * g


def _half_mask(hh):
    lane = lax.broadcasted_iota(I32, (1, LANES), 1)
    return (lane >= HEAD_DIM) if hh else (lane < HEAD_DIM)


def _keep_half(block, hh):
    return jnp.where(_half_mask(hh), block, jnp.zeros_like(block))


def _rope_kernel(pos_ref, inv_ref, cos_ref, sin_ref):
    ang = pos_ref[...].astype(F32) * inv_ref[...]
    cos_ref[...] = jnp.cos(ang)
    sin_ref[...] = jnp.sin(ang)


def rope_tables(pos_col, inv128):
    m = pos_col.shape[0]
    tm = min(m, 2048)
    return pl.pallas_call(
        _rope_kernel,
        out_shape=(jax.ShapeDtypeStruct((m, LANES), F32), jax.ShapeDtypeStruct((m, LANES), F32)),
        grid=(m // tm,),
        in_specs=[pl.BlockSpec((tm, 1), lambda i: (i, 0)), pl.BlockSpec((1, LANES), lambda i: (0, 0))],
        out_specs=(pl.BlockSpec((tm, LANES), lambda i: (i, 0)), pl.BlockSpec((tm, LANES), lambda i: (i, 0))),
        compiler_params=_cparams(("parallel",)),
        name="rope_tables",
    )(pos_col, inv128)


def _norm_matmul_kernel(x_ref, g_ref, w_ref, o_ref, h_ref):
    @pl.when(pl.program_id(1) == 0)
    def _():
        h_ref[...] = _rms(x_ref[...], g_ref[...]).astype(BF16)

    o_ref[...] = _dot(h_ref[...], w_ref[...]).astype(o_ref.dtype)


def norm_matmul(x, g, w, tm, tn, out_dtype=BF16, name="norm_matmul"):
    m, k = x.shape
    n = w.shape[1]
    tm = min(tm, m)
    return pl.pallas_call(
        _norm_matmul_kernel,
        out_shape=jax.ShapeDtypeStruct((m, n), out_dtype),
        grid=(m // tm, n // tn),
        in_specs=[pl.BlockSpec((tm, k), lambda i, j: (i, 0)),
                  pl.BlockSpec((1, k), lambda i, j: (0, 0)),
                  pl.BlockSpec((k, tn), lambda i, j: (0, j))],
        out_specs=pl.BlockSpec((tm, tn), lambda i, j: (i, j)),
        scratch_shapes=[pltpu.VMEM((tm, k), BF16)],
        compiler_params=_cparams(("parallel", "arbitrary")),
        name=name,
    )(x, g, w)


def _norm_matmul_rope_kernel(x_ref, g_ref, w_ref, cos_ref, sin_ref, o_ref, h_ref):
    @pl.when(pl.program_id(1) == 0)
    def _():
        h_ref[...] = _rms(x_ref[...], g_ref[...]).astype(BF16)

    tn = o_ref.shape[1]
    acc = _dot(h_ref[...], w_ref[...])
    cos = cos_ref[...]
    sin = sin_ref[...]
    for c in range(tn // LANES):
        lo = c * LANES
        o_ref[:, lo:lo + LANES] = (acc[:, lo:lo + LANES] * cos
                                   + acc[:, tn + lo:tn + lo + LANES] * sin).astype(o_ref.dtype)


def norm_matmul_rope(x, g, w, cos, sin, tm, tn):
    m, k = x.shape
    nt = w.shape[1] // (2 * tn)
    tm = min(tm, m)
    return pl.pallas_call(
        _norm_matmul_rope_kernel,
        out_shape=jax.ShapeDtypeStruct((m, nt * tn), BF16),
        grid=(m // tm, nt),
        in_specs=[pl.BlockSpec((tm, k), lambda i, j: (i, 0)),
                  pl.BlockSpec((1, k), lambda i, j: (0, 0)),
                  pl.BlockSpec((k, 2 * tn), lambda i, j: (0, j)),
                  pl.BlockSpec((tm, LANES), lambda i, j: (i, 0)),
                  pl.BlockSpec((tm, LANES), lambda i, j: (i, 0))],
        out_specs=pl.BlockSpec((tm, tn), lambda i, j: (i, j)),
        scratch_shapes=[pltpu.VMEM((tm, k), BF16)],
        compiler_params=_cparams(("parallel", "arbitrary")),
        name="in_proj_rope",
    )(x, g, w, cos, sin)


def _swa_kernel(sink_ref, q_ref, kc_ref, kp_ref, vc_ref, vp_ref, o_ref):
    i = pl.program_id(1)
    r = lax.broadcasted_iota(I32, (BLOCK, 1), 0)
    c = lax.broadcasted_iota(I32, (1, BLOCK), 1)
    mask_cur = c <= r
    mask_prev = jnp.logical_and(c > r, i > 0)
    ninf = jnp.float32(-jnp.inf)
    for p in range(SWA_HEADS // 2):
        g = (2 * p) // (SWA_HEADS // SWA_KV_HEADS)
        kc = kc_ref[:, g * LANES:(g + 1) * LANES]
        kp = kp_ref[:, g * LANES:(g + 1) * LANES]
        vc = vc_ref[:, g * LANES:(g + 1) * LANES]
        vp = vp_ref[:, g * LANES:(g + 1) * LANES]
        qb = q_ref[:, p * LANES:(p + 1) * LANES]
        outs = []
        for hh in range(2):
            qm = _keep_half(qb, hh)
            s_c = jnp.where(mask_cur, _dot_nt(qm, kc), ninf)
            s_p = jnp.where(mask_prev, _dot_nt(qm, kp), ninf)
            sk = sink_ref[2 * p + hh]
            m = jnp.maximum(jnp.maximum(jnp.max(s_c, axis=1, keepdims=True),
                                        jnp.max(s_p, axis=1, keepdims=True)), sk)
            p_c = jnp.exp(s_c - m)
            p_p = jnp.exp(s_p - m)
            den = (jnp.sum(p_c, axis=1, keepdims=True) + jnp.sum(p_p, axis=1, keepdims=True)
                   + jnp.exp(sk - m))
            o = _dot(p_c.astype(BF16), vc) + _dot(p_p.astype(BF16), vp)
            outs.append(o / den)
        o_ref[:, p * LANES:(p + 1) * LANES] = jnp.where(_half_mask(0), outs[0], outs[1]).astype(o_ref.dtype)


def swa_attention(pa, pb, sink, batch, seq):
    nb = seq // BLOCK
    cur = lambda col: (lambda b, i: (b * nb + i, col))
    prev = lambda col: (lambda b, i: (b * nb + jnp.maximum(i - 1, 0), col))
    return pl.pallas_call(
        _swa_kernel,
        out_shape=jax.ShapeDtypeStruct((batch * seq, BRANCH_WIDTH), BF16),
        grid=(batch, nb),
        in_specs=[pl.BlockSpec(memory_space=pltpu.SMEM),
                  pl.BlockSpec((BLOCK, 512), cur(PA_AQ // 512)),
                  pl.BlockSpec((BLOCK, 256), cur(PA_AK // 256)),
                  pl.BlockSpec((BLOCK, 256), prev(PA_AK // 256)),
                  pl.BlockSpec((BLOCK, 256), cur(PB_AV // 256)),
                  pl.BlockSpec((BLOCK, 256), prev(PB_AV // 256))],
        out_specs=pl.BlockSpec((BLOCK, BRANCH_WIDTH), cur(0)),
        compiler_params=_cparams(("parallel", "arbitrary")),
        name="swa_attention",
    )(sink, pa, pa, pa, pb, pb)


DSA_TQ = 256


def _dsa_kernel(q_ref, iq_ref, iw_ref, k_ref, ik_ref, v_ref, o_ref,
                skey_ref, vt_ref, qt_ref, cut_ref, m_ref, l_ref, acc_ref, alpha_ref, bias_ref, s_ref, p_ref,
                *, n_sel, seq):
    tq = DSA_TQ
    nq = seq // tq
    i = pl.program_id(1)
    n_chunks = i + 1
    q_pos = i * tq + lax.broadcasted_iota(I32, (1, tq), 1)
    key_off = lax.broadcasted_iota(I32, (tq, 1), 0)

    @pl.when(i == 0)
    def _():
        for jj in range(nq):
            vt = v_ref[jj * tq:(jj + 1) * tq, :].astype(F32).T
            vt_ref[jj, 0:HEAD_DIM, :] = vt[0:HEAD_DIM].astype(BF16)
            vt_ref[jj, HEAD_DIM:2 * HEAD_DIM, :] = vt[2 * HEAD_DIM:3 * HEAD_DIM].astype(BF16)

    iq_t = iq_ref[...].astype(F32).T.astype(BF16)
    w_t = iw_ref[...].astype(F32).T

    def score_chunk(j, carry):
        ikc = ik_ref[pl.ds(pl.multiple_of(j * tq, tq), tq), 0:IDX_DIM]
        sc = jnp.zeros((tq, tq), F32)
        for h in range(IDX_HEADS):
            sc = sc + jnp.maximum(_dot(ikc, iq_t[h * IDX_DIM:(h + 1) * IDX_DIM, :]), 0.0) * w_t[h:h + 1, :]
        sc = jnp.where(sc == 0.0, 0.0, sc)
        bits = lax.bitcast_convert_type(sc, I32)
        key = jnp.where(bits < 0, bits ^ jnp.int32(0x7FFFFFFF), bits)
        causal = (j * tq + key_off) <= q_pos
        skey_ref[j] = jnp.where(causal, key, jnp.int32(INT_MIN))
        return carry

    lax.fori_loop(0, n_chunks, score_chunk, 0)

    def count_keys(pred):
        def body(j, acc):
            x = pred(skey_ref[j], j)
            return acc + jnp.sum(x.reshape(tq // 8, 8, tq), axis=0)
        acc = lax.fori_loop(0, n_chunks, body, jnp.zeros((8, tq), F32))
        return jnp.sum(acc, axis=0, keepdims=True)

    one = jnp.float32(1.0)
    zero = jnp.float32(0.0)
    k_f = jnp.float32(n_sel)
    cut_bits = max((seq - 1).bit_length(), 1)

    def bit_step(b, res_u):
        cand_u = res_u | lax.shift_left(jnp.int32(1), 31 - b)
        cand_s = cand_u ^ jnp.int32(INT_MIN)
        cnt = count_keys(lambda kc, j: jnp.where(kc >= cand_s, one, zero))
        return jnp.where(cnt >= k_f, cand_u, res_u)

    res_u = lax.fori_loop(0, 32, bit_step, jnp.zeros((1, tq), I32))
    thr = res_u ^ jnp.int32(INT_MIN)
    cnt_gt = count_keys(lambda kc, j: jnp.where(kc > thr, one, zero))
    cnt_eq = count_keys(lambda kc, j: jnp.where(kc == thr, one, zero))
    need = k_f - cnt_gt
    short = thr == jnp.int32(INT_MIN)
    cut_ref[...] = jnp.where(short, jnp.int32(-1), jnp.int32(seq))
    unresolved = jnp.where(jnp.logical_and(cnt_eq != need, jnp.logical_not(short)), one, zero)

    @pl.when(jnp.max(unresolved) > 0.0)
    def _():
        def cut_step(b, ans):
            cand = ans | lax.shift_left(jnp.int32(1), cut_bits - 1 - b)
            lim = cand - 1

            def pred(kc, j):
                return jnp.where(kc == thr, jnp.where((j * tq + key_off) <= lim, one, zero), zero)

            return jnp.where(count_keys(pred) < need, cand, ans)

        ans = lax.fori_loop(0, cut_bits, cut_step, jnp.zeros((1, tq), I32))
        cut_ref[...] = jnp.where(short, jnp.int32(-1), ans)

    qt_ref[...] = q_ref[...].astype(F32).T.astype(BF16)
    m_ref[...] = jnp.full(m_ref.shape, NEG_BIG, F32)
    l_ref[...] = jnp.zeros(l_ref.shape, F32)
    acc_ref[...] = jnp.zeros(acc_ref.shape, F32)
    cut = cut_ref[...]
    neg = jnp.float32(NEG_BIG)
    group = DSA_HEADS // DSA_KV_HEADS

    def attend(j, carry):
        kc = skey_ref[j]
        start = pl.multiple_of(j * tq, tq)
        tie = jnp.where(kc == thr, jnp.where((j * tq + key_off) <= cut, zero, neg), neg)
        bias_ref[...] = jnp.where(kc > thr, zero, tie)
        for h in range(DSA_HEADS):
            g = h // group
            kblk = k_ref[pl.ds(start, tq), g * LANES:g * LANES + HEAD_DIM]
            s = _dot(kblk, qt_ref[h * HEAD_DIM:(h + 1) * HEAD_DIM, :]) + bias_ref[...]
            s_ref[h] = s
            m_old = m_ref[h]
            m_new = jnp.maximum(m_old, jnp.max(s, axis=0, keepdims=True))
            alpha_ref[h] = jnp.exp(m_old - m_new)
            m_ref[h] = m_new
        for h in range(DSA_HEADS):
            p = jnp.exp(s_ref[h] - m_ref[h])
            l_ref[h] = alpha_ref[h] * l_ref[h] + jnp.sum(p, axis=0, keepdims=True)
            p_ref[h] = p.astype(BF16)
        for h in range(DSA_HEADS):
            g = h // group
            acc_ref[h] = alpha_ref[h] * acc_ref[h] + _dot(vt_ref[j, g * HEAD_DIM:(g + 1) * HEAD_DIM, :], p_ref[h])
        return carry

    lax.fori_loop(0, n_chunks, attend, 0)

    for p in range(DSA_HEADS // 2):
        o_t = jnp.concatenate([acc_ref[2 * p] / l_ref[2 * p], acc_ref[2 * p + 1] / l_ref[2 * p + 1]], axis=0)
        o_ref[:, p * LANES:(p + 1) * LANES] = o_t.T.astype(o_ref.dtype)


def dsa_attention(pa, pb, batch, seq):
    tq = DSA_TQ
    nq = seq // tq
    n_sel = min(DSA_TOPK, seq // 4)
    qmap = lambda col: (lambda b, i: (b * nq + i, col))
    smap = lambda col: (lambda b, i: (b, col))
    return pl.pallas_call(
        functools.partial(_dsa_kernel, n_sel=n_sel, seq=seq),
        out_shape=jax.ShapeDtypeStruct((batch * seq, BRANCH_WIDTH), BF16),
        grid=(batch, nq),
        in_specs=[pl.BlockSpec((tq, 512), qmap(PA_BQ // 512)),
                  pl.BlockSpec((tq, 256), qmap(PA_IQ // 256)),
                  pl.BlockSpec((tq, LANES), qmap(PB_IW // LANES)),
                  pl.BlockSpec((seq, 256), smap(PA_BK // 256)),
                  pl.BlockSpec((seq, LANES), smap(PA_IK // LANES)),
                  pl.BlockSpec((seq, 256), smap(PB_BV // 256))],
        out_specs=pl.BlockSpec((tq, BRANCH_WIDTH), qmap(0)),
        scratch_shapes=[pltpu.VMEM((nq, tq, tq), I32),
                        pltpu.VMEM((nq, 2 * HEAD_DIM, tq), BF16),
                        pltpu.VMEM((DSA_HEADS * HEAD_DIM, tq), BF16),
                        pltpu.VMEM((1, tq), I32),
                        pltpu.VMEM((DSA_HEADS, 1, tq), F32),
                        pltpu.VMEM((DSA_HEADS, 1, tq), F32),
                        pltpu.VMEM((DSA_HEADS, HEAD_DIM, tq), F32),
                        pltpu.VMEM((DSA_HEADS, 1, tq), F32),
                        pltpu.VMEM((tq, tq), F32),
                        pltpu.VMEM((DSA_HEADS, tq, tq), F32),
                        pltpu.VMEM((DSA_HEADS, tq, tq), BF16)],
        compiler_params=_cparams(("parallel", "arbitrary")),
        name="dsa_attention",
    )(pa, pa, pb, pa, pa, pb)


def _retention_kernel(lg_ref, q_ref, k_ref, v_ref, g_ref, o_ref, state_ref):
    cdim = RET_CHUNK

    @pl.when(pl.program_id(1) == 0)
    def _():
        state_ref[...] = jnp.zeros(state_ref.shape, F32)

    n = lax.broadcasted_iota(I32, (cdim, 1), 0).astype(F32)
    mcol = lax.broadcasted_iota(I32, (1, cdim), 1).astype(F32)
    rel = n - mcol
    for h in range(RET_HEADS):
        lg = lg_ref[h]
        decay = jnp.where(rel >= 0, jnp.exp(lg * jnp.maximum(rel, 0.0)), 0.0)
        q_decay = jnp.exp(lg * (n + 1.0))
        k_decay = jnp.exp(lg * (cdim - 1.0 - n))
        chunk_decay = jnp.exp(jnp.full((1, LANES), lg * cdim, F32))
        blk = slice((h // 2) * LANES, (h // 2 + 1) * LANES)
        qm = _keep_half(q_ref[:, blk], h % 2)
        km = _keep_half(k_ref[:, blk], h % 2)
        v = v_ref[:, h * RET_VAL_DIM:(h + 1) * RET_VAL_DIM]
        state = state_ref[h]
        inner = _dot((_dot_nt(qm, km) * decay).astype(BF16), v)
        cross = _dot(qm, state.astype(BF16)) * q_decay
        kd_t = (km.astype(F32) * k_decay).T.astype(BF16)
        state_ref[h] = state * chunk_decay + _dot(kd_t, v)
        out = inner + cross
        out = out * lax.rsqrt(jnp.mean(out * out, axis=-1, keepdims=True) + EPS)
        gate = g_ref[:, h * RET_VAL_DIM:(h + 1) * RET_VAL_DIM].astype(F32)
        silu = gate * (1.0 / (1.0 + jnp.exp(-gate)))
        o_ref[:, h * RET_VAL_DIM:(h + 1) * RET_VAL_DIM] = (silu * out).astype(o_ref.dtype)


def retention(pa, pb, log_gamma, batch, seq):
    nc = seq // RET_CHUNK
    cmap = lambda col: (lambda b, c: (b * nc + c, col))
    return pl.pallas_call(
        _retention_kernel,
        out_shape=jax.ShapeDtypeStruct((batch * seq, BRANCH_WIDTH), BF16),
        grid=(batch, nc),
        in_specs=[pl.BlockSpec(memory_space=pltpu.SMEM),
                  pl.BlockSpec((RET_CHUNK, 256), cmap(PA_RQ // 256)),
                  pl.BlockSpec((RET_CHUNK, 256), cmap(PA_RK // 256)),
                  pl.BlockSpec((RET_CHUNK, 512), cmap(PB_RV // 512)),
                  pl.BlockSpec((RET_CHUNK, 512), cmap(PB_RG // 512))],
        out_specs=pl.BlockSpec((RET_CHUNK, BRANCH_WIDTH), cmap(0)),
        scratch_shapes=[pltpu.VMEM((RET_HEADS, LANES, RET_VAL_DIM), F32)],
        compiler_params=_cparams(("parallel", "arbitrary")),
        name="retention",
    )(log_gamma, pa, pa, pb, pb)


POOL_HALO = 16


def _pool_kernel(cur_ref, prev_ref, w_ref, scale_ref, o_ref, *, tiles_per_seq):
    tm = cur_ref.shape[0]
    it = pl.program_id(0) % tiles_per_seq
    cur = cur_ref[...].astype(F32)
    prev = jnp.where(it > 0, prev_ref[...].astype(F32), 0.0)
    ext = jnp.concatenate([prev, cur], axis=0)
    t = it * tm + lax.broadcasted_iota(I32, (tm, 1), 0)
    for gi, win in enumerate(POOL_WINDOWS):
        cols = slice(gi * POOL_GROUP_DIM, (gi + 1) * POOL_GROUP_DIM)
        s = ext[:, cols]
        sh = 1
        while sh < win:
            s = s + pltpu.roll(s, sh, axis=0)
            sh *= 2
        cnt = jnp.minimum(t + 1, win).astype(F32)
        pooled = s[POOL_HALO:, :] / cnt
        y = _dot((pooled - cur[:, cols]).astype(BF16), w_ref[gi])
        o_ref[:, cols] = (y * scale_ref[:, cols]).astype(o_ref.dtype)


def multiscale_pool(pb, pool_w, pool_scale, batch, seq):
    m = batch * seq
    tm = min(seq, 512)
    tps = seq // tm
    ratio = tm // POOL_HALO
    return pl.pallas_call(
        functools.partial(_pool_kernel, tiles_per_seq=tps),
        out_shape=jax.ShapeDtypeStruct((m, BRANCH_WIDTH), BF16),
        grid=(m // tm,),
        in_specs=[pl.BlockSpec((tm, 512), lambda i: (i, PB_PU // 512)),
                  pl.BlockSpec((POOL_HALO, 512), lambda i: (jnp.maximum(i * ratio - 1, 0), PB_PU // 512)),
                  pl.BlockSpec((len(POOL_WINDOWS), POOL_GROUP_DIM, POOL_GROUP_DIM), lambda i: (0, 0, 0)),
                  pl.BlockSpec((1, 512), lambda i: (0, 0))],
        out_specs=pl.BlockSpec((tm, BRANCH_WIDTH), lambda i: (i, 0)),
        compiler_params=_cparams(("parallel",)),
        name="multiscale_pool",
    )(pb, pb, pool_w, pool_scale)


def _merge_kernel(x_ref, oa_ref, ob_ref, oc_ref, od_ref, gates_ref, wb_ref, wo_ref, o_ref):
    d = x_ref.shape[1]
    merged = jnp.zeros(x_ref.shape, F32)
    for bi, br in enumerate((oa_ref, ob_ref, oc_ref, od_ref)):
        gate = gates_ref[:, bi * d:(bi + 1) * d].astype(F32)
        merged = merged + (1.0 / (1.0 + jnp.exp(-gate))) * _dot(br[...], wb_ref[bi])
    o_ref[...] = x_ref[...] + _dot(merged.astype(BF16), wo_ref[...])


def merge_branches(x, branches, pb, w_branch, w_out):
    m, d = x.shape
    tm = min(m, 512)
    row = lambda i: (i, 0)
    return pl.pallas_call(
        _merge_kernel,
        out_shape=jax.ShapeDtypeStruct((m, d), F32),
        grid=(m // tm,),
        in_specs=[pl.BlockSpec((tm, d), row)]
        + [pl.BlockSpec((tm, BRANCH_WIDTH), row)] * N_BRANCH
        + [pl.BlockSpec((tm, N_BRANCH * d), lambda i: (i, PB_GATES // (N_BRANCH * d))),
           pl.BlockSpec((N_BRANCH, BRANCH_WIDTH, d), lambda i: (0, 0, 0)),
           pl.BlockSpec((d, d), lambda i: (0, 0))],
        out_specs=pl.BlockSpec((tm, d), row),
        compiler_params=_cparams(("parallel",)),
        name="merge_branches",
    )(x, *branches, pb, w_branch, w_out)


def _cross_kernel(x_ref, g_ref, wq_ref, k_ref, v_ref, wo_ref, o_ref):
    x = x_ref[...]
    d = x.shape[1]
    hd = d // X_HEADS
    q = _dot(_rms(x, g_ref[...]).astype(BF16), wq_ref[...]).astype(BF16)
    outs = []
    for h in range(X_HEADS):
        cols = slice(h * hd, (h + 1) * hd)
        s = _dot_nt(q[:, cols], k_ref[:, cols])
        p = jnp.exp(s - jnp.max(s, axis=1, keepdims=True))
        o = _dot(p.astype(BF16), v_ref[:, cols]) / jnp.sum(p, axis=1, keepdims=True)
        outs.append(o.astype(BF16))
    o_ref[...] = x + _dot(jnp.concatenate(outs, axis=1), wo_ref[...])


def cross_attention(x, g, wq, kv, wo, batch, seq):
    m, d = x.shape
    mem_len = kv.shape[0] // batch
    tm = min(seq, 512)
    nt = seq // tm
    return pl.pallas_call(
        _cross_kernel,
        out_shape=jax.ShapeDtypeStruct((m, d), F32),
        grid=(batch, nt),
        in_specs=[pl.BlockSpec((tm, d), lambda b, i: (b * nt + i, 0)),
                  pl.BlockSpec((1, d), lambda b, i: (0, 0)),
                  pl.BlockSpec((d, d), lambda b, i: (0, 0)),
                  pl.BlockSpec((mem_len, d), lambda b, i: (b, 0)),
                  pl.BlockSpec((mem_len, d), lambda b, i: (b, 1)),
                  pl.BlockSpec((d, d), lambda b, i: (0, 0))],
        out_specs=pl.BlockSpec((tm, d), lambda b, i: (b * nt + i, 0)),
        compiler_params=_cparams(("parallel", "arbitrary")),
        name="cross_attention",
    )(x, g, wq, kv, kv, wo)


FFN_TF = 512


def _swiglu(h, wg, wu):
    a = _dot(h, wg)
    return a * (1.0 / (1.0 + jnp.exp(-a))) * _dot(h, wu)


def _ffn_kernel(x_ref, g_ref, wg_ref, wu_ref, wd_ref, o_ref, h_ref, acc_ref):
    f = pl.program_id(1)

    @pl.when(f == 0)
    def _():
        x = x_ref[...]
        h_ref[...] = _rms(x, g_ref[...]).astype(BF16)
        acc_ref[...] = x

    acc_ref[...] += _dot(_swiglu(h_ref[...], wg_ref[...], wu_ref[...]).astype(BF16), wd_ref[...])

    @pl.when(f == pl.num_programs(1) - 1)
    def _():
        o_ref[...] = acc_ref[...]


def ffn_dense(x, g, w_gu, w_down):
    m, d = x.shape
    dff = w_down.shape[0]
    tm = min(m, 1024)
    nf = dff // FFN_TF
    return pl.pallas_call(
        _ffn_kernel,
        out_shape=jax.ShapeDtypeStruct((m, d), F32),
        grid=(m // tm, nf),
        in_specs=[pl.BlockSpec((tm, d), lambda i, f: (i, 0)),
                  pl.BlockSpec((1, d), lambda i, f: (0, 0)),
                  pl.BlockSpec((d, FFN_TF), lambda i, f: (0, f)),
                  pl.BlockSpec((d, FFN_TF), lambda i, f: (0, nf + f)),
                  pl.BlockSpec((FFN_TF, d), lambda i, f: (f, 0))],
        out_specs=pl.BlockSpec((tm, d), lambda i, f: (i, 0)),
        scratch_shapes=[pltpu.VMEM((tm, d), BF16), pltpu.VMEM((tm, d), F32)],
        compiler_params=_cparams(("parallel", "arbitrary")),
        name="ffn_dense",
    )(x, g, w_gu, w_gu, w_down)


MOE_TR = 512
MOE_TG = 1024
MOE_ALIGN = 16
R_I1, R_I2, R_W1, R_W2, R_LR1, R_LR2 = range(6)


def _route_dispatch_kernel(x_ref, g_ref, rw_ref, xs_hbm, route_ref, starts_ref, seg_ref,
                           buf_ref, sem_ref, cnt_ref, *, n_experts, region):
    i = pl.program_id(0)
    tr = MOE_TR

    @pl.when(i == 0)
    def _():
        for e in range(n_experts):
            cnt_ref[e] = 0

    hn = _rms(x_ref[...], g_ref[...])
    logits = jnp.dot(hn, rw_ref[...], precision=lax.Precision.HIGHEST, preferred_element_type=F32)
    lane = lax.broadcasted_iota(I32, (1, LANES), 1).astype(F32)
    ninf = jnp.float32(-jnp.inf)
    lg = jnp.where(lane < n_experts, logits, ninf)
    m1 = jnp.max(lg, axis=1, keepdims=True)
    i1 = jnp.min(jnp.where(lg == m1, lane, float(LANES)), axis=1, keepdims=True)
    lg2 = jnp.where(lane == i1, ninf, lg)
    m2 = jnp.max(lg2, axis=1, keepdims=True)
    i2 = jnp.min(jnp.where(lg2 == m2, lane, float(LANES)), axis=1, keepdims=True)
    e2 = jnp.exp(m2 - m1)
    den = 1.0 + e2
    oh1 = lane == i1
    oh2 = lane == i2
    both = jnp.where(jnp.logical_or(oh1, oh2), 1.0, 0.0)
    tri = (lax.broadcasted_iota(I32, (tr, tr), 0) > lax.broadcasted_iota(I32, (tr, tr), 1))
    prefix = _dot(jnp.where(tri, 1.0, 0.0).astype(BF16), both.astype(BF16))
    lr1 = jnp.sum(jnp.where(oh1, prefix, 0.0), axis=1, keepdims=True)
    lr2 = jnp.sum(jnp.where(oh2, prefix, 0.0), axis=1, keepdims=True)
    record = jnp.zeros((tr, LANES), F32)
    for col, val in ((R_I1, i1), (R_I2, i2), (R_W1, 1.0 / den), (R_W2, e2 / den), (R_LR1, lr1), (R_LR2, lr2)):
        record = jnp.where(lane == col, val, record)
    route_ref[...] = record
    counts = jnp.sum(both, axis=0, keepdims=True)

    rec_t = record.T
    i1_t, i2_t = rec_t[R_I1:R_I1 + 1, :], rec_t[R_I2:R_I2 + 1, :]
    lr1_t, lr2_t = rec_t[R_LR1:R_LR1 + 1, :], rec_t[R_LR2:R_LR2 + 1, :]
    h = hn.astype(BF16)
    slot = lax.broadcasted_iota(I32, (tr, tr), 0).astype(F32)
    lane_i = lax.broadcasted_iota(I32, (1, LANES), 1)
    starts = jnp.zeros((1, LANES), I32)
    segs = jnp.zeros((1, LANES), I32)

    def block_copy(buf_slot, start):
        return pltpu.make_async_copy(buf_ref.at[buf_slot],
                                     xs_hbm.at[pl.ds(pl.multiple_of(start, MOE_ALIGN), tr), :],
                                     sem_ref.at[buf_slot])

    for e in range(n_experts):
        in_e1 = i1_t == float(e)
        lr = jnp.where(in_e1, lr1_t, jnp.where(i2_t == float(e), lr2_t, -1.0))
        perm = jnp.where(slot == lr, 1.0, 0.0).astype(BF16)
        blk = _dot(perm, h).astype(BF16)
        buf_slot = e % 2
        if e >= 2:
            block_copy(buf_slot, 0).wait()
        else:
            @pl.when(i > 0)
            def _(buf_slot=buf_slot):
                block_copy(buf_slot, 0).wait()
        buf_ref[buf_slot] = blk
        start = e * region + cnt_ref[e]
        block_copy(buf_slot, start).start()
        seg = ((counts[0, e].astype(I32) + (MOE_ALIGN - 1)) // MOE_ALIGN) * MOE_ALIGN
        starts = jnp.where(lane_i == e, start, starts)
        segs = jnp.where(lane_i == e, seg, segs)
        cnt_ref[e] = cnt_ref[e] + seg

    starts_ref[0] = starts
    seg_ref[0] = segs

    @pl.when(i == pl.num_programs(0) - 1)
    def _():
        block_copy(0, 0).wait()
        block_copy(1, 0).wait()
        buf_ref[0] = jnp.zeros(buf_ref.shape[1:], BF16)
        for e in range(n_experts):
            for k in range(MOE_TG // tr + 1):
                tail = block_copy(0, e * region + jnp.minimum(cnt_ref[e] + k * tr, region - tr))
                tail.start()
                tail.wait()


def moe_route_dispatch(x, g, router_w_padded, n_experts):
    m, d = x.shape
    tr = MOE_TR
    n_tiles = m // tr
    region = m + MOE_TG
    kern = functools.partial(_route_dispatch_kernel, n_experts=n_experts, region=region)
    return pl.pallas_call(
        kern,
        out_shape=(jax.ShapeDtypeStruct((n_experts * region, d), BF16),
                   jax.ShapeDtypeStruct((m, LANES), F32),
                   jax.ShapeDtypeStruct((n_tiles, 1, LANES), I32),
                   jax.ShapeDtypeStruct((n_tiles, 1, LANES), I32)),
        grid=(n_tiles,),
        in_specs=[pl.BlockSpec((tr, d), lambda i: (i, 0)),
                  pl.BlockSpec((1, d), lambda i: (0, 0)),
                  pl.BlockSpec((d, LANES), lambda i: (0, 0))],
        out_specs=(pl.BlockSpec(memory_space=pl.ANY),
                   pl.BlockSpec((tr, LANES), lambda i: (i, 0)),
                   pl.BlockSpec((1, 1, LANES), lambda i: (i, 0, 0)),
                   pl.BlockSpec((1, 1, LANES), lambda i: (i, 0, 0))),
        scratch_shapes=[pltpu.VMEM((2, tr, d), BF16), pltpu.SemaphoreType.DMA((2,)),
                        pltpu.SMEM((n_experts,), I32)],
        compiler_params=_cparams(("arbitrary",)),
        name="moe_route_dispatch",
    )(x, g, router_w_padded)


def _ffn_grouped_kernel(trow_ref, texp_ref, tval_ref, x_ref, wg_ref, wu_ref, wd_ref, o_ref, acc_ref):
    t = pl.program_id(0)
    f = pl.program_id(1)

    @pl.when(tval_ref[t] != 0)
    def _():
        part = _dot(_swiglu(x_ref[...], wg_ref[0], wu_ref[0]).astype(BF16), wd_ref[0])

        @pl.when(f == 0)
        def _():
            acc_ref[...] = part

        @pl.when(f > 0)
        def _():
            acc_ref[...] += part

        @pl.when(f == pl.num_programs(1) - 1)
        def _():
            o_ref[...] = acc_ref[...].astype(o_ref.dtype)


def moe_grouped_ffn(xs, w_gu, w_down, trow, texp, tval):
    rows, d = xs.shape
    dff = w_down.shape[1]
    nf = dff // FFN_TF
    n_steps = trow.shape[0]
    col = lambda f, tv, t: jnp.where(tv[t] != 0, f, nf - 1)
    grid_spec = pltpu.PrefetchScalarGridSpec(
        num_scalar_prefetch=3,
        grid=(n_steps, nf),
        in_specs=[pl.BlockSpec((MOE_TG, d), lambda t, f, tr_, te, tv: (tr_[t], 0)),
                  pl.BlockSpec((1, d, FFN_TF), lambda t, f, tr_, te, tv: (te[t], 0, col(f, tv, t))),
                  pl.BlockSpec((1, d, FFN_TF), lambda t, f, tr_, te, tv: (te[t], 0, nf + col(f, tv, t))),
                  pl.BlockSpec((1, FFN_TF, d), lambda t, f, tr_, te, tv: (te[t], col(f, tv, t), 0))],
        out_specs=pl.BlockSpec((MOE_TG, d), lambda t, f, tr_, te, tv: (tr_[t], 0)),
        scratch_shapes=[pltpu.VMEM((MOE_TG, d), F32)])
    return pl.pallas_call(
        _ffn_grouped_kernel,
        out_shape=jax.ShapeDtypeStruct((rows, d), BF16),
        grid_spec=grid_spec,
        compiler_params=_cparams(("arbitrary", "arbitrary")),
        name="moe_grouped_ffn",
    )(trow, texp, tval, xs, w_gu, w_gu, w_down)


def _moe_combine_kernel(starts_ref, x_ref, route_ref, ys_hbm, o_ref, buf_ref, sem_ref, *, n_experts):
    i = pl.program_id(0)
    tr = MOE_TR

    def block_copy(e):
        start = starts_ref[i * n_experts + e]
        return pltpu.make_async_copy(ys_hbm.at[pl.ds(pl.multiple_of(start, MOE_ALIGN), tr), :],
                                     buf_ref.at[e], sem_ref.at[e])

    for e in range(n_experts):
        block_copy(e).start()
    rec = route_ref[...]
    i1, i2 = rec[:, R_I1:R_I1 + 1], rec[:, R_I2:R_I2 + 1]
    w1, w2 = rec[:, R_W1:R_W1 + 1], rec[:, R_W2:R_W2 + 1]
    lr1, lr2 = rec[:, R_LR1:R_LR1 + 1], rec[:, R_LR2:R_LR2 + 1]
    slot = lax.broadcasted_iota(I32, (tr, tr), 1).astype(F32)
    out = x_ref[...]
    for e in range(n_experts):
        in_e1 = i1 == float(e)
        in_e2 = i2 == float(e)
        lr = jnp.where(in_e1, lr1, jnp.where(in_e2, lr2, -1.0))
        w = jnp.where(in_e1, w1, jnp.where(in_e2, w2, 0.0))
        pick = jnp.where(slot == lr, 1.0, 0.0).astype(BF16)
        block_copy(e).wait()
        out = out + w * _dot(pick, buf_ref[e])
    o_ref[...] = out


def moe_combine(x, route, ys, starts_flat, n_experts):
    m, d = x.shape
    tr = MOE_TR
    grid_spec = pltpu.PrefetchScalarGridSpec(
        num_scalar_prefetch=1,
        grid=(m // tr,),
        in_specs=[pl.BlockSpec((tr, d), lambda i, s: (i, 0)),
                  pl.BlockSpec((tr, LANES), lambda i, s: (i, 0)),
                  pl.BlockSpec(memory_space=pl.ANY)],
        out_specs=pl.BlockSpec((tr, d), lambda i, s: (i, 0)),
        scratch_shapes=[pltpu.VMEM((n_experts, tr, d), BF16), pltpu.SemaphoreType.DMA((n_experts,))])
    return pl.pallas_call(
        functools.partial(_moe_combine_kernel, n_experts=n_experts),
        out_shape=jax.ShapeDtypeStruct((m, d), F32),
        grid_spec=grid_spec,
        compiler_params=_cparams(("arbitrary",)),
        name="moe_combine",
    )(starts_flat, x, route, ys)


def _moe_tile_plan(seg, n_experts, region, n_steps):
    tiles_per_region = region // MOE_TG
    rows = jnp.sum(seg, axis=0)
    live = jnp.minimum((rows + MOE_TR + MOE_TG - 1) // MOE_TG, tiles_per_region)
    ends = jnp.cumsum(live)
    total = ends[-1]
    t = jnp.minimum(jnp.arange(n_steps, dtype=I32), total - 1)
    texp = jnp.sum((t[:, None] >= ends[None, :]).astype(I32), axis=1)
    first = ends - live
    trow = texp * tiles_per_region + (t - first[texp])
    tval = (jnp.arange(n_steps, dtype=I32) < total).astype(I32)
    return trow.astype(I32), texp.astype(I32), tval


def moe_ffn(x, g, router_w, w_gu, w_down):
    m, d = x.shape
    n_experts = router_w.shape[-1]
    region = m + MOE_TG
    n_tiles = m // MOE_TR
    rw = jnp.pad(router_w, ((0, 0), (0, LANES - n_experts)))
    xs, route, starts, seg = moe_route_dispatch(x, g, rw, n_experts)
    starts = starts[:, 0, :n_experts]
    seg = seg[:, 0, :n_experts]
    max_rows = TOP_K * m + n_tiles * n_experts * (MOE_ALIGN - 1)
    n_steps = max_rows // MOE_TG + 2 * n_experts
    trow, texp, tval = _moe_tile_plan(seg, n_experts, region, n_steps)
    ys = moe_grouped_ffn(xs, w_gu, w_down, trow, texp, tval)
    return moe_combine(x, route, ys, starts.reshape(-1), n_experts)


def _final_norm_kernel(x_ref, g_ref, o_ref):
    o_ref[...] = _rms(x_ref[...], g_ref[...])


def final_norm(x, g):
    m, d = x.shape
    tm = min(m, 1024)
    return pl.pallas_call(
        _final_norm_kernel,
        out_shape=jax.ShapeDtypeStruct((m, d), F32),
        grid=(m // tm,),
        in_specs=[pl.BlockSpec((tm, d), lambda i: (i, 0)), pl.BlockSpec((1, d), lambda i: (0, 0))],
        out_specs=pl.BlockSpec((tm, d), lambda i: (i, 0)),
        compiler_params=_cparams(("parallel",)),
        name="final_norm",
    )(x, g)


def _rotate_half_cols(w):
    d = w.shape[0]
    w4 = w.reshape(d, -1, 2, HEAD_DIM // 2)
    return jnp.stack([-w4[:, :, 1], w4[:, :, 0]], axis=2).reshape(d, -1)


def _dup_heads(w):
    d = w.shape[0]
    w3 = w.reshape(d, -1, 1, HEAD_DIM)
    return jnp.concatenate([w3, w3], axis=2).reshape(d, -1)


def _split_w_in(w_in):
    sizes = (SWA_HEADS * HEAD_DIM, SWA_KV_HEADS * HEAD_DIM, SWA_KV_HEADS * HEAD_DIM,
             DSA_HEADS * HEAD_DIM, DSA_KV_HEADS * HEAD_DIM, DSA_KV_HEADS * HEAD_DIM,
             IDX_HEADS * IDX_DIM, IDX_DIM, IDX_HEADS,
             RET_HEADS * RET_KEY_DIM, RET_HEADS * RET_KEY_DIM, RET_HEADS * RET_VAL_DIM, RET_HEADS * RET_VAL_DIM,
             len(POOL_WINDOWS) * POOL_GROUP_DIM, N_BRANCH * w_in.shape[0])
    parts, off = [], 0
    for s in sizes:
        parts.append(w_in[:, off:off + s])
        off += s
    return parts


def _layout_w_in(w_in):
    d = w_in.shape[0]
    (a_q, a_k, a_v, b_q, b_k, b_v, i_q, i_k, i_w, r_q, r_k, r_v, r_g, p_u, gates) = _split_w_in(w_in)
    qs = HEAD_DIM ** -0.5
    rope_cols = [a_q * qs, b_q * qs, i_q * (IDX_DIM ** -0.5), r_q, r_k * (RET_KEY_DIM ** -0.5),
                 _dup_heads(a_k), _dup_heads(b_k), _dup_heads(i_k)]
    wx = jnp.concatenate(rope_cols, axis=1)
    wp = jnp.concatenate([_rotate_half_cols(c) for c in rope_cols], axis=1)
    pad = PA_COLS - wx.shape[1]
    wx = jnp.pad(wx, ((0, 0), (0, pad)))
    wp = jnp.pad(wp, ((0, 0), (0, pad)))
    nt = PA_COLS // PA_TN
    w_rope = jnp.concatenate([wx.reshape(d, nt, 1, PA_TN), wp.reshape(d, nt, 1, PA_TN)], axis=2)
    w_rope = w_rope.reshape(d, 2 * PA_COLS).astype(BF16)
    iw_pad = jnp.pad(i_w * (IDX_HEADS ** -0.5), ((0, 0), (0, LANES - IDX_HEADS)))
    w_plain = jnp.concatenate([gates, r_v, r_g, p_u, _dup_heads(a_v), _dup_heads(b_v), iw_pad], axis=1)
    return w_rope, w_plain.astype(BF16)


def kernel(x, mem, positions, norm_mix, w_in, attn_sink, w_branch, w_out, pool_w, pool_scale, norm_cross, norm_mem, w_xq, w_xkv, w_xo, norm_ffn, dense_w_gu, dense_w_down, router_w, moe_w_gu, moe_w_down, norm_final):
    batch, seq, d = x.shape
    depth = norm_mix.shape[0]
    m = batch * seq
    mem_len = mem.shape[1]
    xs = x.reshape(m, d)
    mem2 = mem.reshape(batch * mem_len, d)

    inv = 1.0 / (ROPE_THETA ** (jnp.arange(0, HEAD_DIM, 2, dtype=F32) / HEAD_DIM))
    inv128 = jnp.tile(inv, LANES // (HEAD_DIM // 2))[None, :]
    cos, sin = rope_tables(positions.reshape(m, 1), inv128)
    log_gamma = jnp.log1p(-(2.0 ** (-5.0 - jnp.arange(RET_HEADS, dtype=F32))))
    x_scale = (d // X_HEADS) ** -0.5

    for layer in range(depth):
        w_rope, w_plain = _layout_w_in(w_in[layer])
        pa = norm_matmul_rope(xs, norm_mix[layer][None, :], w_rope, cos, sin, tm=512, tn=PA_TN)
        pb = norm_matmul(xs, norm_mix[layer][None, :], w_plain, tm=512, tn=PB_TN, name="in_proj_plain")
        o_a = swa_attention(pa, pb, attn_sink[layer], batch, seq)
        o_b = dsa_attention(pa, pb, batch, seq)
        o_c = retention(pa, pb, log_gamma, batch, seq)
        o_d = multiscale_pool(pb, pool_w[layer].astype(BF16), pool_scale[layer][None, :], batch, seq)
        xs = merge_branches(xs, (o_a, o_b, o_c, o_d), pb, w_branch[layer].astype(BF16), w_out[layer].astype(BF16))

        kv = norm_matmul(mem2, norm_mem[layer][None, :], w_xkv[layer].astype(BF16),
                         tm=batch * mem_len, tn=512, name="mem_kv_proj")
        xs = cross_attention(xs, norm_cross[layer][None, :], (w_xq[layer] * x_scale).astype(BF16), kv,
                             w_xo[layer].astype(BF16), batch, seq)

        li = layer // 2
        if layer % 2 == 0:
            xs = ffn_dense(xs, norm_ffn[layer][None, :], dense_w_gu[li].astype(BF16), dense_w_down[li].astype(BF16))
        else:
            xs = moe_ffn(xs, norm_ffn[layer][None, :], router_w[li],
                         moe_w_gu[li].astype(BF16), moe_w_down[li].astype(BF16))

    return final_norm(xs, norm_final[None, :]).reshape(batch, seq, d)
```

```python
import functools

import jax
import jax.numpy as jnp
from jax import lax
from jax.experimental import pallas as pl
from jax.experimental.pallas import tpu as pltpu

F32 = jnp.float32
BF16 = jnp.bfloat16
I32 = jnp.int32

EPS = 1e-6
HEAD_DIM = 64
ROPE_THETA = 10000.0
BLOCK = 128
SWA_HEADS, SWA_KV_HEADS = 8, 2
DSA_HEADS, DSA_KV_HEADS = 8, 2
IDX_HEADS, IDX_DIM = 4, 64
DSA_TOPK = 256
RET_HEADS, RET_KEY_DIM, RET_VAL_DIM, RET_CHUNK = 4, 64, 128, 128
POOL_WINDOWS = (2, 4, 8, 16)
POOL_GROUP_DIM = 128
N_BRANCH = 4
BRANCH_WIDTH = 512
X_HEADS = 4
TOP_K = 2

LANES = 128
VMEM_LIMIT = 56 * 1024 * 1024
NEG_BIG = -1e30
LOG2_E = 1.4426950408889634
INT_MIN = -2 ** 31

PA_AQ, PA_BQ, PA_IQ, PA_RQ, PA_RK, PA_AK, PA_BK, PA_IK = 0, 512, 1024, 1280, 1536, 1792, 2048, 2304
PA_COLS, PA_TN = 2560, 640
PB_GATES, PB_RV, PB_RG, PB_PU, PB_AV, PB_BV, PB_IW = 0, 4096, 4608, 5120, 5632, 5888, 6144
PB_COLS, PB_TN = 6272, 896


def _cparams(sem):
    return pltpu.CompilerParams(dimension_semantics=sem, vmem_limit_bytes=VMEM_LIMIT)


def _dot(a, b):
    return jnp.dot(a, b, preferred_element_type=F32)


def _dot_nt(a, b):
    return lax.dot_general(a, b, (((1,), (1,)), ((), ())), preferred_element_type=F32)


def _rms(x, g):
    return x * lax.rsqrt(jnp.mean(x * x, axis=-1, keepdims=True) + EPS) * g


def _half_mask(hh):
    lane = lax.broadcasted_iota(I32, (1, LANES), 1)
    return (lane >= HEAD_DIM) if hh else (lane < HEAD_DIM)


def _keep_half(block, hh):
    return jnp.where(_half_mask(hh), block, jnp.zeros_like(block))


def _rope_kernel(pos_ref, inv_ref, cos_ref, sin_ref):
    ang = pos_ref[...].astype(F32) * inv_ref[...]
    cos_ref[...] = jnp.cos(ang)
    sin_ref[...] = jnp.sin(ang)


def rope_tables(pos_col, inv128):
    m = pos_col.shape[0]
    tm = min(m, 2048)
    return pl.pallas_call(
        _rope_kernel,
        out_shape=(jax.ShapeDtypeStruct((m, LANES), F32), jax.ShapeDtypeStruct((m, LANES), F32)),
        grid=(m // tm,),
        in_specs=[pl.BlockSpec((tm, 1), lambda i: (i, 0)), pl.BlockSpec((1, LANES), lambda i: (0, 0))],
        out_specs=(pl.BlockSpec((tm, LANES), lambda i: (i, 0)), pl.BlockSpec((tm, LANES), lambda i: (i, 0))),
        compiler_params=_cparams(("parallel",)),
        name="rope_tables",
    )(pos_col, inv128)


def _norm_matmul_kernel(x_ref, g_ref, w_ref, o_ref, h_ref):
    @pl.when(pl.program_id(1) == 0)
    def _():
        h_ref[...] = _rms(x_ref[...], g_ref[...]).astype(BF16)

    o_ref[...] = _dot(h_ref[...], w_ref[...]).astype(o_ref.dtype)


def norm_matmul(x, g, w, tm, tn, out_dtype=BF16, name="norm_matmul"):
    m, k = x.shape
    n = w.shape[1]
    tm = min(tm, m)
    return pl.pallas_call(
        _norm_matmul_kernel,
        out_shape=jax.ShapeDtypeStruct((m, n), out_dtype),
        grid=(m // tm, n // tn),
        in_specs=[pl.BlockSpec((tm, k), lambda i, j: (i, 0)),
                  pl.BlockSpec((1, k), lambda i, j: (0, 0)),
                  pl.BlockSpec((k, tn), lambda i, j: (0, j))],
        out_specs=pl.BlockSpec((tm, tn), lambda i, j: (i, j)),
        scratch_shapes=[pltpu.VMEM((tm, k), BF16)],
        compiler_params=_cparams(("parallel", "arbitrary")),
        name=name,
    )(x, g, w)


def _norm_matmul_rope_kernel(x_ref, g_ref, w_ref, cos_ref, sin_ref, o_ref, h_ref):
    @pl.when(pl.program_id(1) == 0)
    def _():
        h_ref[...] = _rms(x_ref[...], g_ref[...]).astype(BF16)

    tn = o_ref.shape[1]
    acc = _dot(h_ref[...], w_ref[...])
    cos = cos_ref[...]
    sin = sin_ref[...]
    for c in range(tn // LANES):
        lo = c * LANES
        o_ref[:, lo:lo + LANES] = (acc[:, lo:lo + LANES] * cos
                                   + acc[:, tn + lo:tn + lo + LANES] * sin).astype(o_ref.dtype)


def norm_matmul_rope(x, g, w, cos, sin, tm, tn):
    m, k = x.shape
    nt = w.shape[1] // (2 * tn)
    tm = min(tm, m)
    return pl.pallas_call(
        _norm_matmul_rope_kernel,
        out_shape=jax.ShapeDtypeStruct((m, nt * tn), BF16),
        grid=(m // tm, nt),
        in_specs=[pl.BlockSpec((tm, k), lambda i, j: (i, 0)),
                  pl.BlockSpec((1, k), lambda i, j: (0, 0)),
                  pl.BlockSpec((k, 2 * tn), lambda i, j: (0, j)),
                  pl.BlockSpec((tm, LANES), lambda i, j: (i, 0)),
                  pl.BlockSpec((tm, LANES), lambda i, j: (i, 0))],
        out_specs=pl.BlockSpec((tm, tn), lambda i, j: (i, j)),
        scratch_shapes=[pltpu.VMEM((tm, k), BF16)],
        compiler_params=_cparams(("parallel", "arbitrary")),
        name="in_proj_rope",
    )(x, g, w, cos, sin)


def _swa_kernel(sink_ref, q_ref, kc_ref, kp_ref, vc_ref, vp_ref, o_ref):
    i = pl.program_id(1)
    per_group = SWA_HEADS // SWA_KV_HEADS
    q_t = (q_ref[...].astype(F32) * LOG2_E).T.astype(BF16)
    key = lax.broadcasted_iota(I32, (2 * BLOCK, 1), 0)
    rel = lax.broadcasted_iota(I32, (1, BLOCK), 1) + BLOCK - key
    band = jnp.logical_and(jnp.logical_and(rel >= 0, rel < BLOCK), jnp.logical_or(key >= BLOCK, i > 0))
    bias = jnp.where(band, 0.0, -jnp.inf).astype(F32)
    bias = jnp.concatenate([bias] * per_group, axis=1)
    outs = []
    for g in range(SWA_KV_HEADS):
        cols = slice(g * LANES, (g + 1) * LANES)
        k_win = jnp.concatenate([kp_ref[:, cols], kc_ref[:, cols]], axis=0)[:, 0:HEAD_DIM]
        v_t = jnp.concatenate([vp_ref[:, cols], vc_ref[:, cols]], axis=0).astype(F32).T[0:HEAD_DIM].astype(BF16)
        heads = range(g * per_group, (g + 1) * per_group)
        q_g = jnp.concatenate([q_t[h * HEAD_DIM:(h + 1) * HEAD_DIM, :] for h in heads], axis=1)
        sink = jnp.concatenate([jnp.full((1, BLOCK), sink_ref[h] * LOG2_E, F32) for h in heads], axis=1)
        s = _dot(k_win, q_g) + bias
        m = jnp.maximum(jnp.max(s, axis=0, keepdims=True), sink)
        p = jnp.exp2(s - m)
        den = jnp.sum(p, axis=0, keepdims=True) + jnp.exp2(sink - m)
        o_t = _dot(v_t, p.astype(BF16)) / den
        outs += [o_t[:, a * BLOCK:(a + 1) * BLOCK] for a in range(per_group)]
    o_ref[...] = jnp.concatenate(outs, axis=0).T.astype(o_ref.dtype)


def swa_attention(pa, pb, sink, batch, seq):
    nb = seq // BLOCK
    cur = lambda col: (lambda b, i: (b * nb + i, col))
    prev = lambda col: (lambda b, i: (b * nb + jnp.maximum(i - 1, 0), col))
    return pl.pallas_call(
        _swa_kernel,
        out_shape=jax.ShapeDtypeStruct((batch * seq, BRANCH_WIDTH), BF16),
        grid=(batch, nb),
        in_specs=[pl.BlockSpec(memory_space=pltpu.SMEM),
                  pl.BlockSpec((BLOCK, 512), cur(PA_AQ // 512)),
                  pl.BlockSpec((BLOCK, 256), cur(PA_AK // 256)),
                  pl.BlockSpec((BLOCK, 256), prev(PA_AK // 256)),
                  pl.BlockSpec((BLOCK, 256), cur(PB_AV // 256)),
                  pl.BlockSpec((BLOCK, 256), prev(PB_AV // 256))],
        out_specs=pl.BlockSpec((BLOCK, BRANCH_WIDTH), cur(0)),
        compiler_params=_cparams(("parallel", "arbitrary")),
        name="swa_attention",
    )(sink, pa, pa, pa, pb, pb)


DSA_TQ = 256


def _dsa_kernel(q_ref, iq_ref, iw_ref, k_ref, ik_ref, v_ref, o_ref,
                skey_ref, vt_ref, qt_ref, cut_ref, m_ref, l_ref, acc_ref, alpha_ref, bias_ref, s_ref, p_ref,
                *, n_sel, seq):
    tq = DSA_TQ
    nq = seq // tq
    i = pl.program_id(1)
    n_chunks = i + 1
    q_pos = i * tq + lax.broadcasted_iota(I32, (1, tq), 1)
    key_off = lax.broadcasted_iota(I32, (tq, 1), 0)

    @pl.when(i == 0)
    def _():
        for jj in range(nq):
            vt = v_ref[jj * tq:(jj + 1) * tq, :].astype(F32).T
            vt_ref[jj, 0:HEAD_DIM, :] = vt[0:HEAD_DIM].astype(BF16)
            vt_ref[jj, HEAD_DIM:2 * HEAD_DIM, :] = vt[2 * HEAD_DIM:3 * HEAD_DIM].astype(BF16)

    iq_t = iq_ref[...].astype(F32).T.astype(BF16)
    w_t = iw_ref[...].astype(F32).T

    def score_chunk(j, carry):
        ikc = ik_ref[pl.ds(pl.multiple_of(j * tq, tq), tq), 0:IDX_DIM]
        sc = jnp.zeros((tq, tq), F32)
        for h in range(IDX_HEADS):
            sc = sc + jnp.maximum(_dot(ikc, iq_t[h * IDX_DIM:(h + 1) * IDX_DIM, :]), 0.0) * w_t[h:h + 1, :]
        sc = jnp.where(sc == 0.0, 0.0, sc)
        bits = lax.bitcast_convert_type(sc, I32)
        key = jnp.where(bits < 0, bits ^ jnp.int32(0x7FFFFFFF), bits)
        causal = (j * tq + key_off) <= q_pos
        skey_ref[j] = jnp.where(causal, key, jnp.int32(INT_MIN))
        return carry

    lax.fori_loop(0, n_chunks, score_chunk, 0)

    view = (tq // 8, 8, tq)
    pos_in_chunk = lax.broadcasted_iota(I32, view, 0) * 8 + lax.broadcasted_iota(I32, view, 1)

    def all_sublanes(x, op):
        for sh in (4, 2, 1):
            x = op(x, pltpu.roll(x, sh, axis=0))
        return x

    def count_keys(pred):
        def body(j, acc):
            x = pred(skey_ref[j].reshape(view), j)
            part = view[0] // 4
            sums = [jnp.sum(x[a * part:(a + 1) * part], axis=0) for a in range(4)]
            return acc + ((sums[0] + sums[1]) + (sums[2] + sums[3]))
        return all_sublanes(lax.fori_loop(0, n_chunks, body, jnp.zeros((8, tq), F32)), jnp.add)

    one = jnp.float32(1.0)
    zero = jnp.float32(0.0)
    k_f = jnp.float32(n_sel)
    cut_bits = max((seq - 1).bit_length(), 1)

    def bit_step(b, carry):
        res_u, c_ge = carry
        cand_u = res_u | lax.shift_left(jnp.int32(1), 31 - b)
        cand_s = cand_u ^ jnp.int32(INT_MIN)
        cnt = count_keys(lambda kc, j: jnp.where(kc >= cand_s, one, zero))
        take = cnt >= k_f
        return jnp.where(take, cand_u, res_u), jnp.where(take, cnt, c_ge)

    res_u, c_ge = lax.fori_loop(0, 32, bit_step, (jnp.zeros((8, tq), I32), jnp.zeros((8, tq), F32)))
    thr = res_u ^ jnp.int32(INT_MIN)
    short = thr == jnp.int32(INT_MIN)
    cut_ref[...] = jnp.where(short, jnp.int32(-1), jnp.int32(seq))
    unresolved = jnp.where(jnp.logical_and(c_ge != k_f, jnp.logical_not(short)), one, zero)

    @pl.when(jnp.max(unresolved) > 0.0)
    def _():
        need = k_f - count_keys(lambda kc, j: jnp.where(kc > thr, one, zero))

        def cut_step(b, ans):
            cand = ans | lax.shift_left(jnp.int32(1), cut_bits - 1 - b)
            lim = cand - 1

            def pred(kc, j):
                return jnp.where(kc == thr, jnp.where((j * tq + pos_in_chunk) <= lim, one, zero), zero)

            return jnp.where(count_keys(pred) < need, cand, ans)

        ans = lax.fori_loop(0, cut_bits, cut_step, jnp.zeros((8, tq), I32))
        cut_ref[...] = jnp.where(short, jnp.int32(-1), ans)

    qt_ref[...] = (q_ref[...].astype(F32) * LOG2_E).T.astype(BF16)
    m_ref[...] = jnp.full(m_ref.shape, NEG_BIG, F32)
    l_ref[...] = jnp.zeros(l_ref.shape, F32)
    acc_ref[...] = jnp.zeros(acc_ref.shape, F32)
    cut = cut_ref[...]
    neg = jnp.float32(NEG_BIG)
    group = DSA_HEADS // DSA_KV_HEADS

    def attend(j, carry):
        kc = skey_ref[j].reshape(view)
        start = pl.multiple_of(j * tq, tq)
        tie = jnp.where(kc == thr, jnp.where((j * tq + pos_in_chunk) <= cut, zero, neg), neg)
        bias_ref[...] = jnp.where(kc > thr, zero, tie).reshape(tq, tq)
        for h in range(DSA_HEADS):
            g = h // group
            kblk = k_ref[pl.ds(start, tq), g * LANES:g * LANES + HEAD_DIM]
            s = _dot(kblk, qt_ref[h * HEAD_DIM:(h + 1) * HEAD_DIM, :]) + bias_ref[...]
            s_ref[h] = s
            m_old = m_ref[h]
            m_new = jnp.maximum(m_old, jnp.max(s, axis=0, keepdims=True))
            alpha_ref[h] = jnp.exp2(m_old - m_new)
            m_ref[h] = m_new
        for h in range(DSA_HEADS):
            p = jnp.exp2(s_ref[h] - m_ref[h])
            l_ref[h] = alpha_ref[h] * l_ref[h] + jnp.sum(p, axis=0, keepdims=True)
            p_ref[h] = p.astype(BF16)
        for h in range(DSA_HEADS):
            g = h // group
            acc_ref[h] = alpha_ref[h] * acc_ref[h] + _dot(vt_ref[j, g * HEAD_DIM:(g + 1) * HEAD_DIM, :], p_ref[h])
        return carry

    lax.fori_loop(0, n_chunks, attend, 0)

    for p in range(DSA_HEADS // 2):
        o_t = jnp.concatenate([acc_ref[2 * p] / l_ref[2 * p], acc_ref[2 * p + 1] / l_ref[2 * p + 1]], axis=0)
        o_ref[:, p * LANES:(p + 1) * LANES] = o_t.T.astype(o_ref.dtype)


def dsa_attention(pa, pb, batch, seq):
    tq = DSA_TQ
    nq = seq // tq
    n_sel = min(DSA_TOPK, seq // 4)
    qmap = lambda col: (lambda b, i: (b * nq + i, col))
    smap = lambda col: (lambda b, i: (b, col))
    return pl.pallas_call(
        functools.partial(_dsa_kernel, n_sel=n_sel, seq=seq),
        out_shape=jax.ShapeDtypeStruct((batch * seq, BRANCH_WIDTH), BF16),
        grid=(batch, nq),
        in_specs=[pl.BlockSpec((tq, 512), qmap(PA_BQ // 512)),
                  pl.BlockSpec((tq, 256), qmap(PA_IQ // 256)),
                  pl.BlockSpec((tq, LANES), qmap(PB_IW // LANES)),
                  pl.BlockSpec((seq, 256), smap(PA_BK // 256)),
                  pl.BlockSpec((seq, LANES), smap(PA_IK // LANES)),
                  pl.BlockSpec((seq, 256), smap(PB_BV // 256))],
        out_specs=pl.BlockSpec((tq, BRANCH_WIDTH), qmap(0)),
        scratch_shapes=[pltpu.VMEM((nq, tq, tq), I32),
                        pltpu.VMEM((nq, 2 * HEAD_DIM, tq), BF16),
                        pltpu.VMEM((DSA_HEADS * HEAD_DIM, tq), BF16),
                        pltpu.VMEM((8, tq), I32),
                        pltpu.VMEM((DSA_HEADS, 1, tq), F32),
                        pltpu.VMEM((DSA_HEADS, 1, tq), F32),
                        pltpu.VMEM((DSA_HEADS, HEAD_DIM, tq), F32),
                        pltpu.VMEM((DSA_HEADS, 1, tq), F32),
                        pltpu.VMEM((tq, tq), F32),
                        pltpu.VMEM((DSA_HEADS, tq, tq), F32),
                        pltpu.VMEM((DSA_HEADS, tq, tq), BF16)],
        compiler_params=_cparams(("parallel", "arbitrary")),
        name="dsa_attention",
    )(pa, pa, pb, pa, pa, pb)


def _retention_kernel(lg_ref, q_ref, k_ref, v_ref, g_ref, o_ref, state_ref):
    cdim = RET_CHUNK

    @pl.when(pl.program_id(1) == 0)
    def _():
        state_ref[...] = jnp.zeros(state_ref.shape, F32)

    n = lax.broadcasted_iota(I32, (cdim, 1), 0).astype(F32)
    mcol = lax.broadcasted_iota(I32, (1, cdim), 1).astype(F32)
    rel = n - mcol
    for h in range(RET_HEADS):
        lg = lg_ref[h]
        decay = jnp.where(rel >= 0, jnp.exp(lg * jnp.maximum(rel, 0.0)), 0.0)
        q_decay = jnp.exp(lg * (n + 1.0))
        k_decay = jnp.exp(lg * (cdim - 1.0 - n))
        chunk_decay = jnp.exp(jnp.full((1, LANES), lg * cdim, F32))
        blk = slice((h // 2) * LANES, (h // 2 + 1) * LANES)
        qm = _keep_half(q_ref[:, blk], h % 2)
        km = _keep_half(k_ref[:, blk], h % 2)
        v = v_ref[:, h * RET_VAL_DIM:(h + 1) * RET_VAL_DIM]
        state = state_ref[h]
        inner = _dot((_dot_nt(qm, km) * decay).astype(BF16), v)
        cross = _dot(qm, state.astype(BF16)) * q_decay
        kd_t = (km.astype(F32) * k_decay).T.astype(BF16)
        state_ref[h] = state * chunk_decay + _dot(kd_t, v)
        out = inner + cross
        out = out * lax.rsqrt(jnp.mean(out * out, axis=-1, keepdims=True) + EPS)
        gate = g_ref[:, h * RET_VAL_DIM:(h + 1) * RET_VAL_DIM].astype(F32)
        silu = gate * (1.0 / (1.0 + jnp.exp(-gate)))
        o_ref[:, h * RET_VAL_DIM:(h + 1) * RET_VAL_DIM] = (silu * out).astype(o_ref.dtype)


def retention(pa, pb, log_gamma, batch, seq):
    nc = seq // RET_CHUNK
    cmap = lambda col: (lambda b, c: (b * nc + c, col))
    return pl.pallas_call(
        _retention_kernel,
        out_shape=jax.ShapeDtypeStruct((batch * seq, BRANCH_WIDTH), BF16),
        grid=(batch, nc),
        in_specs=[pl.BlockSpec(memory_space=pltpu.SMEM),
                  pl.BlockSpec((RET_CHUNK, 256), cmap(PA_RQ // 256)),
                  pl.BlockSpec((RET_CHUNK, 256), cmap(PA_RK // 256)),
                  pl.BlockSpec((RET_CHUNK, 512), cmap(PB_RV // 512)),
                  pl.BlockSpec((RET_CHUNK, 512), cmap(PB_RG // 512))],
        out_specs=pl.BlockSpec((RET_CHUNK, BRANCH_WIDTH), cmap(0)),
        scratch_shapes=[pltpu.VMEM((RET_HEADS, LANES, RET_VAL_DIM), F32)],
        compiler_params=_cparams(("parallel", "arbitrary")),
        name="retention",
    )(log_gamma, pa, pa, pb, pb)


POOL_HALO = 16


def _pool_kernel(cur_ref, prev_ref, w_ref, scale_ref, o_ref, *, tiles_per_seq):
    tm = cur_ref.shape[0]
    it = pl.program_id(0) % tiles_per_seq
    cur = cur_ref[...].astype(F32)
    prev = jnp.where(it > 0, prev_ref[...].astype(F32), 0.0)
    ext = jnp.concatenate([prev, cur], axis=0)
    t = it * tm + lax.broadcasted_iota(I32, (tm, 1), 0)
    for gi, win in enumerate(POOL_WINDOWS):
        cols = slice(gi * POOL_GROUP_DIM, (gi + 1) * POOL_GROUP_DIM)
        s = ext[:, cols]
        sh = 1
        while sh < win:
            s = s + pltpu.roll(s, sh, axis=0)
            sh *= 2
        cnt = jnp.minimum(t + 1, win).astype(F32)
        pooled = s[POOL_HALO:, :] / cnt
        y = _dot((pooled - cur[:, cols]).astype(BF16), w_ref[gi])
        o_ref[:, cols] = (y * scale_ref[:, cols]).astype(o_ref.dtype)


def multiscale_pool(pb, pool_w, pool_scale, batch, seq):
    m = batch * seq
    tm = min(seq, 512)
    tps = seq // tm
    ratio = tm // POOL_HALO
    return pl.pallas_call(
        functools.partial(_pool_kernel, tiles_per_seq=tps),
        out_shape=jax.ShapeDtypeStruct((m, BRANCH_WIDTH), BF16),
        grid=(m // tm,),
        in_specs=[pl.BlockSpec((tm, 512), lambda i: (i, PB_PU // 512)),
                  pl.BlockSpec((POOL_HALO, 512), lambda i: (jnp.maximum(i * ratio - 1, 0), PB_PU // 512)),
                  pl.BlockSpec((len(POOL_WINDOWS), POOL_GROUP_DIM, POOL_GROUP_DIM), lambda i: (0, 0, 0)),
                  pl.BlockSpec((1, 512), lambda i: (0, 0))],
        out_specs=pl.BlockSpec((tm, BRANCH_WIDTH), lambda i: (i, 0)),
        compiler_params=_cparams(("parallel",)),
        name="multiscale_pool",
    )(pb, pb, pool_w, pool_scale)


def _merge_kernel(x_ref, oa_ref, ob_ref, oc_ref, od_ref, gates_ref, wb_ref, wo_ref, o_ref):
    d = x_ref.shape[1]
    merged = jnp.zeros(x_ref.shape, F32)
    for bi, br in enumerate((oa_ref, ob_ref, oc_ref, od_ref)):
        gate = gates_ref[:, bi * d:(bi + 1) * d].astype(F32)
        merged = merged + (1.0 / (1.0 + jnp.exp(-gate))) * _dot(br[...], wb_ref[bi])
    o_ref[...] = x_ref[...] + _dot(merged.astype(BF16), wo_ref[...])


def merge_branches(x, branches, pb, w_branch, w_out):
    m, d = x.shape
    tm = min(m, 512)
    row = lambda i: (i, 0)
    return pl.pallas_call(
        _merge_kernel,
        out_shape=jax.ShapeDtypeStruct((m, d), F32),
        grid=(m // tm,),
        in_specs=[pl.BlockSpec((tm, d), row)]
        + [pl.BlockSpec((tm, BRANCH_WIDTH), row)] * N_BRANCH
        + [pl.BlockSpec((tm, N_BRANCH * d), lambda i: (i, PB_GATES // (N_BRANCH * d))),
           pl.BlockSpec((N_BRANCH, BRANCH_WIDTH, d), lambda i: (0, 0, 0)),
           pl.BlockSpec((d, d), lambda i: (0, 0))],
        out_specs=pl.BlockSpec((tm, d), row),
        compiler_params=_cparams(("parallel",)),
        name="merge_branches",
    )(x, *branches, pb, w_branch, w_out)


def _cross_kernel(x_ref, g_ref, wq_ref, k_ref, v_ref, wo_ref, o_ref):
    x = x_ref[...]
    d = x.shape[1]
    hd = d // X_HEADS
    q = _dot(_rms(x, g_ref[...]).astype(BF16), wq_ref[...]).astype(BF16)
    outs = []
    for h in range(X_HEADS):
        cols = slice(h * hd, (h + 1) * hd)
        s = _dot_nt(q[:, cols], k_ref[:, cols])
        p = jnp.exp(s - jnp.max(s, axis=1, keepdims=True))
        o = _dot(p.astype(BF16), v_ref[:, cols]) / jnp.sum(p, axis=1, keepdims=True)
        outs.append(o.astype(BF16))
    o_ref[...] = x + _dot(jnp.concatenate(outs, axis=1), wo_ref[...])


def cross_attention(x, g, wq, kv, wo, batch, seq):
    m, d = x.shape
    mem_len = kv.shape[0] // batch
    tm = min(seq, 512)
    nt = seq // tm
    return pl.pallas_call(
        _cross_kernel,
        out_shape=jax.ShapeDtypeStruct((m, d), F32),
        grid=(batch, nt),
        in_specs=[pl.BlockSpec((tm, d), lambda b, i: (b * nt + i, 0)),
                  pl.BlockSpec((1, d), lambda b, i: (0, 0)),
                  pl.BlockSpec((d, d), lambda b, i: (0, 0)),
                  pl.BlockSpec((mem_len, d), lambda b, i: (b, 0)),
                  pl.BlockSpec((mem_len, d), lambda b, i: (b, 1)),
                  pl.BlockSpec((d, d), lambda b, i: (0, 0))],
        out_specs=pl.BlockSpec((tm, d), lambda b, i: (b * nt + i, 0)),
        compiler_params=_cparams(("parallel", "arbitrary")),
        name="cross_attention",
    )(x, g, wq, kv, kv, wo)


FFN_TF = 512


def _swiglu(h, wg, wu):
    a = _dot(h, wg)
    return a * (1.0 / (1.0 + jnp.exp(-a))) * _dot(h, wu)


def _ffn_kernel(x_ref, g_ref, wg_ref, wu_ref, wd_ref, o_ref, h_ref, acc_ref):
    f = pl.program_id(1)

    @pl.when(f == 0)
    def _():
        x = x_ref[...]
        h_ref[...] = _rms(x, g_ref[...]).astype(BF16)
        acc_ref[...] = x

    acc_ref[...] += _dot(_swiglu(h_ref[...], wg_ref[...], wu_ref[...]).astype(BF16), wd_ref[...])

    @pl.when(f == pl.num_programs(1) - 1)
    def _():
        o_ref[...] = acc_ref[...]


def ffn_dense(x, g, w_gu, w_down):
    m, d = x.shape
    dff = w_down.shape[0]
    tm = min(m, 1024)
    nf = dff // FFN_TF
    return pl.pallas_call(
        _ffn_kernel,
        out_shape=jax.ShapeDtypeStruct((m, d), F32),
        grid=(m // tm, nf),
        in_specs=[pl.BlockSpec((tm, d), lambda i, f: (i, 0)),
                  pl.BlockSpec((1, d), lambda i, f: (0, 0)),
                  pl.BlockSpec((d, FFN_TF), lambda i, f: (0, f)),
                  pl.BlockSpec((d, FFN_TF), lambda i, f: (0, nf + f)),
                  pl.BlockSpec((FFN_TF, d), lambda i, f: (f, 0))],
        out_specs=pl.BlockSpec((tm, d), lambda i, f: (i, 0)),
        scratch_shapes=[pltpu.VMEM((tm, d), BF16), pltpu.VMEM((tm, d), F32)],
        compiler_params=_cparams(("parallel", "arbitrary")),
        name="ffn_dense",
    )(x, g, w_gu, w_gu, w_down)


MOE_TR = 512
MOE_TG = 1024
MOE_ALIGN = 16
R_I1, R_I2, R_W1, R_W2, R_LR1, R_LR2 = range(6)


def _route_dispatch_kernel(x_ref, g_ref, rw_ref, xs_hbm, route_ref, starts_ref, seg_ref,
                           buf_ref, sem_ref, cnt_ref, *, n_experts, region):
    i = pl.program_id(0)
    tr = MOE_TR

    @pl.when(i == 0)
    def _():
        for e in range(n_experts):
            cnt_ref[e] = 0

    hn = _rms(x_ref[...], g_ref[...])
    logits = jnp.dot(hn, rw_ref[...], precision=lax.Precision.HIGHEST, preferred_element_type=F32)
    lane = lax.broadcasted_iota(I32, (1, LANES), 1).astype(F32)
    ninf = jnp.float32(-jnp.inf)
    lg = jnp.where(lane < n_experts, logits, ninf)
    m1 = jnp.max(lg, axis=1, keepdims=True)
    i1 = jnp.min(jnp.where(lg == m1, lane, float(LANES)), axis=1, keepdims=True)
    lg2 = jnp.where(lane == i1, ninf, lg)
    m2 = jnp.max(lg2, axis=1, keepdims=True)
    i2 = jnp.min(jnp.where(lg2 == m2, lane, float(LANES)), axis=1, keepdims=True)
    e2 = jnp.exp(m2 - m1)
    den = 1.0 + e2
    oh1 = lane == i1
    oh2 = lane == i2
    both = jnp.where(jnp.logical_or(oh1, oh2), 1.0, 0.0)
    tri = (lax.broadcasted_iota(I32, (tr, tr), 0) > lax.broadcasted_iota(I32, (tr, tr), 1))
    prefix = _dot(jnp.where(tri, 1.0, 0.0).astype(BF16), both.astype(BF16))
    lr1 = jnp.sum(jnp.where(oh1, prefix, 0.0), axis=1, keepdims=True)
    lr2 = jnp.sum(jnp.where(oh2, prefix, 0.0), axis=1, keepdims=True)
    record = jnp.zeros((tr, LANES), F32)
    for col, val in ((R_I1, i1), (R_I2, i2), (R_W1, 1.0 / den), (R_W2, e2 / den), (R_LR1, lr1), (R_LR2, lr2)):
        record = jnp.where(lane == col, val, record)
    route_ref[...] = record
    counts = jnp.sum(both, axis=0, keepdims=True)

    rec_t = record.T
    i1_t, i2_t = rec_t[R_I1:R_I1 + 1, :], rec_t[R_I2:R_I2 + 1, :]
    lr1_t, lr2_t = rec_t[R_LR1:R_LR1 + 1, :], rec_t[R_LR2:R_LR2 + 1, :]
    h = hn.astype(BF16)
    slot = lax.broadcasted_iota(I32, (tr, tr), 0).astype(F32)
    lane_i = lax.broadcasted_iota(I32, (1, LANES), 1)
    starts = jnp.zeros((1, LANES), I32)
    segs = jnp.zeros((1, LANES), I32)

    def block_copy(buf_slot, start):
        return pltpu.make_async_copy(buf_ref.at[buf_slot],
                                     xs_hbm.at[pl.ds(pl.multiple_of(start, MOE_ALIGN), tr), :],
                                     sem_ref.at[buf_slot])

    for e in range(n_experts):
        in_e1 = i1_t == float(e)
        lr = jnp.where(in_e1, lr1_t, jnp.where(i2_t == float(e), lr2_t, -1.0))
        perm = jnp.where(slot == lr, 1.0, 0.0).astype(BF16)
        blk = _dot(perm, h).astype(BF16)
        buf_slot = e % 2
        if e >= 2:
            block_copy(buf_slot, 0).wait()
        else:
            @pl.when(i > 0)
            def _(buf_slot=buf_slot):
                block_copy(buf_slot, 0).wait()
        buf_ref[buf_slot] = blk
        start = e * region + cnt_ref[e]
        block_copy(buf_slot, start).start()
        seg = ((counts[0, e].astype(I32) + (MOE_ALIGN - 1)) // MOE_ALIGN) * MOE_ALIGN
        starts = jnp.where(lane_i == e, start, starts)
        segs = jnp.where(lane_i == e, seg, segs)
        cnt_ref[e] = cnt_ref[e] + seg

    starts_ref[0] = starts
    seg_ref[0] = segs

    @pl.when(i == pl.num_programs(0) - 1)
    def _():
        block_copy(0, 0).wait()
        block_copy(1, 0).wait()
        buf_ref[0] = jnp.zeros(buf_ref.shape[1:], BF16)
        for e in range(n_experts):
            for k in range(MOE_TG // tr + 1):
                tail = block_copy(0, e * region + jnp.minimum(cnt_ref[e] + k * tr, region - tr))
                tail.start()
                tail.wait()


def moe_route_dispatch(x, g, router_w_padded, n_experts):
    m, d = x.shape
    tr = MOE_TR
    n_tiles = m // tr
    region = m + MOE_TG
    kern = functools.partial(_route_dispatch_kernel, n_experts=n_experts, region=region)
    return pl.pallas_call(
        kern,
        out_shape=(jax.ShapeDtypeStruct((n_experts * region, d), BF16),
                   jax.ShapeDtypeStruct((m, LANES), F32),
                   jax.ShapeDtypeStruct((n_tiles, 1, LANES), I32),
                   jax.ShapeDtypeStruct((n_tiles, 1, LANES), I32)),
        grid=(n_tiles,),
        in_specs=[pl.BlockSpec((tr, d), lambda i: (i, 0)),
                  pl.BlockSpec((1, d), lambda i: (0, 0)),
                  pl.BlockSpec((d, LANES), lambda i: (0, 0))],
        out_specs=(pl.BlockSpec(memory_space=pl.ANY),
                   pl.BlockSpec((tr, LANES), lambda i: (i, 0)),
                   pl.BlockSpec((1, 1, LANES), lambda i: (i, 0, 0)),
                   pl.BlockSpec((1, 1, LANES), lambda i: (i, 0, 0))),
        scratch_shapes=[pltpu.VMEM((2, tr, d), BF16), pltpu.SemaphoreType.DMA((2,)),
                        pltpu.SMEM((n_experts,), I32)],
        compiler_params=_cparams(("arbitrary",)),
        name="moe_route_dispatch",
    )(x, g, router_w_padded)


def _ffn_grouped_kernel(trow_ref, texp_ref, tval_ref, x_ref, wg_ref, wu_ref, wd_ref, o_ref, acc_ref):
    t = pl.program_id(0)
    f = pl.program_id(1)

    @pl.when(tval_ref[t] != 0)
    def _():
        part = _dot(_swiglu(x_ref[...], wg_ref[0], wu_ref[0]).astype(BF16), wd_ref[0])

        @pl.when(f == 0)
        def _():
            acc_ref[...] = part

        @pl.when(f > 0)
        def _():
            acc_ref[...] += part

        @pl.when(f == pl.num_programs(1) - 1)
        def _():
            o_ref[...] = acc_ref[...].astype(o_ref.dtype)


def moe_grouped_ffn(xs, w_gu, w_down, trow, texp, tval):
    rows, d = xs.shape
    dff = w_down.shape[1]
    nf = dff // FFN_TF
    n_steps = trow.shape[0]
    col = lambda f, tv, t: jnp.where(tv[t] != 0, f, nf - 1)
    grid_spec = pltpu.PrefetchScalarGridSpec(
        num_scalar_prefetch=3,
        grid=(n_steps, nf),
        in_specs=[pl.BlockSpec((MOE_TG, d), lambda t, f, tr_, te, tv: (tr_[t], 0)),
                  pl.BlockSpec((1, d, FFN_TF), lambda t, f, tr_, te, tv: (te[t], 0, col(f, tv, t))),
                  pl.BlockSpec((1, d, FFN_TF), lambda t, f, tr_, te, tv: (te[t], 0, nf + col(f, tv, t))),
                  pl.BlockSpec((1, FFN_TF, d), lambda t, f, tr_, te, tv: (te[t], col(f, tv, t), 0))],
        out_specs=pl.BlockSpec((MOE_TG, d), lambda t, f, tr_, te, tv: (tr_[t], 0)),
        scratch_shapes=[pltpu.VMEM((MOE_TG, d), F32)])
    return pl.pallas_call(
        _ffn_grouped_kernel,
        out_shape=jax.ShapeDtypeStruct((rows, d), BF16),
        grid_spec=grid_spec,
        compiler_params=_cparams(("arbitrary", "arbitrary")),
        name="moe_grouped_ffn",
    )(trow, texp, tval, xs, w_gu, w_gu, w_down)


def _moe_combine_kernel(starts_ref, x_ref, route_ref, ys_hbm, o_ref, buf_ref, sem_ref, *, n_experts):
    i = pl.program_id(0)
    tr = MOE_TR

    def block_copy(e):
        start = starts_ref[i * n_experts + e]
        return pltpu.make_async_copy(ys_hbm.at[pl.ds(pl.multiple_of(start, MOE_ALIGN), tr), :],
                                     buf_ref.at[e], sem_ref.at[e])

    for e in range(n_experts):
        block_copy(e).start()
    rec = route_ref[...]
    i1, i2 = rec[:, R_I1:R_I1 + 1], rec[:, R_I2:R_I2 + 1]
    w1, w2 = rec[:, R_W1:R_W1 + 1], rec[:, R_W2:R_W2 + 1]
    lr1, lr2 = rec[:, R_LR1:R_LR1 + 1], rec[:, R_LR2:R_LR2 + 1]
    slot = lax.broadcasted_iota(I32, (tr, tr), 1).astype(F32)
    out = x_ref[...]
    for e in range(n_experts):
        in_e1 = i1 == float(e)
        in_e2 = i2 == float(e)
        lr = jnp.where(in_e1, lr1, jnp.where(in_e2, lr2, -1.0))
        w = jnp.where(in_e1, w1, jnp.where(in_e2, w2, 0.0))
        pick = jnp.where(slot == lr, 1.0, 0.0).astype(BF16)
        block_copy(e).wait()
        out = out + w * _dot(pick, buf_ref[e])
    o_ref[...] = out


def moe_combine(x, route, ys, starts_flat, n_experts):
    m, d = x.shape
    tr = MOE_TR
    grid_spec = pltpu.PrefetchScalarGridSpec(
        num_scalar_prefetch=1,
        grid=(m // tr,),
        in_specs=[pl.BlockSpec((tr, d), lambda i, s: (i, 0)),
                  pl.BlockSpec((tr, LANES), lambda i, s: (i, 0)),
                  pl.BlockSpec(memory_space=pl.ANY)],
        out_specs=pl.BlockSpec((tr, d), lambda i, s: (i, 0)),
        scratch_shapes=[pltpu.VMEM((n_experts, tr, d), BF16), pltpu.SemaphoreType.DMA((n_experts,))])
    return pl.pallas_call(
        functools.partial(_moe_combine_kernel, n_experts=n_experts),
        out_shape=jax.ShapeDtypeStruct((m, d), F32),
        grid_spec=grid_spec,
        compiler_params=_cparams(("arbitrary",)),
        name="moe_combine",
    )(starts_flat, x, route, ys)


def _moe_tile_plan(seg, n_experts, region, n_steps):
    tiles_per_region = region // MOE_TG
    rows = jnp.sum(seg, axis=0)
    live = jnp.minimum((rows + MOE_TR + MOE_TG - 1) // MOE_TG, tiles_per_region)
    ends = jnp.cumsum(live)
    total = ends[-1]
    t = jnp.minimum(jnp.arange(n_steps, dtype=I32), total - 1)
    texp = jnp.sum((t[:, None] >= ends[None, :]).astype(I32), axis=1)
    first = ends - live
    trow = texp * tiles_per_region + (t - first[texp])
    tval = (jnp.arange(n_steps, dtype=I32) < total).astype(I32)
    return trow.astype(I32), texp.astype(I32), tval


def moe_ffn(x, g, router_w, w_gu, w_down):
    m, d = x.shape
    n_experts = router_w.shape[-1]
    region = m + MOE_TG
    n_tiles = m // MOE_TR
    rw = jnp.pad(router_w, ((0, 0), (0, LANES - n_experts)))
    xs, route, starts, seg = moe_route_dispatch(x, g, rw, n_experts)
    starts = starts[:, 0, :n_experts]
    seg = seg[:, 0, :n_experts]
    max_rows = TOP_K * m + n_tiles * n_experts * (MOE_ALIGN - 1)
    n_steps = max_rows // MOE_TG + 2 * n_experts
    trow, texp, tval = _moe_tile_plan(seg, n_experts, region, n_steps)
    ys = moe_grouped_ffn(xs, w_gu, w_down, trow, texp, tval)
    return moe_combine(x, route, ys, starts.reshape(-1), n_experts)


def _final_norm_kernel(x_ref, g_ref, o_ref):
    o_ref[...] = _rms(x_ref[...], g_ref[...])


def final_norm(x, g):
    m, d = x.shape
    tm = min(m, 1024)
    return pl.pallas_call(
        _final_norm_kernel,
        out_shape=jax.ShapeDtypeStruct((m, d), F32),
        grid=(m // tm,),
        in_specs=[pl.BlockSpec((tm, d), lambda i: (i, 0)), pl.BlockSpec((1, d), lambda i: (0, 0))],
        out_specs=pl.BlockSpec((tm, d), lambda i: (i, 0)),
        compiler_params=_cparams(("parallel",)),
        name="final_norm",
    )(x, g)


def _rotate_half_cols(w):
    d = w.shape[0]
    w4 = w.reshape(d, -1, 2, HEAD_DIM // 2)
    return jnp.stack([-w4[:, :, 1], w4[:, :, 0]], axis=2).reshape(d, -1)


def _dup_heads(w):
    d = w.shape[0]
    w3 = w.reshape(d, -1, 1, HEAD_DIM)
    return jnp.concatenate([w3, w3], axis=2).reshape(d, -1)


def _split_w_in(w_in):
    sizes = (SWA_HEADS * HEAD_DIM, SWA_KV_HEADS * HEAD_DIM, SWA_KV_HEADS * HEAD_DIM,
             DSA_HEADS * HEAD_DIM, DSA_KV_HEADS * HEAD_DIM, DSA_KV_HEADS * HEAD_DIM,
             IDX_HEADS * IDX_DIM, IDX_DIM, IDX_HEADS,
             RET_HEADS * RET_KEY_DIM, RET_HEADS * RET_KEY_DIM, RET_HEADS * RET_VAL_DIM, RET_HEADS * RET_VAL_DIM,
             len(POOL_WINDOWS) * POOL_GROUP_DIM, N_BRANCH * w_in.shape[0])
    parts, off = [], 0
    for s in sizes:
        parts.append(w_in[:, off:off + s])
        off += s
    return parts


def _layout_w_in(w_in):
    d = w_in.shape[0]
    (a_q, a_k, a_v, b_q, b_k, b_v, i_q, i_k, i_w, r_q, r_k, r_v, r_g, p_u, gates) = _split_w_in(w_in)
    qs = HEAD_DIM ** -0.5
    rope_cols = [a_q * qs, b_q * qs, i_q * (IDX_DIM ** -0.5), r_q, r_k * (RET_KEY_DIM ** -0.5),
                 _dup_heads(a_k), _dup_heads(b_k), _dup_heads(i_k)]
    wx = jnp.concatenate(rope_cols, axis=1)
    wp = jnp.concatenate([_rotate_half_cols(c) for c in rope_cols], axis=1)
    pad = PA_COLS - wx.shape[1]
    wx = jnp.pad(wx, ((0, 0), (0, pad)))
    wp = jnp.pad(wp, ((0, 0), (0, pad)))
    nt = PA_COLS // PA_TN
    w_rope = jnp.concatenate([wx.reshape(d, nt, 1, PA_TN), wp.reshape(d, nt, 1, PA_TN)], axis=2)
    w_rope = w_rope.reshape(d, 2 * PA_COLS).astype(BF16)
    iw_pad = jnp.pad(i_w * (IDX_HEADS ** -0.5), ((0, 0), (0, LANES - IDX_HEADS)))
    w_plain = jnp.concatenate([gates, r_v, r_g, p_u, _dup_heads(a_v), _dup_heads(b_v), iw_pad], axis=1)
    return w_rope, w_plain.astype(BF16)


def kernel(x, mem, positions, norm_mix, w_in, attn_sink, w_branch, w_out, pool_w, pool_scale, norm_cross, norm_mem, w_xq, w_xkv, w_xo, norm_ffn, dense_w_gu, dense_w_down, router_w, moe_w_gu, moe_w_down, norm_final):
    batch, seq, d = x.shape
    depth = norm_mix.shape[0]
    m = batch * seq
    mem_len = mem.shape[1]
    xs = x.reshape(m, d)
    mem2 = mem.reshape(batch * mem_len, d)

    inv = 1.0 / (ROPE_THETA ** (jnp.arange(0, HEAD_DIM, 2, dtype=F32) / HEAD_DIM))
    inv128 = jnp.tile(inv, LANES // (HEAD_DIM // 2))[None, :]
    cos, sin = rope_tables(positions.reshape(m, 1), inv128)
    log_gamma = jnp.log1p(-(2.0 ** (-5.0 - jnp.arange(RET_HEADS, dtype=F32))))
    x_scale = (d // X_HEADS) ** -0.5

    for layer in range(depth):
        w_rope, w_plain = _layout_w_in(w_in[layer])
        pa = norm_matmul_rope(xs, norm_mix[layer][None, :], w_rope, cos, sin, tm=1024, tn=PA_TN)
        pb = norm_matmul(xs, norm_mix[layer][None, :], w_plain, tm=1024, tn=PB_TN, name="in_proj_plain")
        o_a = swa_attention(pa, pb, attn_sink[layer], batch, seq)
        o_b = dsa_attention(pa, pb, batch, seq)
        o_c = retention(pa, pb, log_gamma, batch, seq)
        o_d = multiscale_pool(pb, pool_w[layer].astype(BF16), pool_scale[layer][None, :], batch, seq)
        xs = merge_branches(xs, (o_a, o_b, o_c, o_d), pb, w_branch[layer].astype(BF16), w_out[layer].astype(BF16))

        kv = norm_matmul(mem2, norm_mem[layer][None, :], w_xkv[layer].astype(BF16),
                         tm=batch * mem_len, tn=512, name="mem_kv_proj")
        xs = cross_attention(xs, norm_cross[layer][None, :], (w_xq[layer] * x_scale).astype(BF16), kv,
                             w_xo[layer].astype(BF16), batch, seq)

        li = layer // 2
        if layer % 2 == 0:
            xs = ffn_dense(xs, norm_ffn[layer][None, :], dense_w_gu[li].astype(BF16), dense_w_down[li].astype(BF16))
        else:
            xs = moe_ffn(xs, norm_ffn[layer][None, :], router_w[li],
                         moe_w_gu[li].astype(BF16), moe_w_down[li].astype(BF16))

    return final_norm(xs, norm_final[None, :]).reshape(batch, seq, d)
```

```python
import functools

import jax
import jax.numpy as jnp
from jax import lax
from jax.experimental import pallas as pl
from jax.experimental.pallas import tpu as pltpu

F32 = jnp.float32
BF16 = jnp.bfloat16
I32 = jnp.int32

EPS = 1e-6
HEAD_DIM = 64
ROPE_THETA = 10000.0
BLOCK = 128
SWA_HEADS, SWA_KV_HEADS = 8, 2
DSA_HEADS, DSA_KV_HEADS = 8, 2
IDX_HEADS, IDX_DIM = 4, 64
DSA_TOPK = 256
RET_HEADS, RET_KEY_DIM, RET_VAL_DIM = 4, 64, 128
RET_CHUNK = 256
POOL_WINDOWS = (2, 4, 8, 16)
POOL_GROUP_DIM = 128
N_BRANCH = 4
BRANCH_WIDTH = 512
X_HEADS = 4
TOP_K = 2

LANES = 128
VMEM_LIMIT = 56 * 1024 * 1024
NEG_BIG = -1e30
LOG2_E = 1.4426950408889634
INT_MIN = -2 ** 31

PA_AQ, PA_BQ, PA_IQ, PA_RQ, PA_RK, PA_AK, PA_BK, PA_IK = 0, 512, 1024, 1280, 1536, 1792, 2048, 2304
PA_COLS, PA_TN = 2560, 640
PB_GATES, PB_RV, PB_RG, PB_PU, PB_AV, PB_BV, PB_IW = 0, 4096, 4608, 5120, 5632, 5888, 6144
PB_COLS, PB_TN = 6272, 896


def _cparams(sem):
    return pltpu.CompilerParams(dimension_semantics=sem, vmem_limit_bytes=VMEM_LIMIT)


def _dot(a, b):
    return jnp.dot(a, b, preferred_element_type=F32)


def _dot_nt(a, b):
    return lax.dot_general(a, b, (((1,), (1,)), ((), ())), preferred_element_type=F32)


def _rms(x, g):
    return x * lax.rsqrt(jnp.mean(x * x, axis=-1, keepdims=True) + EPS) * g


def _half_mask(hh):
    lane = lax.broadcasted_iota(I32, (1, LANES), 1)
    return (lane >= HEAD_DIM) if hh else (lane < HEAD_DIM)


def _keep_half(block, hh):
    return jnp.where(_half_mask(hh), block, jnp.zeros_like(block))


def _rope_kernel(pos_ref, inv_ref, cos_ref, sin_ref):
    ang = pos_ref[...].astype(F32) * inv_ref[...]
    cos_ref[...] = jnp.cos(ang)
    sin_ref[...] = jnp.sin(ang)


def rope_tables(pos_col, inv128):
    m = pos_col.shape[0]
    tm = min(m, 2048)
    return pl.pallas_call(
        _rope_kernel,
        out_shape=(jax.ShapeDtypeStruct((m, LANES), F32), jax.ShapeDtypeStruct((m, LANES), F32)),
        grid=(m // tm,),
        in_specs=[pl.BlockSpec((tm, 1), lambda i: (i, 0)), pl.BlockSpec((1, LANES), lambda i: (0, 0))],
        out_specs=(pl.BlockSpec((tm, LANES), lambda i: (i, 0)), pl.BlockSpec((tm, LANES), lambda i: (i, 0))),
        compiler_params=_cparams(("parallel",)),
        name="rope_tables",
    )(pos_col, inv128)


def _norm_matmul_kernel(x_ref, g_ref, w_ref, o_ref, h_ref):
    @pl.when(pl.program_id(1) == 0)
    def _():
        h_ref[...] = _rms(x_ref[...], g_ref[...]).astype(BF16)

    o_ref[...] = _dot(h_ref[...], w_ref[...]).astype(o_ref.dtype)


def norm_matmul(x, g, w, tm, tn, out_dtype=BF16, name="norm_matmul"):
    m, k = x.shape
    n = w.shape[1]
    tm = min(tm, m)
    return pl.pallas_call(
        _norm_matmul_kernel,
        out_shape=jax.ShapeDtypeStruct((m, n), out_dtype),
        grid=(m // tm, n // tn),
        in_specs=[pl.BlockSpec((tm, k), lambda i, j: (i, 0)),
                  pl.BlockSpec((1, k), lambda i, j: (0, 0)),
                  pl.BlockSpec((k, tn), lambda i, j: (0, j))],
        out_specs=pl.BlockSpec((tm, tn), lambda i, j: (i, j)),
        scratch_shapes=[pltpu.VMEM((tm, k), BF16)],
        compiler_params=_cparams(("parallel", "arbitrary")),
        name=name,
    )(x, g, w)


def _norm_matmul_rope_kernel(x_ref, g_ref, w_ref, cos_ref, sin_ref, o_ref, h_ref):
    @pl.when(pl.program_id(1) == 0)
    def _():
        h_ref[...] = _rms(x_ref[...], g_ref[...]).astype(BF16)

    tn = o_ref.shape[1]
    acc = _dot(h_ref[...], w_ref[...])
    cos = cos_ref[...]
    sin = sin_ref[...]
    for c in range(tn // LANES):
        lo = c * LANES
        o_ref[:, lo:lo + LANES] = (acc[:, lo:lo + LANES] * cos
                                   + acc[:, tn + lo:tn + lo + LANES] * sin).astype(o_ref.dtype)


def norm_matmul_rope(x, g, w, cos, sin, tm, tn):
    m, k = x.shape
    nt = w.shape[1] // (2 * tn)
    tm = min(tm, m)
    return pl.pallas_call(
        _norm_matmul_rope_kernel,
        out_shape=jax.ShapeDtypeStruct((m, nt * tn), BF16),
        grid=(m // tm, nt),
        in_specs=[pl.BlockSpec((tm, k), lambda i, j: (i, 0)),
                  pl.BlockSpec((1, k), lambda i, j: (0, 0)),
                  pl.BlockSpec((k, 2 * tn), lambda i, j: (0, j)),
                  pl.BlockSpec((tm, LANES), lambda i, j: (i, 0)),
                  pl.BlockSpec((tm, LANES), lambda i, j: (i, 0))],
        out_specs=pl.BlockSpec((tm, tn), lambda i, j: (i, j)),
        scratch_shapes=[pltpu.VMEM((tm, k), BF16)],
        compiler_params=_cparams(("parallel", "arbitrary")),
        name="in_proj_rope",
    )(x, g, w, cos, sin)


def _swa_kernel(sink_ref, q_ref, kc_ref, kp_ref, vc_ref, vp_ref, o_ref):
    i = pl.program_id(1)
    per_group = SWA_HEADS // SWA_KV_HEADS
    q_t = (q_ref[...].astype(F32) * LOG2_E).T.astype(BF16)
    key = lax.broadcasted_iota(I32, (2 * BLOCK, 1), 0)
    rel = lax.broadcasted_iota(I32, (1, BLOCK), 1) + BLOCK - key
    band = jnp.logical_and(jnp.logical_and(rel >= 0, rel < BLOCK), jnp.logical_or(key >= BLOCK, i > 0))
    bias = jnp.where(band, 0.0, -jnp.inf).astype(F32)
    bias = jnp.concatenate([bias] * per_group, axis=1)
    outs = []
    for g in range(SWA_KV_HEADS):
        cols = slice(g * LANES, (g + 1) * LANES)
        k_win = jnp.concatenate([kp_ref[:, cols], kc_ref[:, cols]], axis=0)[:, 0:HEAD_DIM]
        v_t = jnp.concatenate([vp_ref[:, cols], vc_ref[:, cols]], axis=0).astype(F32).T[0:HEAD_DIM].astype(BF16)
        heads = range(g * per_group, (g + 1) * per_group)
        q_g = jnp.concatenate([q_t[h * HEAD_DIM:(h + 1) * HEAD_DIM, :] for h in heads], axis=1)
        sink = jnp.concatenate([jnp.full((1, BLOCK), sink_ref[h] * LOG2_E, F32) for h in heads], axis=1)
        s = _dot(k_win, q_g) + bias
        m = jnp.maximum(jnp.max(s, axis=0, keepdims=True), sink)
        p = jnp.exp2(s - m)
        den = jnp.sum(p, axis=0, keepdims=True) + jnp.exp2(sink - m)
        o_t = _dot(v_t, p.astype(BF16)) / den
        outs += [o_t[:, a * BLOCK:(a + 1) * BLOCK] for a in range(per_group)]
    o_ref[...] = jnp.concatenate(outs, axis=0).T.astype(o_ref.dtype)


def swa_attention(pa, pb, sink, batch, seq):
    nb = seq // BLOCK
    cur = lambda col: (lambda b, i: (b * nb + i, col))
    prev = lambda col: (lambda b, i: (b * nb + jnp.maximum(i - 1, 0), col))
    return pl.pallas_call(
        _swa_kernel,
        out_shape=jax.ShapeDtypeStruct((batch * seq, BRANCH_WIDTH), BF16),
        grid=(batch, nb),
        in_specs=[pl.BlockSpec(memory_space=pltpu.SMEM),
                  pl.BlockSpec((BLOCK, 512), cur(PA_AQ // 512)),
                  pl.BlockSpec((BLOCK, 256), cur(PA_AK // 256)),
                  pl.BlockSpec((BLOCK, 256), prev(PA_AK // 256)),
                  pl.BlockSpec((BLOCK, 256), cur(PB_AV // 256)),
                  pl.BlockSpec((BLOCK, 256), prev(PB_AV // 256))],
        out_specs=pl.BlockSpec((BLOCK, BRANCH_WIDTH), cur(0)),
        compiler_params=_cparams(("parallel", "arbitrary")),
        name="swa_attention",
    )(sink, pa, pa, pa, pb, pb)


DSA_TQ = 256


def _dsa_kernel(q_ref, iq_ref, iw_ref, k_ref, ik_ref, v_ref, o_ref,
                skey_ref, vt_ref, qt_ref, cut_ref, m_ref, l_ref, acc_ref, alpha_ref, bias_ref, s_ref, p_ref,
                *, n_sel, seq):
    tq = DSA_TQ
    nq = seq // tq
    i = pl.program_id(1)
    n_chunks = i + 1
    q_pos = i * tq + lax.broadcasted_iota(I32, (1, tq), 1)
    key_off = lax.broadcasted_iota(I32, (tq, 1), 0)

    @pl.when(i == 0)
    def _():
        for jj in range(nq):
            vt = v_ref[jj * tq:(jj + 1) * tq, :].astype(F32).T
            vt_ref[jj, 0:HEAD_DIM, :] = vt[0:HEAD_DIM].astype(BF16)
            vt_ref[jj, HEAD_DIM:2 * HEAD_DIM, :] = vt[2 * HEAD_DIM:3 * HEAD_DIM].astype(BF16)

    iq_t = iq_ref[...].astype(F32).T.astype(BF16)
    w_t = iw_ref[...].astype(F32).T

    def score_chunk(j, carry):
        ikc = ik_ref[pl.ds(pl.multiple_of(j * tq, tq), tq), 0:IDX_DIM]
        sc = jnp.zeros((tq, tq), F32)
        for h in range(IDX_HEADS):
            sc = sc + jnp.maximum(_dot(ikc, iq_t[h * IDX_DIM:(h + 1) * IDX_DIM, :]), 0.0) * w_t[h:h + 1, :]
        sc = jnp.where(sc == 0.0, 0.0, sc)
        bits = lax.bitcast_convert_type(sc, I32)
        key = jnp.where(bits < 0, bits ^ jnp.int32(0x7FFFFFFF), bits)
        causal = (j * tq + key_off) <= q_pos
        skey_ref[j] = jnp.where(causal, key, jnp.int32(INT_MIN))
        return carry

    lax.fori_loop(0, n_chunks, score_chunk, 0)

    view = (tq // 8, 8, tq)
    pos_in_chunk = lax.broadcasted_iota(I32, view, 0) * 8 + lax.broadcasted_iota(I32, view, 1)

    def all_sublanes(x, op):
        for sh in (4, 2, 1):
            x = op(x, pltpu.roll(x, sh, axis=0))
        return x

    def count_keys(pred):
        def body(j, acc):
            x = pred(skey_ref[j].reshape(view), j)
            part = view[0] // 4
            sums = [jnp.sum(x[a * part:(a + 1) * part], axis=0) for a in range(4)]
            return acc + ((sums[0] + sums[1]) + (sums[2] + sums[3]))
        return all_sublanes(lax.fori_loop(0, n_chunks, body, jnp.zeros((8, tq), F32)), jnp.add)

    one = jnp.float32(1.0)
    zero = jnp.float32(0.0)
    k_f = jnp.float32(n_sel)
    cut_bits = max((seq - 1).bit_length(), 1)

    def bit_step(b, carry):
        res_u, c_ge = carry
        cand_u = res_u | lax.shift_left(jnp.int32(1), 31 - b)
        cand_s = cand_u ^ jnp.int32(INT_MIN)
        cnt = count_keys(lambda kc, j: jnp.where(kc >= cand_s, one, zero))
        take = cnt >= k_f
        return jnp.where(take, cand_u, res_u), jnp.where(take, cnt, c_ge)

    res_u, c_ge = lax.fori_loop(0, 32, bit_step, (jnp.zeros((8, tq), I32), jnp.zeros((8, tq), F32)))
    thr = res_u ^ jnp.int32(INT_MIN)
    short = thr == jnp.int32(INT_MIN)
    cut_ref[...] = jnp.where(short, jnp.int32(-1), jnp.int32(seq))
    unresolved = jnp.where(jnp.logical_and(c_ge != k_f, jnp.logical_not(short)), one, zero)

    @pl.when(jnp.max(unresolved) > 0.0)
    def _():
        need = k_f - count_keys(lambda kc, j: jnp.where(kc > thr, one, zero))

        def cut_step(b, ans):
            cand = ans | lax.shift_left(jnp.int32(1), cut_bits - 1 - b)
            lim = cand - 1

            def pred(kc, j):
                return jnp.where(kc == thr, jnp.where((j * tq + pos_in_chunk) <= lim, one, zero), zero)

            return jnp.where(count_keys(pred) < need, cand, ans)

        ans = lax.fori_loop(0, cut_bits, cut_step, jnp.zeros((8, tq), I32))
        cut_ref[...] = jnp.where(short, jnp.int32(-1), ans)

    qt_ref[...] = (q_ref[...].astype(F32) * LOG2_E).T.astype(BF16)
    m_ref[...] = jnp.full(m_ref.shape, NEG_BIG, F32)
    l_ref[...] = jnp.zeros(l_ref.shape, F32)
    acc_ref[...] = jnp.zeros(acc_ref.shape, F32)
    cut = cut_ref[...]
    neg = jnp.float32(NEG_BIG)
    group = DSA_HEADS // DSA_KV_HEADS

    def attend(j, carry):
        kc = skey_ref[j].reshape(view)
        start = pl.multiple_of(j * tq, tq)
        tie = jnp.where(kc == thr, jnp.where((j * tq + pos_in_chunk) <= cut, zero, neg), neg)
        bias_ref[...] = jnp.where(kc > thr, zero, tie).reshape(tq, tq)
        for h in range(DSA_HEADS):
            g = h // group
            kblk = k_ref[pl.ds(start, tq), g * LANES:g * LANES + HEAD_DIM]
            s = _dot(kblk, qt_ref[h * HEAD_DIM:(h + 1) * HEAD_DIM, :]) + bias_ref[...]
            s_ref[h] = s
            m_old = m_ref[h]
            m_new = jnp.maximum(m_old, jnp.max(s, axis=0, keepdims=True))
            alpha_ref[h] = jnp.exp2(m_old - m_new)
            m_ref[h] = m_new
        for h in range(DSA_HEADS):
            p = jnp.exp2(s_ref[h] - m_ref[h])
            l_ref[h] = alpha_ref[h] * l_ref[h] + jnp.sum(p, axis=0, keepdims=True)
            p_ref[h] = p.astype(BF16)
        for h in range(DSA_HEADS):
            g = h // group
            acc_ref[h] = alpha_ref[h] * acc_ref[h] + _dot(vt_ref[j, g * HEAD_DIM:(g + 1) * HEAD_DIM, :], p_ref[h])
        return carry

    lax.fori_loop(0, n_chunks, attend, 0)

    for p in range(DSA_HEADS // 2):
        o_t = jnp.concatenate([acc_ref[2 * p] / l_ref[2 * p], acc_ref[2 * p + 1] / l_ref[2 * p + 1]], axis=0)
        o_ref[:, p * LANES:(p + 1) * LANES] = o_t.T.astype(o_ref.dtype)


def dsa_attention(pa, pb, batch, seq):
    tq = DSA_TQ
    nq = seq // tq
    n_sel = min(DSA_TOPK, seq // 4)
    qmap = lambda col: (lambda b, i: (b * nq + i, col))
    smap = lambda col: (lambda b, i: (b, col))
    return pl.pallas_call(
        functools.partial(_dsa_kernel, n_sel=n_sel, seq=seq),
        out_shape=jax.ShapeDtypeStruct((batch * seq, BRANCH_WIDTH), BF16),
        grid=(batch, nq),
        in_specs=[pl.BlockSpec((tq, 512), qmap(PA_BQ // 512)),
                  pl.BlockSpec((tq, 256), qmap(PA_IQ // 256)),
                  pl.BlockSpec((tq, LANES), qmap(PB_IW // LANES)),
                  pl.BlockSpec((seq, 256), smap(PA_BK // 256)),
                  pl.BlockSpec((seq, LANES), smap(PA_IK // LANES)),
                  pl.BlockSpec((seq, 256), smap(PB_BV // 256))],
        out_specs=pl.BlockSpec((tq, BRANCH_WIDTH), qmap(0)),
        scratch_shapes=[pltpu.VMEM((nq, tq, tq), I32),
                        pltpu.VMEM((nq, 2 * HEAD_DIM, tq), BF16),
                        pltpu.VMEM((DSA_HEADS * HEAD_DIM, tq), BF16),
                        pltpu.VMEM((8, tq), I32),
                        pltpu.VMEM((DSA_HEADS, 1, tq), F32),
                        pltpu.VMEM((DSA_HEADS, 1, tq), F32),
                        pltpu.VMEM((DSA_HEADS, HEAD_DIM, tq), F32),
                        pltpu.VMEM((DSA_HEADS, 1, tq), F32),
                        pltpu.VMEM((tq, tq), F32),
                        pltpu.VMEM((DSA_HEADS, tq, tq), F32),
                        pltpu.VMEM((DSA_HEADS, tq, tq), BF16)],
        compiler_params=_cparams(("parallel", "arbitrary")),
        name="dsa_attention",
    )(pa, pa, pb, pa, pa, pb)


def _retention_kernel(lg_ref, q_ref, k_ref, v_ref, g_ref, o_ref, state_ref):
    cdim = RET_CHUNK

    @pl.when(pl.program_id(1) == 0)
    def _():
        state_ref[...] = jnp.zeros(state_ref.shape, F32)

    n = lax.broadcasted_iota(I32, (cdim, 1), 0).astype(F32)
    mcol = lax.broadcasted_iota(I32, (1, cdim), 1).astype(F32)
    rel = n - mcol
    for h in range(RET_HEADS):
        lg = lg_ref[h]
        decay = jnp.where(rel >= 0, jnp.exp(lg * jnp.maximum(rel, 0.0)), 0.0)
        q_decay = jnp.exp(lg * (n + 1.0))
        k_decay = jnp.exp(lg * (cdim - 1.0 - n))
        chunk_decay = jnp.exp(jnp.full((1, LANES), lg * cdim, F32))
        blk = slice((h // 2) * LANES, (h // 2 + 1) * LANES)
        qm = _keep_half(q_ref[:, blk], h % 2)
        km = _keep_half(k_ref[:, blk], h % 2)
        v = v_ref[:, h * RET_VAL_DIM:(h + 1) * RET_VAL_DIM]
        state = state_ref[h]
        inner = _dot((_dot_nt(qm, km) * decay).astype(BF16), v)
        cross = _dot(qm, state.astype(BF16)) * q_decay
        kd_t = (km.astype(F32) * k_decay).T.astype(BF16)
        state_ref[h] = state * chunk_decay + _dot(kd_t, v)
        out = inner + cross
        out = out * lax.rsqrt(jnp.mean(out * out, axis=-1, keepdims=True) + EPS)
        gate = g_ref[:, h * RET_VAL_DIM:(h + 1) * RET_VAL_DIM].astype(F32)
        silu = gate * (1.0 / (1.0 + jnp.exp(-gate)))
        o_ref[:, h * RET_VAL_DIM:(h + 1) * RET_VAL_DIM] = (silu * out).astype(o_ref.dtype)


def retention(pa, pb, log_gamma, batch, seq):
    nc = seq // RET_CHUNK
    cmap = lambda col: (lambda b, c: (b * nc + c, col))
    return pl.pallas_call(
        _retention_kernel,
        out_shape=jax.ShapeDtypeStruct((batch * seq, BRANCH_WIDTH), BF16),
        grid=(batch, nc),
        in_specs=[pl.BlockSpec(memory_space=pltpu.SMEM),
                  pl.BlockSpec((RET_CHUNK, 256), cmap(PA_RQ // 256)),
                  pl.BlockSpec((RET_CHUNK, 256), cmap(PA_RK // 256)),
                  pl.BlockSpec((RET_CHUNK, 512), cmap(PB_RV // 512)),
                  pl.BlockSpec((RET_CHUNK, 512), cmap(PB_RG // 512))],
        out_specs=pl.BlockSpec((RET_CHUNK, BRANCH_WIDTH), cmap(0)),
        scratch_shapes=[pltpu.VMEM((RET_HEADS, LANES, RET_VAL_DIM), F32)],
        compiler_params=_cparams(("parallel", "arbitrary")),
        name="retention",
    )(log_gamma, pa, pa, pb, pb)


POOL_HALO = 16


def _pool_kernel(cur_ref, prev_ref, w_ref, scale_ref, o_ref, *, tiles_per_seq):
    tm = cur_ref.shape[0]
    it = pl.program_id(0) % tiles_per_seq
    cur = cur_ref[...].astype(F32)
    prev = jnp.where(it > 0, prev_ref[...].astype(F32), 0.0)
    ext = jnp.concatenate([prev, cur], axis=0)
    t = it * tm + lax.broadcasted_iota(I32, (tm, 1), 0)
    for gi, win in enumerate(POOL_WINDOWS):
        cols = slice(gi * POOL_GROUP_DIM, (gi + 1) * POOL_GROUP_DIM)
        s = ext[:, cols]
        sh = 1
        while sh < win:
            s = s + pltpu.roll(s, sh, axis=0)
            sh *= 2
        cnt = jnp.minimum(t + 1, win).astype(F32)
        pooled = s[POOL_HALO:, :] / cnt
        y = _dot((pooled - cur[:, cols]).astype(BF16), w_ref[gi])
        o_ref[:, cols] = (y * scale_ref[:, cols]).astype(o_ref.dtype)


def multiscale_pool(pb, pool_w, pool_scale, batch, seq):
    m = batch * seq
    tm = min(seq, 512)
    tps = seq // tm
    ratio = tm // POOL_HALO
    return pl.pallas_call(
        functools.partial(_pool_kernel, tiles_per_seq=tps),
        out_shape=jax.ShapeDtypeStruct((m, BRANCH_WIDTH), BF16),
        grid=(m // tm,),
        in_specs=[pl.BlockSpec((tm, 512), lambda i: (i, PB_PU // 512)),
                  pl.BlockSpec((POOL_HALO, 512), lambda i: (jnp.maximum(i * ratio - 1, 0), PB_PU // 512)),
                  pl.BlockSpec((len(POOL_WINDOWS), POOL_GROUP_DIM, POOL_GROUP_DIM), lambda i: (0, 0, 0)),
                  pl.BlockSpec((1, 512), lambda i: (0, 0))],
        out_specs=pl.BlockSpec((tm, BRANCH_WIDTH), lambda i: (i, 0)),
        compiler_params=_cparams(("parallel",)),
        name="multiscale_pool",
    )(pb, pb, pool_w, pool_scale)


def _merge_kernel(x_ref, oa_ref, ob_ref, oc_ref, od_ref, gates_ref, wb_ref, wo_ref, o_ref):
    d = x_ref.shape[1]
    merged = jnp.zeros(x_ref.shape, F32)
    for bi, br in enumerate((oa_ref, ob_ref, oc_ref, od_ref)):
        gate = gates_ref[:, bi * d:(bi + 1) * d].astype(F32)
        merged = merged + (1.0 / (1.0 + jnp.exp(-gate))) * _dot(br[...], wb_ref[bi])
    o_ref[...] = x_ref[...] + _dot(merged.astype(BF16), wo_ref[...])


def merge_branches(x, branches, pb, w_branch, w_out):
    m, d = x.shape
    tm = min(m, 512)
    row = lambda i: (i, 0)
    return pl.pallas_call(
        _merge_kernel,
        out_shape=jax.ShapeDtypeStruct((m, d), F32),
        grid=(m // tm,),
        in_specs=[pl.BlockSpec((tm, d), row)]
        + [pl.BlockSpec((tm, BRANCH_WIDTH), row)] * N_BRANCH
        + [pl.BlockSpec((tm, N_BRANCH * d), lambda i: (i, PB_GATES // (N_BRANCH * d))),
           pl.BlockSpec((N_BRANCH, BRANCH_WIDTH, d), lambda i: (0, 0, 0)),
           pl.BlockSpec((d, d), lambda i: (0, 0))],
        out_specs=pl.BlockSpec((tm, d), row),
        compiler_params=_cparams(("parallel",)),
        name="merge_branches",
    )(x, *branches, pb, w_branch, w_out)


def _cross_kernel(x_ref, g_ref, wq_ref, k_ref, v_ref, wo_ref, o_ref):
    x = x_ref[...]
    d = x.shape[1]
    hd = d // X_HEADS
    q = _dot(_rms(x, g_ref[...]).astype(BF16), wq_ref[...]).astype(BF16)
    outs = []
    for h in range(X_HEADS):
        cols = slice(h * hd, (h + 1) * hd)
        s = _dot_nt(q[:, cols], k_ref[:, cols])
        p = jnp.exp(s - jnp.max(s, axis=1, keepdims=True))
        o = _dot(p.astype(BF16), v_ref[:, cols]) / jnp.sum(p, axis=1, keepdims=True)
        outs.append(o.astype(BF16))
    o_ref[...] = x + _dot(jnp.concatenate(outs, axis=1), wo_ref[...])


def cross_attention(x, g, wq, kv, wo, batch, seq):
    m, d = x.shape
    mem_len = kv.shape[0] // batch
    tm = min(seq, 512)
    nt = seq // tm
    return pl.pallas_call(
        _cross_kernel,
        out_shape=jax.ShapeDtypeStruct((m, d), F32),
        grid=(batch, nt),
        in_specs=[pl.BlockSpec((tm, d), lambda b, i: (b * nt + i, 0)),
                  pl.BlockSpec((1, d), lambda b, i: (0, 0)),
                  pl.BlockSpec((d, d), lambda b, i: (0, 0)),
                  pl.BlockSpec((mem_len, d), lambda b, i: (b, 0)),
                  pl.BlockSpec((mem_len, d), lambda b, i: (b, 1)),
                  pl.BlockSpec((d, d), lambda b, i: (0, 0))],
        out_specs=pl.BlockSpec((tm, d), lambda b, i: (b * nt + i, 0)),
        compiler_params=_cparams(("parallel", "arbitrary")),
        name="cross_attention",
    )(x, g, wq, kv, kv, wo)


FFN_TF = 512


def _swiglu(h, wg, wu):
    a = _dot(h, wg)
    return a * (1.0 / (1.0 + jnp.exp(-a))) * _dot(h, wu)


def _ffn_kernel(x_ref, g_ref, wg_ref, wu_ref, wd_ref, o_ref, h_ref, acc_ref):
    f = pl.program_id(1)

    @pl.when(f == 0)
    def _():
        x = x_ref[...]
        h_ref[...] = _rms(x, g_ref[...]).astype(BF16)
        acc_ref[...] = x

    acc_ref[...] += _dot(_swiglu(h_ref[...], wg_ref[...], wu_ref[...]).astype(BF16), wd_ref[...])

    @pl.when(f == pl.num_programs(1) - 1)
    def _():
        o_ref[...] = acc_ref[...]


def ffn_dense(x, g, w_gu, w_down):
    m, d = x.shape
    dff = w_down.shape[0]
    tm = min(m, 1024)
    nf = dff // FFN_TF
    return pl.pallas_call(
        _ffn_kernel,
        out_shape=jax.ShapeDtypeStruct((m, d), F32),
        grid=(m // tm, nf),
        in_specs=[pl.BlockSpec((tm, d), lambda i, f: (i, 0)),
                  pl.BlockSpec((1, d), lambda i, f: (0, 0)),
                  pl.BlockSpec((d, FFN_TF), lambda i, f: (0, f)),
                  pl.BlockSpec((d, FFN_TF), lambda i, f: (0, nf + f)),
                  pl.BlockSpec((FFN_TF, d), lambda i, f: (f, 0))],
        out_specs=pl.BlockSpec((tm, d), lambda i, f: (i, 0)),
        scratch_shapes=[pltpu.VMEM((tm, d), BF16), pltpu.VMEM((tm, d), F32)],
        compiler_params=_cparams(("parallel", "arbitrary")),
        name="ffn_dense",
    )(x, g, w_gu, w_gu, w_down)


MOE_TR = 512
MOE_TG = 1024
MOE_ALIGN = 16
MOE_SUB = 128
MOE_KSUB = 256
R_I1, R_I2, R_W1, R_W2, R_LR1, R_LR2 = range(6)


def _route_dispatch_kernel(x_ref, g_ref, rw_ref, xs_hbm, route_ref, starts_ref, seg_ref,
                           buf_ref, sem_ref, cnt_ref, *, n_experts, region):
    i = pl.program_id(0)
    tr = MOE_TR

    @pl.when(i == 0)
    def _():
        for e in range(n_experts):
            cnt_ref[e] = 0

    hn = _rms(x_ref[...], g_ref[...])
    logits = jnp.dot(hn, rw_ref[...], precision=lax.Precision.HIGHEST, preferred_element_type=F32)
    lane = lax.broadcasted_iota(I32, (1, LANES), 1).astype(F32)
    ninf = jnp.float32(-jnp.inf)
    lg = jnp.where(lane < n_experts, logits, ninf)
    m1 = jnp.max(lg, axis=1, keepdims=True)
    i1 = jnp.min(jnp.where(lg == m1, lane, float(LANES)), axis=1, keepdims=True)
    lg2 = jnp.where(lane == i1, ninf, lg)
    m2 = jnp.max(lg2, axis=1, keepdims=True)
    i2 = jnp.min(jnp.where(lg2 == m2, lane, float(LANES)), axis=1, keepdims=True)
    e2 = jnp.exp(m2 - m1)
    den = 1.0 + e2
    oh1 = lane == i1
    oh2 = lane == i2
    both = jnp.where(jnp.logical_or(oh1, oh2), 1.0, 0.0)
    tri = (lax.broadcasted_iota(I32, (tr, tr), 0) > lax.broadcasted_iota(I32, (tr, tr), 1))
    prefix = _dot(jnp.where(tri, 1.0, 0.0).astype(BF16), both.astype(BF16))
    lr1 = jnp.sum(jnp.where(oh1, prefix, 0.0), axis=1, keepdims=True)
    lr2 = jnp.sum(jnp.where(oh2, prefix, 0.0), axis=1, keepdims=True)
    record = jnp.zeros((tr, LANES), F32)
    for col, val in ((R_I1, i1), (R_I2, i2), (R_W1, 1.0 / den), (R_W2, e2 / den), (R_LR1, lr1), (R_LR2, lr2)):
        record = jnp.where(lane == col, val, record)
    route_ref[...] = record
    counts = jnp.sum(both, axis=0, keepdims=True)

    rec_t = record.T
    i1_t, i2_t = rec_t[R_I1:R_I1 + 1, :], rec_t[R_I2:R_I2 + 1, :]
    lr1_t, lr2_t = rec_t[R_LR1:R_LR1 + 1, :], rec_t[R_LR2:R_LR2 + 1, :]
    h = hn.astype(BF16)
    lane_i = lax.broadcasted_iota(I32, (1, LANES), 1)
    starts = jnp.zeros((1, LANES), I32)
    segs = jnp.zeros((1, LANES), I32)

    def block_copy(buf_slot, start):
        return pltpu.make_async_copy(buf_ref.at[buf_slot],
                                     xs_hbm.at[pl.ds(pl.multiple_of(start, MOE_ALIGN), tr), :],
                                     sem_ref.at[buf_slot])

    for e in range(n_experts):
        in_e1 = i1_t == float(e)
        lr = jnp.where(in_e1, lr1_t, jnp.where(i2_t == float(e), lr2_t, -1.0))
        seg = ((counts[0, e].astype(I32) + (MOE_ALIGN - 1)) // MOE_ALIGN) * MOE_ALIGN
        buf_slot = e % 2
        if e >= 2:
            block_copy(buf_slot, 0).wait()
        else:
            @pl.when(i > 0)
            def _(buf_slot=buf_slot):
                block_copy(buf_slot, 0).wait()
        for sb in range(tr // MOE_SUB):
            rows = slice(sb * MOE_SUB, (sb + 1) * MOE_SUB)

            @pl.when(sb * MOE_SUB < seg)
            def _(sb=sb, rows=rows, lr=lr, buf_slot=buf_slot):
                slot = (lax.broadcasted_iota(I32, (MOE_SUB, tr), 0) + sb * MOE_SUB).astype(F32)
                perm = jnp.where(slot == lr, 1.0, 0.0).astype(BF16)
                buf_ref[buf_slot, rows, :] = _dot(perm, h).astype(BF16)

            @pl.when(sb * MOE_SUB >= seg)
            def _(rows=rows, buf_slot=buf_slot):
                buf_ref[buf_slot, rows, :] = jnp.zeros((MOE_SUB, h.shape[1]), BF16)
        start = e * region + cnt_ref[e]
        block_copy(buf_slot, start).start()
        starts = jnp.where(lane_i == e, start, starts)
        segs = jnp.where(lane_i == e, seg, segs)
        cnt_ref[e] = cnt_ref[e] + seg

    starts_ref[0] = starts
    seg_ref[0] = segs

    @pl.when(i == pl.num_programs(0) - 1)
    def _():
        block_copy(0, 0).wait()
        block_copy(1, 0).wait()
        buf_ref[0] = jnp.zeros(buf_ref.shape[1:], BF16)
        for e in range(n_experts):
            for k in range(MOE_TG // tr + 1):
                tail = block_copy(0, e * region + jnp.minimum(cnt_ref[e] + k * tr, region - tr))
                tail.start()
                tail.wait()


def moe_route_dispatch(x, g, router_w_padded, n_experts):
    m, d = x.shape
    tr = MOE_TR
    n_tiles = m // tr
    region = m + MOE_TG
    kern = functools.partial(_route_dispatch_kernel, n_experts=n_experts, region=region)
    return pl.pallas_call(
        kern,
        out_shape=(jax.ShapeDtypeStruct((n_experts * region, d), BF16),
                   jax.ShapeDtypeStruct((m, LANES), F32),
                   jax.ShapeDtypeStruct((n_tiles, 1, LANES), I32),
                   jax.ShapeDtypeStruct((n_tiles, 1, LANES), I32)),
        grid=(n_tiles,),
        in_specs=[pl.BlockSpec((tr, d), lambda i: (i, 0)),
                  pl.BlockSpec((1, d), lambda i: (0, 0)),
                  pl.BlockSpec((d, LANES), lambda i: (0, 0))],
        out_specs=(pl.BlockSpec(memory_space=pl.ANY),
                   pl.BlockSpec((tr, LANES), lambda i: (i, 0)),
                   pl.BlockSpec((1, 1, LANES), lambda i: (i, 0, 0)),
                   pl.BlockSpec((1, 1, LANES), lambda i: (i, 0, 0))),
        scratch_shapes=[pltpu.VMEM((2, tr, d), BF16), pltpu.SemaphoreType.DMA((2,)),
                        pltpu.SMEM((n_experts,), I32)],
        compiler_params=_cparams(("arbitrary",)),
        name="moe_route_dispatch",
    )(x, g, router_w_padded)


def _ffn_grouped_kernel(trow_ref, texp_ref, tval_ref, x_ref, wg_ref, wu_ref, wd_ref, o_ref, acc_ref):
    t = pl.program_id(0)
    f = pl.program_id(1)

    @pl.when(jnp.logical_and(tval_ref[t] == TILE_ZERO, f == 0))
    def _():
        o_ref[...] = jnp.zeros(o_ref.shape, o_ref.dtype)

    @pl.when(tval_ref[t] == TILE_COMPUTE)
    def _():
        part = _dot(_swiglu(x_ref[...], wg_ref[0], wu_ref[0]).astype(BF16), wd_ref[0])

        @pl.when(f == 0)
        def _():
            acc_ref[...] = part

        @pl.when(f > 0)
        def _():
            acc_ref[...] += part

        @pl.when(f == pl.num_programs(1) - 1)
        def _():
            o_ref[...] = acc_ref[...].astype(o_ref.dtype)


def moe_grouped_ffn(xs, w_gu, w_down, trow, texp, tval):
    rows, d = xs.shape
    dff = w_down.shape[1]
    nf = dff // FFN_TF
    n_steps = trow.shape[0]
    col = lambda f, tv, t: jnp.where(tv[t] == TILE_COMPUTE, f, nf - 1)
    grid_spec = pltpu.PrefetchScalarGridSpec(
        num_scalar_prefetch=3,
        grid=(n_steps, nf),
        in_specs=[pl.BlockSpec((MOE_TG, d), lambda t, f, tr_, te, tv: (tr_[t], 0)),
                  pl.BlockSpec((1, d, FFN_TF), lambda t, f, tr_, te, tv: (te[t], 0, col(f, tv, t))),
                  pl.BlockSpec((1, d, FFN_TF), lambda t, f, tr_, te, tv: (te[t], 0, nf + col(f, tv, t))),
                  pl.BlockSpec((1, FFN_TF, d), lambda t, f, tr_, te, tv: (te[t], col(f, tv, t), 0))],
        out_specs=pl.BlockSpec((MOE_TG, d), lambda t, f, tr_, te, tv: (tr_[t], 0)),
        scratch_shapes=[pltpu.VMEM((MOE_TG, d), F32)])
    return pl.pallas_call(
        _ffn_grouped_kernel,
        out_shape=jax.ShapeDtypeStruct((rows, d), BF16),
        grid_spec=grid_spec,
        compiler_params=_cparams(("arbitrary", "arbitrary")),
        name="moe_grouped_ffn",
    )(trow, texp, tval, xs, w_gu, w_gu, w_down)


def _moe_combine_kernel(starts_ref, seg_ref, x_ref, route_ref, gain_ref, ys_hbm, o_ref, buf_ref, sem_ref,
                        *, n_experts, final_norm_fused):
    i = pl.program_id(0)
    tr = MOE_TR

    def block_copy(e):
        start = starts_ref[i * n_experts + e]
        return pltpu.make_async_copy(ys_hbm.at[pl.ds(pl.multiple_of(start, MOE_ALIGN), tr), :],
                                     buf_ref.at[e], sem_ref.at[e])

    for e in range(n_experts):
        block_copy(e).start()
    rec = route_ref[...]
    i1, i2 = rec[:, R_I1:R_I1 + 1], rec[:, R_I2:R_I2 + 1]
    w1, w2 = rec[:, R_W1:R_W1 + 1], rec[:, R_W2:R_W2 + 1]
    lr1, lr2 = rec[:, R_LR1:R_LR1 + 1], rec[:, R_LR2:R_LR2 + 1]
    o_ref[...] = x_ref[...]
    for e in range(n_experts):
        in_e1 = i1 == float(e)
        in_e2 = i2 == float(e)
        lr = jnp.where(in_e1, lr1, jnp.where(in_e2, lr2, -1.0))
        w = jnp.where(in_e1, w1, jnp.where(in_e2, w2, 0.0))
        seg = seg_ref[i * n_experts + e]
        block_copy(e).wait()
        for kb in range(tr // MOE_KSUB):
            @pl.when(kb * MOE_KSUB < seg)
            def _(kb=kb, lr=lr, w=w, e=e):
                slot = (lax.broadcasted_iota(I32, (tr, MOE_KSUB), 1) + kb * MOE_KSUB).astype(F32)
                pick = jnp.where(slot == lr, 1.0, 0.0).astype(BF16)
                o_ref[...] += w * _dot(pick, buf_ref[e, kb * MOE_KSUB:(kb + 1) * MOE_KSUB, :])
    if final_norm_fused:
        o_ref[...] = _rms(o_ref[...], gain_ref[...])


def moe_combine(x, route, ys, starts_flat, seg_flat, n_experts, final_gain):
    m, d = x.shape
    tr = MOE_TR
    fused = final_gain is not None
    gain = final_gain if fused else jnp.ones((1, d), F32)
    grid_spec = pltpu.PrefetchScalarGridSpec(
        num_scalar_prefetch=2,
        grid=(m // tr,),
        in_specs=[pl.BlockSpec((tr, d), lambda i, s, g: (i, 0)),
                  pl.BlockSpec((tr, LANES), lambda i, s, g: (i, 0)),
                  pl.BlockSpec((1, d), lambda i, s, g: (0, 0)),
                  pl.BlockSpec(memory_space=pl.ANY)],
        out_specs=pl.BlockSpec((tr, d), lambda i, s, g: (i, 0)),
        scratch_shapes=[pltpu.VMEM((n_experts, tr, d), BF16), pltpu.SemaphoreType.DMA((n_experts,))])
    return pl.pallas_call(
        functools.partial(_moe_combine_kernel, n_experts=n_experts, final_norm_fused=fused),
        out_shape=jax.ShapeDtypeStruct((m, d), F32),
        grid_spec=grid_spec,
        compiler_params=_cparams(("arbitrary",)),
        name="moe_combine",
    )(starts_flat, seg_flat, x, route, gain, ys)


TILE_DEAD, TILE_COMPUTE, TILE_ZERO = 0, 1, 2


def _moe_tile_plan(seg, n_experts, region, n_steps):
    tiles_per_region = region // MOE_TG
    rows = jnp.sum(seg, axis=0)
    n_compute = (rows + MOE_TG - 1) // MOE_TG
    n_live = jnp.minimum((rows + MOE_TR + MOE_TG - 1) // MOE_TG, tiles_per_region)
    ends = jnp.cumsum(n_live)
    total = ends[-1]
    step = jnp.arange(n_steps, dtype=I32)
    t = jnp.minimum(step, total - 1)
    texp = jnp.sum((t[:, None] >= ends[None, :]).astype(I32), axis=1)
    within = t - (ends - n_live)[texp]
    trow = texp * tiles_per_region + within
    kind = jnp.where(step < total, jnp.where(within < n_compute[texp], TILE_COMPUTE, TILE_ZERO), TILE_DEAD)
    return trow.astype(I32), texp.astype(I32), kind.astype(I32)


def moe_ffn(x, g, router_w, w_gu, w_down, final_gain=None):
    m, d = x.shape
    n_experts = router_w.shape[-1]
    region = m + MOE_TG
    n_tiles = m // MOE_TR
    rw = jnp.pad(router_w, ((0, 0), (0, LANES - n_experts)))
    xs, route, starts, seg = moe_route_dispatch(x, g, rw, n_experts)
    starts = starts[:, 0, :n_experts]
    seg = seg[:, 0, :n_experts]
    max_rows = TOP_K * m + n_tiles * n_experts * (MOE_ALIGN - 1)
    n_steps = max_rows // MOE_TG + 2 * n_experts
    trow, texp, tval = _moe_tile_plan(seg, n_experts, region, n_steps)
    ys = moe_grouped_ffn(xs, w_gu, w_down, trow, texp, tval)
    return moe_combine(x, route, ys, starts.reshape(-1), seg.reshape(-1), n_experts, final_gain)


def _final_norm_kernel(x_ref, g_ref, o_ref):
    o_ref[...] = _rms(x_ref[...], g_ref[...])


def final_norm(x, g):
    m, d = x.shape
    tm = min(m, 1024)
    return pl.pallas_call(
        _final_norm_kernel,
        out_shape=jax.ShapeDtypeStruct((m, d), F32),
        grid=(m // tm,),
        in_specs=[pl.BlockSpec((tm, d), lambda i: (i, 0)), pl.BlockSpec((1, d), lambda i: (0, 0))],
        out_specs=pl.BlockSpec((tm, d), lambda i: (i, 0)),
        compiler_params=_cparams(("parallel",)),
        name="final_norm",
    )(x, g)


def _rotate_half_cols(w):
    d = w.shape[0]
    w4 = w.reshape(d, -1, 2, HEAD_DIM // 2)
    return jnp.stack([-w4[:, :, 1], w4[:, :, 0]], axis=2).reshape(d, -1)


def _dup_heads(w):
    d = w.shape[0]
    w3 = w.reshape(d, -1, 1, HEAD_DIM)
    return jnp.concatenate([w3, w3], axis=2).reshape(d, -1)


def _split_w_in(w_in):
    sizes = (SWA_HEADS * HEAD_DIM, SWA_KV_HEADS * HEAD_DIM, SWA_KV_HEADS * HEAD_DIM,
             DSA_HEADS * HEAD_DIM, DSA_KV_HEADS * HEAD_DIM, DSA_KV_HEADS * HEAD_DIM,
             IDX_HEADS * IDX_DIM, IDX_DIM, IDX_HEADS,
             RET_HEADS * RET_KEY_DIM, RET_HEADS * RET_KEY_DIM, RET_HEADS * RET_VAL_DIM, RET_HEADS * RET_VAL_DIM,
             len(POOL_WINDOWS) * POOL_GROUP_DIM, N_BRANCH * w_in.shape[0])
    parts, off = [], 0
    for s in sizes:
        parts.append(w_in[:, off:off + s])
        off += s
    return parts


def _layout_w_in(w_in):
    d = w_in.shape[0]
    (a_q, a_k, a_v, b_q, b_k, b_v, i_q, i_k, i_w, r_q, r_k, r_v, r_g, p_u, gates) = _split_w_in(w_in)
    qs = HEAD_DIM ** -0.5
    rope_cols = [a_q * qs, b_q * qs, i_q * (IDX_DIM ** -0.5), r_q, r_k * (RET_KEY_DIM ** -0.5),
                 _dup_heads(a_k), _dup_heads(b_k), _dup_heads(i_k)]
    wx = jnp.concatenate(rope_cols, axis=1)
    wp = jnp.concatenate([_rotate_half_cols(c) for c in rope_cols], axis=1)
    pad = PA_COLS - wx.shape[1]
    wx = jnp.pad(wx, ((0, 0), (0, pad)))
    wp = jnp.pad(wp, ((0, 0), (0, pad)))
    nt = PA_COLS // PA_TN
    w_rope = jnp.concatenate([wx.reshape(d, nt, 1, PA_TN), wp.reshape(d, nt, 1, PA_TN)], axis=2)
    w_rope = w_rope.reshape(d, 2 * PA_COLS).astype(BF16)
    iw_pad = jnp.pad(i_w * (IDX_HEADS ** -0.5), ((0, 0), (0, LANES - IDX_HEADS)))
    w_plain = jnp.concatenate([gates, r_v, r_g, p_u, _dup_heads(a_v), _dup_heads(b_v), iw_pad], axis=1)
    return w_rope, w_plain.astype(BF16)


def kernel(x, mem, positions, norm_mix, w_in, attn_sink, w_branch, w_out, pool_w, pool_scale, norm_cross, norm_mem, w_xq, w_xkv, w_xo, norm_ffn, dense_w_gu, dense_w_down, router_w, moe_w_gu, moe_w_down, norm_final):
    batch, seq, d = x.shape
    depth = norm_mix.shape[0]
    m = batch * seq
    mem_len = mem.shape[1]
    xs = x.reshape(m, d)
    mem2 = mem.reshape(batch * mem_len, d)

    inv = 1.0 / (ROPE_THETA ** (jnp.arange(0, HEAD_DIM, 2, dtype=F32) / HEAD_DIM))
    inv128 = jnp.tile(inv, LANES // (HEAD_DIM // 2))[None, :]
    cos, sin = rope_tables(positions.reshape(m, 1), inv128)
    log_gamma = jnp.log1p(-(2.0 ** (-5.0 - jnp.arange(RET_HEADS, dtype=F32))))
    x_scale = (d // X_HEADS) ** -0.5

    for layer in range(depth):
        w_rope, w_plain = _layout_w_in(w_in[layer])
        pa = norm_matmul_rope(xs, norm_mix[layer][None, :], w_rope, cos, sin, tm=1024, tn=PA_TN)
        pb = norm_matmul(xs, norm_mix[layer][None, :], w_plain, tm=1024, tn=PB_TN, name="in_proj_plain")
        o_a = swa_attention(pa, pb, attn_sink[layer], batch, seq)
        o_b = dsa_attention(pa, pb, batch, seq)
        o_c = retention(pa, pb, log_gamma, batch, seq)
        o_d = multiscale_pool(pb, pool_w[layer].astype(BF16), pool_scale[layer][None, :], batch, seq)
        xs = merge_branches(xs, (o_a, o_b, o_c, o_d), pb, w_branch[layer].astype(BF16), w_out[layer].astype(BF16))

        kv = norm_matmul(mem2, norm_mem[layer][None, :], w_xkv[layer].astype(BF16),
                         tm=batch * mem_len, tn=512, name="mem_kv_proj")
        xs = cross_attention(xs, norm_cross[layer][None, :], (w_xq[layer] * x_scale).astype(BF16), kv,
                             w_xo[layer].astype(BF16), batch, seq)

        li = layer // 2
        if layer % 2 == 0:
            xs = ffn_dense(xs, norm_ffn[layer][None, :], dense_w_gu[li].astype(BF16), dense_w_down[li].astype(BF16))
        else:
            last = layer == depth - 1
            xs = moe_ffn(xs, norm_ffn[layer][None, :], router_w[li], moe_w_gu[li].astype(BF16),
                         moe_w_down[li].astype(BF16), final_gain=norm_final[None, :] if last else None)

    if depth % 2 == 1:
        xs = final_norm(xs, norm_final[None, :])
    return xs.reshape(batch, seq, d)
```

```python
import functools

import jax
import jax.numpy as jnp
from jax import lax
from jax.experimental import pallas as pl
from jax.experimental.pallas import tpu as pltpu

F32 = jnp.float32
BF16 = jnp.bfloat16
I32 = jnp.int32

EPS = 1e-6
HEAD_DIM = 64
ROPE_THETA = 10000.0
BLOCK = 128
SWA_HEADS, SWA_KV_HEADS = 8, 2
DSA_HEADS, DSA_KV_HEADS = 8, 2
IDX_HEADS, IDX_DIM = 4, 64
DSA_TOPK = 256
RET_HEADS, RET_KEY_DIM, RET_VAL_DIM = 4, 64, 128
RET_CHUNK = 256
POOL_WINDOWS = (2, 4, 8, 16)
POOL_GROUP_DIM = 128
N_BRANCH = 4
BRANCH_WIDTH = 512
X_HEADS = 4
TOP_K = 2

LANES = 128
VMEM_LIMIT = 56 * 1024 * 1024
NEG_BIG = -1e30
LOG2_E = 1.4426950408889634
INT_MIN = -2 ** 31

PA_AQ, PA_BQ, PA_IQ, PA_RQ, PA_RK, PA_AK, PA_BK, PA_IK = 0, 512, 1024, 1280, 1536, 1792, 2048, 2304
PA_COLS, PA_TN = 2560, 1280
PB_GATES, PB_RV, PB_RG, PB_PU, PB_AV, PB_BV, PB_IW = 0, 4096, 4608, 5120, 5632, 5888, 6144
PB_COLS, PB_TN = 6272, 896


def _cparams(sem):
    return pltpu.CompilerParams(dimension_semantics=sem, vmem_limit_bytes=VMEM_LIMIT)


def _dot(a, b):
    return jnp.dot(a, b, preferred_element_type=F32)


def _dot_nt(a, b):
    return lax.dot_general(a, b, (((1,), (1,)), ((), ())), preferred_element_type=F32)


def _rms(x, g):
    return x * lax.rsqrt(jnp.mean(x * x, axis=-1, keepdims=True) + EPS) * g


def _half_mask(hh):
    lane = lax.broadcasted_iota(I32, (1, LANES), 1)
    return (lane >= HEAD_DIM) if hh else (lane < HEAD_DIM)


def _keep_half(block, hh):
    return jnp.where(_half_mask(hh), block, jnp.zeros_like(block))


def _rope_kernel(pos_ref, inv_ref, cos_ref, sin_ref):
    ang = pos_ref[...].astype(F32) * inv_ref[...]
    cos_ref[...] = jnp.cos(ang)
    sin_ref[...] = jnp.sin(ang)


def rope_tables(pos_col, inv128):
    m = pos_col.shape[0]
    tm = min(m, 2048)
    return pl.pallas_call(
        _rope_kernel,
        out_shape=(jax.ShapeDtypeStruct((m, LANES), F32), jax.ShapeDtypeStruct((m, LANES), F32)),
        grid=(m // tm,),
        in_specs=[pl.BlockSpec((tm, 1), lambda i: (i, 0)), pl.BlockSpec((1, LANES), lambda i: (0, 0))],
        out_specs=(pl.BlockSpec((tm, LANES), lambda i: (i, 0)), pl.BlockSpec((tm, LANES), lambda i: (i, 0))),
        compiler_params=_cparams(("parallel",)),
        name="rope_tables",
    )(pos_col, inv128)


def _norm_matmul_kernel(x_ref, g_ref, w_ref, o_ref, h_ref):
    @pl.when(pl.program_id(1) == 0)
    def _():
        h_ref[...] = _rms(x_ref[...], g_ref[...]).astype(BF16)

    o_ref[...] = _dot(h_ref[...], w_ref[...]).astype(o_ref.dtype)


def norm_matmul(x, g, w, tm, tn, out_dtype=BF16, name="norm_matmul"):
    m, k = x.shape
    n = w.shape[1]
    tm = min(tm, m)
    return pl.pallas_call(
        _norm_matmul_kernel,
        out_shape=jax.ShapeDtypeStruct((m, n), out_dtype),
        grid=(m // tm, n // tn),
        in_specs=[pl.BlockSpec((tm, k), lambda i, j: (i, 0)),
                  pl.BlockSpec((1, k), lambda i, j: (0, 0)),
                  pl.BlockSpec((k, tn), lambda i, j: (0, j))],
        out_specs=pl.BlockSpec((tm, tn), lambda i, j: (i, j)),
        scratch_shapes=[pltpu.VMEM((tm, k), BF16)],
        compiler_params=_cparams(("parallel", "arbitrary")),
        name=name,
    )(x, g, w)


def _norm_matmul_rope_kernel(x_ref, g_ref, wx_ref, wr_ref, cos_ref, sin_ref, o_ref, h_ref):
    @pl.when(pl.program_id(1) == 0)
    def _():
        h_ref[...] = _rms(x_ref[...], g_ref[...]).astype(BF16)

    h = h_ref[...]
    proj = _dot(h, wx_ref[...])
    rot = _dot(h, wr_ref[...])
    cos = cos_ref[...]
    sin = sin_ref[...]
    for c in range(o_ref.shape[1] // LANES):
        cols = slice(c * LANES, (c + 1) * LANES)
        o_ref[:, cols] = (proj[:, cols] * cos + rot[:, cols] * sin).astype(o_ref.dtype)


def norm_matmul_rope(x, g, wx, wr, cos, sin, tm, tn):
    m, k = x.shape
    n = wx.shape[1]
    tm = min(tm, m)
    return pl.pallas_call(
        _norm_matmul_rope_kernel,
        out_shape=jax.ShapeDtypeStruct((m, n), BF16),
        grid=(m // tm, n // tn),
        in_specs=[pl.BlockSpec((tm, k), lambda i, j: (i, 0)),
                  pl.BlockSpec((1, k), lambda i, j: (0, 0)),
                  pl.BlockSpec((k, tn), lambda i, j: (0, j)),
                  pl.BlockSpec((k, tn), lambda i, j: (0, j)),
                  pl.BlockSpec((tm, LANES), lambda i, j: (i, 0)),
                  pl.BlockSpec((tm, LANES), lambda i, j: (i, 0))],
        out_specs=pl.BlockSpec((tm, tn), lambda i, j: (i, j)),
        scratch_shapes=[pltpu.VMEM((tm, k), BF16)],
        compiler_params=_cparams(("parallel", "arbitrary")),
        name="in_proj_rope",
    )(x, g, wx, wr, cos, sin)


def _swa_kernel(sink_ref, q_ref, kc_ref, kp_ref, vc_ref, vp_ref, o_ref):
    i = pl.program_id(1)
    per_group = SWA_HEADS // SWA_KV_HEADS
    q_t = (q_ref[...].astype(F32) * LOG2_E).T.astype(BF16)
    key = lax.broadcasted_iota(I32, (2 * BLOCK, 1), 0)
    rel = lax.broadcasted_iota(I32, (1, BLOCK), 1) + BLOCK - key
    band = jnp.logical_and(jnp.logical_and(rel >= 0, rel < BLOCK), jnp.logical_or(key >= BLOCK, i > 0))
    bias = jnp.where(band, 0.0, -jnp.inf).astype(F32)
    bias = jnp.concatenate([bias] * per_group, axis=1)
    outs = []
    for g in range(SWA_KV_HEADS):
        cols = slice(g * LANES, (g + 1) * LANES)
        k_win = jnp.concatenate([kp_ref[:, cols], kc_ref[:, cols]], axis=0)[:, 0:HEAD_DIM]
        v_t = jnp.concatenate([vp_ref[:, cols], vc_ref[:, cols]], axis=0).astype(F32).T[0:HEAD_DIM].astype(BF16)
        heads = range(g * per_group, (g + 1) * per_group)
        q_g = jnp.concatenate([q_t[h * HEAD_DIM:(h + 1) * HEAD_DIM, :] for h in heads], axis=1)
        sink = jnp.concatenate([jnp.full((1, BLOCK), sink_ref[h] * LOG2_E, F32) for h in heads], axis=1)
        s = _dot(k_win, q_g) + bias
        m = jnp.maximum(jnp.max(s, axis=0, keepdims=True), sink)
        p = jnp.exp2(s - m)
        den = jnp.sum(p, axis=0, keepdims=True) + jnp.exp2(sink - m)
        o_t = _dot(v_t, p.astype(BF16)) / den
        outs += [o_t[:, a * BLOCK:(a + 1) * BLOCK] for a in range(per_group)]
    o_ref[...] = jnp.concatenate(outs, axis=0).T.astype(o_ref.dtype)


def swa_attention(pa, pb, sink, batch, seq):
    nb = seq // BLOCK
    cur = lambda col: (lambda b, i: (b * nb + i, col))
    prev = lambda col: (lambda b, i: (b * nb + jnp.maximum(i - 1, 0), col))
    return pl.pallas_call(
        _swa_kernel,
        out_shape=jax.ShapeDtypeStruct((batch * seq, BRANCH_WIDTH), BF16),
        grid=(batch, nb),
        in_specs=[pl.BlockSpec(memory_space=pltpu.SMEM),
                  pl.BlockSpec((BLOCK, 512), cur(PA_AQ // 512)),
                  pl.BlockSpec((BLOCK, 256), cur(PA_AK // 256)),
                  pl.BlockSpec((BLOCK, 256), prev(PA_AK // 256)),
                  pl.BlockSpec((BLOCK, 256), cur(PB_AV // 256)),
                  pl.BlockSpec((BLOCK, 256), prev(PB_AV // 256))],
        out_specs=pl.BlockSpec((BLOCK, BRANCH_WIDTH), cur(0)),
        compiler_params=_cparams(("parallel", "arbitrary")),
        name="swa_attention",
    )(sink, pa, pa, pa, pb, pb)


DSA_TQ = 256
DSA_ONES = 16


def _dsa_kernel(q_ref, iq_ref, iw_ref, k_ref, ik_ref, v_ref, o_ref,
                skey_ref, vt_ref, qt_ref, cut_ref, m_ref, l_ref, acc_ref, alpha_ref, bias_ref, s_ref, p_ref,
                *, n_sel, seq):
    tq = DSA_TQ
    nq = seq // tq
    i = pl.program_id(1)
    n_chunks = i + 1
    q_pos = i * tq + lax.broadcasted_iota(I32, (1, tq), 1)
    key_off = lax.broadcasted_iota(I32, (tq, 1), 0)

    @pl.when(i == 0)
    def _():
        for jj in range(nq):
            vt = v_ref[jj * tq:(jj + 1) * tq, :].astype(F32).T
            ones = jnp.ones((DSA_ONES, tq), BF16)
            vt_ref[jj, 0, 0:HEAD_DIM, :] = vt[0:HEAD_DIM].astype(BF16)
            vt_ref[jj, 0, HEAD_DIM:, :] = ones
            vt_ref[jj, 1, 0:HEAD_DIM, :] = vt[2 * HEAD_DIM:3 * HEAD_DIM].astype(BF16)
            vt_ref[jj, 1, HEAD_DIM:, :] = ones

    iq_t = iq_ref[...].astype(F32).T.astype(BF16)
    w_t = iw_ref[...].astype(F32).T

    def score_chunk(j, carry):
        ikc = ik_ref[pl.ds(pl.multiple_of(j * tq, tq), tq), 0:IDX_DIM]
        sc = jnp.zeros((tq, tq), F32)
        for h in range(IDX_HEADS):
            sc = sc + jnp.maximum(_dot(ikc, iq_t[h * IDX_DIM:(h + 1) * IDX_DIM, :]), 0.0) * w_t[h:h + 1, :]
        sc = jnp.where(sc == 0.0, 0.0, sc)
        bits = lax.bitcast_convert_type(sc, I32)
        key = jnp.where(bits < 0, bits ^ jnp.int32(0x7FFFFFFF), bits)
        causal = (j * tq + key_off) <= q_pos
        skey_ref[j] = jnp.where(causal, key, jnp.int32(INT_MIN))
        return carry

    lax.fori_loop(0, n_chunks, score_chunk, 0)

    view = (tq // 8, 8, tq)
    pos_in_chunk = lax.broadcasted_iota(I32, view, 0) * 8 + lax.broadcasted_iota(I32, view, 1)

    def all_sublanes(x, op):
        for sh in (4, 2, 1):
            x = op(x, pltpu.roll(x, sh, axis=0))
        return x

    def count_keys(pred):
        def body(j, acc):
            x = pred(skey_ref[j].reshape(view), j)
            part = view[0] // 4
            sums = [jnp.sum(x[a * part:(a + 1) * part], axis=0) for a in range(4)]
            return acc + ((sums[0] + sums[1]) + (sums[2] + sums[3]))
        return all_sublanes(lax.fori_loop(0, n_chunks, body, jnp.zeros((8, tq), F32)), jnp.add)

    one = jnp.float32(1.0)
    zero = jnp.float32(0.0)
    k_f = jnp.float32(n_sel)
    cut_bits = max((seq - 1).bit_length(), 1)

    def bit_step(b, carry):
        res_u, c_ge = carry
        cand_u = res_u | lax.shift_left(jnp.int32(1), 31 - b)
        cand_s = cand_u ^ jnp.int32(INT_MIN)
        cnt = count_keys(lambda kc, j: jnp.where(kc >= cand_s, one, zero))
        take = cnt >= k_f
        return jnp.where(take, cand_u, res_u), jnp.where(take, cnt, c_ge)

    res_u, c_ge = lax.fori_loop(0, 32, bit_step, (jnp.zeros((8, tq), I32), jnp.zeros((8, tq), F32)))
    thr = res_u ^ jnp.int32(INT_MIN)
    short = thr == jnp.int32(INT_MIN)
    cut_ref[...] = jnp.where(short, jnp.int32(-1), jnp.int32(seq))
    unresolved = jnp.where(jnp.logical_and(c_ge != k_f, jnp.logical_not(short)), one, zero)

    @pl.when(jnp.max(unresolved) > 0.0)
    def _():
        need = k_f - count_keys(lambda kc, j: jnp.where(kc > thr, one, zero))

        def cut_step(b, ans):
            cand = ans | lax.shift_left(jnp.int32(1), cut_bits - 1 - b)
            lim = cand - 1

            def pred(kc, j):
                return jnp.where(kc == thr, jnp.where((j * tq + pos_in_chunk) <= lim, one, zero), zero)

            return jnp.where(count_keys(pred) < need, cand, ans)

        ans = lax.fori_loop(0, cut_bits, cut_step, jnp.zeros((8, tq), I32))
        cut_ref[...] = jnp.where(short, jnp.int32(-1), ans)

    qt_ref[...] = (q_ref[...].astype(F32) * LOG2_E).T.astype(BF16)
    m_ref[...] = jnp.full(m_ref.shape, NEG_BIG, F32)
    l_ref[...] = jnp.zeros(l_ref.shape, F32)
    acc_ref[...] = jnp.zeros(acc_ref.shape, F32)
    cut = cut_ref[...]
    neg = jnp.float32(NEG_BIG)
    group = DSA_HEADS // DSA_KV_HEADS

    def attend(j, carry):
        kc = skey_ref[j].reshape(view)
        start = pl.multiple_of(j * tq, tq)
        tie = jnp.where(kc == thr, jnp.where((j * tq + pos_in_chunk) <= cut, zero, neg), neg)
        bias_ref[...] = jnp.where(kc > thr, zero, tie).reshape(tq, tq)
        for h in range(DSA_HEADS):
            g = h // group
            kblk = k_ref[pl.ds(start, tq), g * LANES:g * LANES + HEAD_DIM]
            s = _dot(kblk, qt_ref[h * HEAD_DIM:(h + 1) * HEAD_DIM, :]) + bias_ref[...]
            s_ref[h] = s
            m_old = m_ref[h]
            m_new = jnp.maximum(m_old, jnp.max(s, axis=0, keepdims=True))
            alpha_ref[h] = jnp.exp2(m_old - m_new)
            m_ref[h] = m_new
        for h in range(DSA_HEADS):
            p_ref[h] = jnp.exp2(s_ref[h] - m_ref[h]).astype(BF16)
        for h in range(DSA_HEADS):
            pv = _dot(vt_ref[j, h // group], p_ref[h])
            acc_ref[h] = alpha_ref[h] * acc_ref[h] + pv[0:HEAD_DIM]
            l_ref[h] = alpha_ref[h] * l_ref[h] + pv[HEAD_DIM:HEAD_DIM + 1]
        return carry

    lax.fori_loop(0, n_chunks, attend, 0)

    for p in range(DSA_HEADS // 2):
        o_t = jnp.concatenate([acc_ref[2 * p] / l_ref[2 * p], acc_ref[2 * p + 1] / l_ref[2 * p + 1]], axis=0)
        o_ref[:, p * LANES:(p + 1) * LANES] = o_t.T.astype(o_ref.dtype)


def dsa_attention(pa, pb, batch, seq):
    tq = DSA_TQ
    nq = seq // tq
    n_sel = min(DSA_TOPK, seq // 4)
    qmap = lambda col: (lambda b, i: (b * nq + i, col))
    smap = lambda col: (lambda b, i: (b, col))
    return pl.pallas_call(
        functools.partial(_dsa_kernel, n_sel=n_sel, seq=seq),
        out_shape=jax.ShapeDtypeStruct((batch * seq, BRANCH_WIDTH), BF16),
        grid=(batch, nq),
        in_specs=[pl.BlockSpec((tq, 512), qmap(PA_BQ // 512)),
                  pl.BlockSpec((tq, 256), qmap(PA_IQ // 256)),
                  pl.BlockSpec((tq, LANES), qmap(PB_IW // LANES)),
                  pl.BlockSpec((seq, 256), smap(PA_BK // 256)),
                  pl.BlockSpec((seq, LANES), smap(PA_IK // LANES)),
                  pl.BlockSpec((seq, 256), smap(PB_BV // 256))],
        out_specs=pl.BlockSpec((tq, BRANCH_WIDTH), qmap(0)),
        scratch_shapes=[pltpu.VMEM((nq, tq, tq), I32),
                        pltpu.VMEM((nq, DSA_KV_HEADS, HEAD_DIM + DSA_ONES, tq), BF16),
                        pltpu.VMEM((DSA_HEADS * HEAD_DIM, tq), BF16),
                        pltpu.VMEM((8, tq), I32),
                        pltpu.VMEM((DSA_HEADS, 1, tq), F32),
                        pltpu.VMEM((DSA_HEADS, 1, tq), F32),
                        pltpu.VMEM((DSA_HEADS, HEAD_DIM, tq), F32),
                        pltpu.VMEM((DSA_HEADS, 1, tq), F32),
                        pltpu.VMEM((tq, tq), F32),
                        pltpu.VMEM((DSA_HEADS, tq, tq), F32),
                        pltpu.VMEM((DSA_HEADS, tq, tq), BF16)],
        compiler_params=_cparams(("parallel", "arbitrary")),
        name="dsa_attention",
    )(pa, pa, pb, pa, pa, pb)


def _retention_kernel(lg_ref, q_ref, k_ref, v_ref, g_ref, o_ref, state_ref):
    cdim = RET_CHUNK

    @pl.when(pl.program_id(1) == 0)
    def _():
        state_ref[...] = jnp.zeros(state_ref.shape, F32)

    n = lax.broadcasted_iota(I32, (cdim, 1), 0).astype(F32)
    mcol = lax.broadcasted_iota(I32, (1, cdim), 1).astype(F32)
    rel = n - mcol
    for h in range(RET_HEADS):
        lg = lg_ref[h]
        decay = jnp.where(rel >= 0, jnp.exp(lg * jnp.maximum(rel, 0.0)), 0.0)
        q_decay = jnp.exp(lg * (n + 1.0))
        k_decay = jnp.exp(lg * (cdim - 1.0 - n))
        chunk_decay = jnp.exp(jnp.full((1, LANES), lg * cdim, F32))
        blk = slice((h // 2) * LANES, (h // 2 + 1) * LANES)
        qm = _keep_half(q_ref[:, blk], h % 2)
        km = _keep_half(k_ref[:, blk], h % 2)
        v = v_ref[:, h * RET_VAL_DIM:(h + 1) * RET_VAL_DIM]
        state = state_ref[h]
        inner = _dot((_dot_nt(qm, km) * decay).astype(BF16), v)
        cross = _dot(qm, state.astype(BF16)) * q_decay
        kd_t = (km.astype(F32) * k_decay).T.astype(BF16)
        state_ref[h] = state * chunk_decay + _dot(kd_t, v)
        out = inner + cross
        out = out * lax.rsqrt(jnp.mean(out * out, axis=-1, keepdims=True) + EPS)
        gate = g_ref[:, h * RET_VAL_DIM:(h + 1) * RET_VAL_DIM].astype(F32)
        silu = gate * (1.0 / (1.0 + jnp.exp(-gate)))
        o_ref[:, h * RET_VAL_DIM:(h + 1) * RET_VAL_DIM] = (silu * out).astype(o_ref.dtype)


def retention(pa, pb, log_gamma, batch, seq):
    nc = seq // RET_CHUNK
    cmap = lambda col: (lambda b, c: (b * nc + c, col))
    return pl.pallas_call(
        _retention_kernel,
        out_shape=jax.ShapeDtypeStruct((batch * seq, BRANCH_WIDTH), BF16),
        grid=(batch, nc),
        in_specs=[pl.BlockSpec(memory_space=pltpu.SMEM),
                  pl.BlockSpec((RET_CHUNK, 256), cmap(PA_RQ // 256)),
                  pl.BlockSpec((RET_CHUNK, 256), cmap(PA_RK // 256)),
                  pl.BlockSpec((RET_CHUNK, 512), cmap(PB_RV // 512)),
                  pl.BlockSpec((RET_CHUNK, 512), cmap(PB_RG // 512))],
        out_specs=pl.BlockSpec((RET_CHUNK, BRANCH_WIDTH), cmap(0)),
        scratch_shapes=[pltpu.VMEM((RET_HEADS, LANES, RET_VAL_DIM), F32)],
        compiler_params=_cparams(("parallel", "arbitrary")),
        name="retention",
    )(log_gamma, pa, pa, pb, pb)


POOL_HALO = 16


def _pool_kernel(cur_ref, prev_ref, w_ref, scale_ref, o_ref, *, tiles_per_seq):
    tm = cur_ref.shape[0]
    it = pl.program_id(0) % tiles_per_seq
    cur = cur_ref[...].astype(F32)
    prev = jnp.where(it > 0, prev_ref[...].astype(F32), 0.0)
    ext = jnp.concatenate([prev, cur], axis=0)
    t = it * tm + lax.broadcasted_iota(I32, (tm, 1), 0)
    for gi, win in enumerate(POOL_WINDOWS):
        cols = slice(gi * POOL_GROUP_DIM, (gi + 1) * POOL_GROUP_DIM)
        s = ext[:, cols]
        sh = 1
        while sh < win:
            s = s + pltpu.roll(s, sh, axis=0)
            sh *= 2
        cnt = jnp.minimum(t + 1, win).astype(F32)
        pooled = s[POOL_HALO:, :] / cnt
        y = _dot((pooled - cur[:, cols]).astype(BF16), w_ref[gi])
        o_ref[:, cols] = (y * scale_ref[:, cols]).astype(o_ref.dtype)


def multiscale_pool(pb, pool_w, pool_scale, batch, seq):
    m = batch * seq
    tm = min(seq, 512)
    tps = seq // tm
    ratio = tm // POOL_HALO
    return pl.pallas_call(
        functools.partial(_pool_kernel, tiles_per_seq=tps),
        out_shape=jax.ShapeDtypeStruct((m, BRANCH_WIDTH), BF16),
        grid=(m // tm,),
        in_specs=[pl.BlockSpec((tm, 512), lambda i: (i, PB_PU // 512)),
                  pl.BlockSpec((POOL_HALO, 512), lambda i: (jnp.maximum(i * ratio - 1, 0), PB_PU // 512)),
                  pl.BlockSpec((len(POOL_WINDOWS), POOL_GROUP_DIM, POOL_GROUP_DIM), lambda i: (0, 0, 0)),
                  pl.BlockSpec((1, 512), lambda i: (0, 0))],
        out_specs=pl.BlockSpec((tm, BRANCH_WIDTH), lambda i: (i, 0)),
        compiler_params=_cparams(("parallel",)),
        name="multiscale_pool",
    )(pb, pb, pool_w, pool_scale)


def _merge_kernel(x_ref, oa_ref, ob_ref, oc_ref, od_ref, gates_ref, wb_ref, wo_ref, o_ref):
    d = x_ref.shape[1]
    merged = jnp.zeros(x_ref.shape, F32)
    for bi, br in enumerate((oa_ref, ob_ref, oc_ref, od_ref)):
        gate = gates_ref[:, bi * d:(bi + 1) * d].astype(F32)
        merged = merged + (1.0 / (1.0 + jnp.exp(-gate))) * _dot(br[...], wb_ref[bi])
    o_ref[...] = x_ref[...] + _dot(merged.astype(BF16), wo_ref[...])


def merge_branches(x, branches, pb, w_branch, w_out):
    m, d = x.shape
    tm = min(m, 512)
    row = lambda i: (i, 0)
    return pl.pallas_call(
        _merge_kernel,
        out_shape=jax.ShapeDtypeStruct((m, d), F32),
        grid=(m // tm,),
        in_specs=[pl.BlockSpec((tm, d), row)]
        + [pl.BlockSpec((tm, BRANCH_WIDTH), row)] * N_BRANCH
        + [pl.BlockSpec((tm, N_BRANCH * d), lambda i: (i, PB_GATES // (N_BRANCH * d))),
           pl.BlockSpec((N_BRANCH, BRANCH_WIDTH, d), lambda i: (0, 0, 0)),
           pl.BlockSpec((d, d), lambda i: (0, 0))],
        out_specs=pl.BlockSpec((tm, d), row),
        compiler_params=_cparams(("parallel",)),
        name="merge_branches",
    )(x, *branches, pb, w_branch, w_out)


def _cross_kernel(x_ref, g_ref, wq_ref, k_ref, v_ref, wo_ref, o_ref):
    x = x_ref[...]
    d = x.shape[1]
    hd = d // X_HEADS
    q = _dot(_rms(x, g_ref[...]).astype(BF16), wq_ref[...]).astype(BF16)
    outs = []
    for h in range(X_HEADS):
        cols = slice(h * hd, (h + 1) * hd)
        s = _dot_nt(q[:, cols], k_ref[:, cols])
        p = jnp.exp(s - jnp.max(s, axis=1, keepdims=True))
        o = _dot(p.astype(BF16), v_ref[:, cols]) / jnp.sum(p, axis=1, keepdims=True)
        outs.append(o.astype(BF16))
    o_ref[...] = x + _dot(jnp.concatenate(outs, axis=1), wo_ref[...])


def cross_attention(x, g, wq, kv, wo, batch, seq):
    m, d = x.shape
    mem_len = kv.shape[0] // batch
    tm = min(seq, 512)
    nt = seq // tm
    return pl.pallas_call(
        _cross_kernel,
        out_shape=jax.ShapeDtypeStruct((m, d), F32),
        grid=(batch, nt),
        in_specs=[pl.BlockSpec((tm, d), lambda b, i: (b * nt + i, 0)),
                  pl.BlockSpec((1, d), lambda b, i: (0, 0)),
                  pl.BlockSpec((d, d), lambda b, i: (0, 0)),
                  pl.BlockSpec((mem_len, d), lambda b, i: (b, 0)),
                  pl.BlockSpec((mem_len, d), lambda b, i: (b, 1)),
                  pl.BlockSpec((d, d), lambda b, i: (0, 0))],
        out_specs=pl.BlockSpec((tm, d), lambda b, i: (b * nt + i, 0)),
        compiler_params=_cparams(("parallel", "arbitrary")),
        name="cross_attention",
    )(x, g, wq, kv, kv, wo)


FFN_TF = 512


def _swiglu(h, wg, wu):
    a = _dot(h, wg)
    return a * (1.0 / (1.0 + jnp.exp(-a))) * _dot(h, wu)


def _ffn_kernel(x_ref, g_ref, wg_ref, wu_ref, wd_ref, o_ref, h_ref, acc_ref):
    f = pl.program_id(1)

    @pl.when(f == 0)
    def _():
        x = x_ref[...]
        h_ref[...] = _rms(x, g_ref[...]).astype(BF16)
        acc_ref[...] = x

    acc_ref[...] += _dot(_swiglu(h_ref[...], wg_ref[...], wu_ref[...]).astype(BF16), wd_ref[...])

    @pl.when(f == pl.num_programs(1) - 1)
    def _():
        o_ref[...] = acc_ref[...]


def ffn_dense(x, g, w_gu, w_down):
    m, d = x.shape
    dff = w_down.shape[0]
    tm = min(m, 1024)
    nf = dff // FFN_TF
    return pl.pallas_call(
        _ffn_kernel,
        out_shape=jax.ShapeDtypeStruct((m, d), F32),
        grid=(m // tm, nf),
        in_specs=[pl.BlockSpec((tm, d), lambda i, f: (i, 0)),
                  pl.BlockSpec((1, d), lambda i, f: (0, 0)),
                  pl.BlockSpec((d, FFN_TF), lambda i, f: (0, f)),
                  pl.BlockSpec((d, FFN_TF), lambda i, f: (0, nf + f)),
                  pl.BlockSpec((FFN_TF, d), lambda i, f: (f, 0))],
        out_specs=pl.BlockSpec((tm, d), lambda i, f: (i, 0)),
        scratch_shapes=[pltpu.VMEM((tm, d), BF16), pltpu.VMEM((tm, d), F32)],
        compiler_params=_cparams(("parallel", "arbitrary")),
        name="ffn_dense",
    )(x, g, w_gu, w_gu, w_down)


MOE_TR = 512
MOE_TG = 1024
MOE_ALIGN = 16
MOE_SUB = 128
MOE_KSUB = 256
R_I1, R_I2, R_W1, R_W2, R_LR1, R_LR2 = range(6)


def _route_dispatch_kernel(x_ref, g_ref, rw_ref, xs_hbm, route_ref, starts_ref, seg_ref,
                           buf_ref, sem_ref, cnt_ref, *, n_experts, region):
    i = pl.program_id(0)
    tr = MOE_TR

    @pl.when(i == 0)
    def _():
        for e in range(n_experts):
            cnt_ref[e] = 0

    hn = _rms(x_ref[...], g_ref[...])
    h = hn.astype(BF16)
    h_rem = (hn - h.astype(F32)).astype(BF16)
    rw = rw_ref[...]
    rw_head = rw.astype(BF16)
    rw_rem = (rw - rw_head.astype(F32)).astype(BF16)
    logits = (_dot(h, rw_head) + _dot(h_rem, rw_head)) + _dot(h, rw_rem)
    lane = lax.broadcasted_iota(I32, (1, LANES), 1).astype(F32)
    ninf = jnp.float32(-jnp.inf)
    lg = jnp.where(lane < n_experts, logits, ninf)
    m1 = jnp.max(lg, axis=1, keepdims=True)
    i1 = jnp.min(jnp.where(lg == m1, lane, float(LANES)), axis=1, keepdims=True)
    lg2 = jnp.where(lane == i1, ninf, lg)
    m2 = jnp.max(lg2, axis=1, keepdims=True)
    i2 = jnp.min(jnp.where(lg2 == m2, lane, float(LANES)), axis=1, keepdims=True)
    e2 = jnp.exp(m2 - m1)
    den = 1.0 + e2
    oh1 = lane == i1
    oh2 = lane == i2
    both = jnp.where(jnp.logical_or(oh1, oh2), 1.0, 0.0)
    tri = (lax.broadcasted_iota(I32, (tr, tr), 0) > lax.broadcasted_iota(I32, (tr, tr), 1))
    prefix = _dot(jnp.where(tri, 1.0, 0.0).astype(BF16), both.astype(BF16))
    lr1 = jnp.sum(jnp.where(oh1, prefix, 0.0), axis=1, keepdims=True)
    lr2 = jnp.sum(jnp.where(oh2, prefix, 0.0), axis=1, keepdims=True)
    record = jnp.zeros((tr, LANES), F32)
    for col, val in ((R_I1, i1), (R_I2, i2), (R_W1, 1.0 / den), (R_W2, e2 / den), (R_LR1, lr1), (R_LR2, lr2)):
        record = jnp.where(lane == col, val, record)
    route_ref[...] = record
    counts = jnp.sum(both, axis=0, keepdims=True)

    rec_t = record.T
    i1_t, i2_t = rec_t[R_I1:R_I1 + 1, :], rec_t[R_I2:R_I2 + 1, :]
    lr1_t, lr2_t = rec_t[R_LR1:R_LR1 + 1, :], rec_t[R_LR2:R_LR2 + 1, :]
    lane_i = lax.broadcasted_iota(I32, (1, LANES), 1)
    starts = jnp.zeros((1, LANES), I32)
    segs = jnp.zeros((1, LANES), I32)

    def block_copy(buf_slot, start):
        return pltpu.make_async_copy(buf_ref.at[buf_slot],
                                     xs_hbm.at[pl.ds(pl.multiple_of(start, MOE_ALIGN), tr), :],
                                     sem_ref.at[buf_slot])

    for e in range(n_experts):
        in_e1 = i1_t == float(e)
        lr = jnp.where(in_e1, lr1_t, jnp.where(i2_t == float(e), lr2_t, -1.0))
        seg = ((counts[0, e].astype(I32) + (MOE_ALIGN - 1)) // MOE_ALIGN) * MOE_ALIGN
        buf_slot = e % 2
        if e >= 2:
            block_copy(buf_slot, 0).wait()
        else:
            @pl.when(i > 0)
            def _(buf_slot=buf_slot):
                block_copy(buf_slot, 0).wait()
        for sb in range(tr // MOE_SUB):
            rows = slice(sb * MOE_SUB, (sb + 1) * MOE_SUB)

            @pl.when(sb * MOE_SUB < seg)
            def _(sb=sb, rows=rows, lr=lr, buf_slot=buf_slot):
                slot = (lax.broadcasted_iota(I32, (MOE_SUB, tr), 0) + sb * MOE_SUB).astype(F32)
                perm = jnp.where(slot == lr, 1.0, 0.0).astype(BF16)
                buf_ref[buf_slot, rows, :] = _dot(perm, h).astype(BF16)

            @pl.when(sb * MOE_SUB >= seg)
            def _(rows=rows, buf_slot=buf_slot):
                buf_ref[buf_slot, rows, :] = jnp.zeros((MOE_SUB, h.shape[1]), BF16)
        start = e * region + cnt_ref[e]
        block_copy(buf_slot, start).start()
        starts = jnp.where(lane_i == e, start, starts)
        segs = jnp.where(lane_i == e, seg, segs)
        cnt_ref[e] = cnt_ref[e] + seg

    starts_ref[0] = starts
    seg_ref[0] = segs

    @pl.when(i == pl.num_programs(0) - 1)
    def _():
        block_copy(0, 0).wait()
        block_copy(1, 0).wait()
        buf_ref[0] = jnp.zeros(buf_ref.shape[1:], BF16)
        for e in range(n_experts):
            for k in range(MOE_TG // tr + 1):
                tail = block_copy(0, e * region + jnp.minimum(cnt_ref[e] + k * tr, region - tr))
                tail.start()
                tail.wait()


def moe_route_dispatch(x, g, router_w_padded, n_experts):
    m, d = x.shape
    tr = MOE_TR
    n_tiles = m // tr
    region = m + MOE_TG
    kern = functools.partial(_route_dispatch_kernel, n_experts=n_experts, region=region)
    return pl.pallas_call(
        kern,
        out_shape=(jax.ShapeDtypeStruct((n_experts * region, d), BF16),
                   jax.ShapeDtypeStruct((m, LANES), F32),
                   jax.ShapeDtypeStruct((n_tiles, 1, LANES), I32),
                   jax.ShapeDtypeStruct((n_tiles, 1, LANES), I32)),
        grid=(n_tiles,),
        in_specs=[pl.BlockSpec((tr, d), lambda i: (i, 0)),
                  pl.BlockSpec((1, d), lambda i: (0, 0)),
                  pl.BlockSpec((d, LANES), lambda i: (0, 0))],
        out_specs=(pl.BlockSpec(memory_space=pl.ANY),
                   pl.BlockSpec((tr, LANES), lambda i: (i, 0)),
                   pl.BlockSpec((1, 1, LANES), lambda i: (i, 0, 0)),
                   pl.BlockSpec((1, 1, LANES), lambda i: (i, 0, 0))),
        scratch_shapes=[pltpu.VMEM((2, tr, d), BF16), pltpu.SemaphoreType.DMA((2,)),
                        pltpu.SMEM((n_experts,), I32)],
        compiler_params=_cparams(("arbitrary",)),
        name="moe_route_dispatch",
    )(x, g, router_w_padded)


def _ffn_grouped_kernel(trow_ref, texp_ref, tval_ref, x_ref, wg_ref, wu_ref, wd_ref, o_ref, acc_ref):
    t = pl.program_id(0)
    f = pl.program_id(1)

    @pl.when(jnp.logical_and(tval_ref[t] == TILE_ZERO, f == 0))
    def _():
        o_ref[...] = jnp.zeros(o_ref.shape, o_ref.dtype)

    @pl.when(tval_ref[t] == TILE_COMPUTE)
    def _():
        part = _dot(_swiglu(x_ref[...], wg_ref[0], wu_ref[0]).astype(BF16), wd_ref[0])

        @pl.when(f == 0)
        def _():
            acc_ref[...] = part

        @pl.when(f > 0)
        def _():
            acc_ref[...] += part

        @pl.when(f == pl.num_programs(1) - 1)
        def _():
            o_ref[...] = acc_ref[...].astype(o_ref.dtype)


def moe_grouped_ffn(xs, w_gu, w_down, trow, texp, tval):
    rows, d = xs.shape
    dff = w_down.shape[1]
    nf = dff // FFN_TF
    n_steps = trow.shape[0]
    col = lambda f, tv, t: jnp.where(tv[t] == TILE_COMPUTE, f, nf - 1)
    grid_spec = pltpu.PrefetchScalarGridSpec(
        num_scalar_prefetch=3,
        grid=(n_steps, nf),
        in_specs=[pl.BlockSpec((MOE_TG, d), lambda t, f, tr_, te, tv: (tr_[t], 0)),
                  pl.BlockSpec((1, d, FFN_TF), lambda t, f, tr_, te, tv: (te[t], 0, col(f, tv, t))),
                  pl.BlockSpec((1, d, FFN_TF), lambda t, f, tr_, te, tv: (te[t], 0, nf + col(f, tv, t))),
                  pl.BlockSpec((1, FFN_TF, d), lambda t, f, tr_, te, tv: (te[t], col(f, tv, t), 0))],
        out_specs=pl.BlockSpec((MOE_TG, d), lambda t, f, tr_, te, tv: (tr_[t], 0)),
        scratch_shapes=[pltpu.VMEM((MOE_TG, d), F32)])
    return pl.pallas_call(
        _ffn_grouped_kernel,
        out_shape=jax.ShapeDtypeStruct((rows, d), BF16),
        grid_spec=grid_spec,
        compiler_params=_cparams(("arbitrary", "arbitrary")),
        name="moe_grouped_ffn",
    )(trow, texp, tval, xs, w_gu, w_gu, w_down)


def _moe_combine_kernel(starts_ref, seg_ref, x_ref, route_ref, gain_ref, ys_hbm, o_ref, buf_ref, sem_ref,
                        *, n_experts, final_norm_fused):
    i = pl.program_id(0)
    tr = MOE_TR

    n_sub = tr // MOE_KSUB
    n_tiles = pl.num_programs(0)
    cur = i % 2

    def sub_copy(tile, e, kb):
        start = starts_ref[tile * n_experts + e] + kb * MOE_KSUB
        return pltpu.make_async_copy(ys_hbm.at[pl.ds(pl.multiple_of(start, MOE_ALIGN), MOE_KSUB), :],
                                     buf_ref.at[tile % 2, e, kb * MOE_KSUB:(kb + 1) * MOE_KSUB, :],
                                     sem_ref.at[(tile % 2) * n_experts * n_sub + e * n_sub + kb])

    def start_tile(tile):
        for e in range(n_experts):
            for kb in range(n_sub):
                @pl.when(kb * MOE_KSUB < seg_ref[tile * n_experts + e])
                def _(e=e, kb=kb):
                    sub_copy(tile, e, kb).start()

    @pl.when(i == 0)
    def _():
        start_tile(i)

    @pl.when(i + 1 < n_tiles)
    def _():
        start_tile(i + 1)

    rec = route_ref[...]
    i1, i2 = rec[:, R_I1:R_I1 + 1], rec[:, R_I2:R_I2 + 1]
    w1, w2 = rec[:, R_W1:R_W1 + 1], rec[:, R_W2:R_W2 + 1]
    lr1, lr2 = rec[:, R_LR1:R_LR1 + 1], rec[:, R_LR2:R_LR2 + 1]
    o_ref[...] = x_ref[...]
    for e in range(n_experts):
        in_e1 = i1 == float(e)
        in_e2 = i2 == float(e)
        lr = jnp.where(in_e1, lr1, jnp.where(in_e2, lr2, -1.0))
        w = jnp.where(in_e1, w1, jnp.where(in_e2, w2, 0.0))
        seg = seg_ref[i * n_experts + e]
        for kb in range(n_sub):
            @pl.when(kb * MOE_KSUB < seg)
            def _(kb=kb, lr=lr, w=w, e=e):
                sub_copy(i, e, kb).wait()
                slot = (lax.broadcasted_iota(I32, (tr, MOE_KSUB), 1) + kb * MOE_KSUB).astype(F32)
                pick = jnp.where(slot == lr, 1.0, 0.0).astype(BF16)
                o_ref[...] += w * _dot(pick, buf_ref[cur, e, kb * MOE_KSUB:(kb + 1) * MOE_KSUB, :])
    if final_norm_fused:
        o_ref[...] = _rms(o_ref[...], gain_ref[...])


def moe_combine(x, route, ys, starts_flat, seg_flat, n_experts, final_gain):
    m, d = x.shape
    tr = MOE_TR
    fused = final_gain is not None
    gain = final_gain if fused else jnp.ones((1, d), F32)
    grid_spec = pltpu.PrefetchScalarGridSpec(
        num_scalar_prefetch=2,
        grid=(m // tr,),
        in_specs=[pl.BlockSpec((tr, d), lambda i, s, g: (i, 0)),
                  pl.BlockSpec((tr, LANES), lambda i, s, g: (i, 0)),
                  pl.BlockSpec((1, d), lambda i, s, g: (0, 0)),
                  pl.BlockSpec(memory_space=pl.ANY)],
        out_specs=pl.BlockSpec((tr, d), lambda i, s, g: (i, 0)),
        scratch_shapes=[pltpu.VMEM((2, n_experts, tr, d), BF16),
                        pltpu.SemaphoreType.DMA((2 * n_experts * (tr // MOE_KSUB),))])
    return pl.pallas_call(
        functools.partial(_moe_combine_kernel, n_experts=n_experts, final_norm_fused=fused),
        out_shape=jax.ShapeDtypeStruct((m, d), F32),
        grid_spec=grid_spec,
        compiler_params=_cparams(("arbitrary",)),
        name="moe_combine",
    )(starts_flat, seg_flat, x, route, gain, ys)


TILE_DEAD, TILE_COMPUTE, TILE_ZERO = 0, 1, 2


def _moe_tile_plan(seg, n_experts, region, n_steps):
    tiles_per_region = region // MOE_TG
    rows = jnp.sum(seg, axis=0)
    n_compute = (rows + MOE_TG - 1) // MOE_TG
    n_live = jnp.minimum((rows + MOE_TR + MOE_TG - 1) // MOE_TG, tiles_per_region)
    ends = jnp.cumsum(n_live)
    total = ends[-1]
    step = jnp.arange(n_steps, dtype=I32)
    t = jnp.minimum(step, total - 1)
    texp = jnp.sum((t[:, None] >= ends[None, :]).astype(I32), axis=1)
    within = t - (ends - n_live)[texp]
    trow = texp * tiles_per_region + within
    kind = jnp.where(step < total, jnp.where(within < n_compute[texp], TILE_COMPUTE, TILE_ZERO), TILE_DEAD)
    return trow.astype(I32), texp.astype(I32), kind.astype(I32)


def moe_ffn(x, g, router_w, w_gu, w_down, final_gain=None):
    m, d = x.shape
    n_experts = router_w.shape[-1]
    region = m + MOE_TG
    n_tiles = m // MOE_TR
    rw = jnp.pad(router_w, ((0, 0), (0, LANES - n_experts)))
    xs, route, starts, seg = moe_route_dispatch(x, g, rw, n_experts)
    starts = starts[:, 0, :n_experts]
    seg = seg[:, 0, :n_experts]
    max_rows = TOP_K * m + n_tiles * n_experts * (MOE_ALIGN - 1)
    n_steps = max_rows // MOE_TG + 2 * n_experts
    trow, texp, tval = _moe_tile_plan(seg, n_experts, region, n_steps)
    ys = moe_grouped_ffn(xs, w_gu, w_down, trow, texp, tval)
    return moe_combine(x, route, ys, starts.reshape(-1), seg.reshape(-1), n_experts, final_gain)


def _final_norm_kernel(x_ref, g_ref, o_ref):
    o_ref[...] = _rms(x_ref[...], g_ref[...])


def final_norm(x, g):
    m, d = x.shape
    tm = min(m, 1024)
    return pl.pallas_call(
        _final_norm_kernel,
        out_shape=jax.ShapeDtypeStruct((m, d), F32),
        grid=(m // tm,),
        in_specs=[pl.BlockSpec((tm, d), lambda i: (i, 0)), pl.BlockSpec((1, d), lambda i: (0, 0))],
        out_specs=pl.BlockSpec((tm, d), lambda i: (i, 0)),
        compiler_params=_cparams(("parallel",)),
        name="final_norm",
    )(x, g)


def _rotate_half_cols(w):
    d = w.shape[0]
    w4 = w.reshape(d, -1, 2, HEAD_DIM // 2)
    return jnp.stack([-w4[:, :, 1], w4[:, :, 0]], axis=2).reshape(d, -1)


def _dup_heads(w):
    d = w.shape[0]
    w3 = w.reshape(d, -1, 1, HEAD_DIM)
    return jnp.concatenate([w3, w3], axis=2).reshape(d, -1)


def _split_w_in(w_in):
    sizes = (SWA_HEADS * HEAD_DIM, SWA_KV_HEADS * HEAD_DIM, SWA_KV_HEADS * HEAD_DIM,
             DSA_HEADS * HEAD_DIM, DSA_KV_HEADS * HEAD_DIM, DSA_KV_HEADS * HEAD_DIM,
             IDX_HEADS * IDX_DIM, IDX_DIM, IDX_HEADS,
             RET_HEADS * RET_KEY_DIM, RET_HEADS * RET_KEY_DIM, RET_HEADS * RET_VAL_DIM, RET_HEADS * RET_VAL_DIM,
             len(POOL_WINDOWS) * POOL_GROUP_DIM, N_BRANCH * w_in.shape[0])
    parts, off = [], 0
    for s in sizes:
        parts.append(w_in[:, off:off + s])
        off += s
    return parts


def _layout_w_in(w_in):
    d = w_in.shape[0]
    (a_q, a_k, a_v, b_q, b_k, b_v, i_q, i_k, i_w, r_q, r_k, r_v, r_g, p_u, gates) = _split_w_in(w_in)
    qs = HEAD_DIM ** -0.5
    rope_cols = [a_q * qs, b_q * qs, i_q * (IDX_DIM ** -0.5), r_q, r_k * (RET_KEY_DIM ** -0.5),
                 _dup_heads(a_k), _dup_heads(b_k), _dup_heads(i_k)]
    pad = jnp.zeros((d, PA_COLS - sum(c.shape[1] for c in rope_cols)), w_in.dtype)
    wx = jnp.concatenate(rope_cols + [pad], axis=1).astype(BF16)
    wr = jnp.concatenate([_rotate_half_cols(c) for c in rope_cols] + [pad], axis=1).astype(BF16)
    iw_pad = jnp.pad(i_w * (IDX_HEADS ** -0.5), ((0, 0), (0, LANES - IDX_HEADS)))
    w_plain = jnp.concatenate([gates, r_v, r_g, p_u, _dup_heads(a_v), _dup_heads(b_v), iw_pad], axis=1)
    return wx, wr, w_plain.astype(BF16)


def kernel(x, mem, positions, norm_mix, w_in, attn_sink, w_branch, w_out, pool_w, pool_scale, norm_cross, norm_mem, w_xq, w_xkv, w_xo, norm_ffn, dense_w_gu, dense_w_down, router_w, moe_w_gu, moe_w_down, norm_final):
    batch, seq, d = x.shape
    depth = norm_mix.shape[0]
    m = batch * seq
    mem_len = mem.shape[1]
    xs = x.reshape(m, d)
    mem2 = mem.reshape(batch * mem_len, d)

    inv = 1.0 / (ROPE_THETA ** (jnp.arange(0, HEAD_DIM, 2, dtype=F32) / HEAD_DIM))
    inv128 = jnp.tile(inv, LANES // (HEAD_DIM // 2))[None, :]
    cos, sin = rope_tables(positions.reshape(m, 1), inv128)
    log_gamma = jnp.log1p(-(2.0 ** (-5.0 - jnp.arange(RET_HEADS, dtype=F32))))
    x_scale = (d // X_HEADS) ** -0.5

    for layer in range(depth):
        w_x, w_rot, w_plain = _layout_w_in(w_in[layer])
        pa = norm_matmul_rope(xs, norm_mix[layer][None, :], w_x, w_rot, cos, sin, tm=1024, tn=PA_TN)
        pb = norm_matmul(xs, norm_mix[layer][None, :], w_plain, tm=1024, tn=PB_TN, name="in_proj_plain")
        o_a = swa_attention(pa, pb, attn_sink[layer], batch, seq)
        o_b = dsa_attention(pa, pb, batch, seq)
        o_c = retention(pa, pb, log_gamma, batch, seq)
        o_d = multiscale_pool(pb, pool_w[layer].astype(BF16), pool_scale[layer][None, :], batch, seq)
        xs = merge_branches(xs, (o_a, o_b, o_c, o_d), pb, w_branch[layer].astype(BF16), w_out[layer].astype(BF16))

        kv = norm_matmul(mem2, norm_mem[layer][None, :], w_xkv[layer].astype(BF16),
                         tm=batch * mem_len, tn=512, name="mem_kv_proj")
        xs = cross_attention(xs, norm_cross[layer][None, :], (w_xq[layer] * x_scale).astype(BF16), kv,
                             w_xo[layer].astype(BF16), batch, seq)

        li = layer // 2
        if layer % 2 == 0:
            xs = ffn_dense(xs, norm_ffn[layer][None, :], dense_w_gu[li].astype(BF16), dense_w_down[li].astype(BF16))
        else:
            last = layer == depth - 1
            xs = moe_ffn(xs, norm_ffn[layer][None, :], router_w[li], moe_w_gu[li].astype(BF16),
                         moe_w_down[li].astype(BF16), final_gain=norm_final[None, :] if last else None)

    if depth % 2 == 1:
        xs = final_norm(xs, norm_final[None, :])
    return xs.reshape(batch, seq, d)
```

```python
import functools

import jax
import jax.numpy as jnp
from jax import lax
from jax.experimental import pallas as pl
from jax.experimental.pallas import tpu as pltpu

F32 = jnp.float32
BF16 = jnp.bfloat16
I32 = jnp.int32

EPS = 1e-6
HEAD_DIM = 64
ROPE_THETA = 10000.0
BLOCK = 128
SWA_HEADS, SWA_KV_HEADS = 8, 2
DSA_HEADS, DSA_KV_HEADS = 8, 2
IDX_HEADS, IDX_DIM = 4, 64
DSA_TOPK = 256
RET_HEADS, RET_KEY_DIM, RET_VAL_DIM = 4, 64, 128
RET_CHUNK = 256
POOL_WINDOWS = (2, 4, 8, 16)
POOL_GROUP_DIM = 128
N_BRANCH = 4
BRANCH_WIDTH = 512
X_HEADS = 4
TOP_K = 2

LANES = 128
VMEM_LIMIT = 56 * 1024 * 1024
NEG_BIG = -1e30
LOG2_E = 1.4426950408889634
INT_MIN = -2 ** 31

PA_AQ, PA_BQ, PA_IQ, PA_RQ, PA_RK, PA_AK, PA_BK, PA_IK = 0, 512, 1024, 1280, 1536, 1792, 2048, 2304
PA_COLS, PA_TN = 2560, 1280
PB_GATES, PB_RV, PB_RG, PB_PU, PB_AV, PB_BV, PB_IW = 0, 4096, 4608, 5120, 5632, 5888, 6016
PB_COLS, PB_TN = 6144, 1536


def _cparams(sem):
    return pltpu.CompilerParams(dimension_semantics=sem, vmem_limit_bytes=VMEM_LIMIT)


def _dot(a, b):
    return jnp.dot(a, b, preferred_element_type=F32)


def _dot_nt(a, b):
    return lax.dot_general(a, b, (((1,), (1,)), ((), ())), preferred_element_type=F32)


def _rms(x, g):
    return x * lax.rsqrt(jnp.mean(x * x, axis=-1, keepdims=True) + EPS) * g


def _half_mask(hh):
    lane = lax.broadcasted_iota(I32, (1, LANES), 1)
    return (lane >= HEAD_DIM) if hh else (lane < HEAD_DIM)


def _keep_half(block, hh):
    return jnp.where(_half_mask(hh), block, jnp.zeros_like(block))


def _rope_kernel(pos_ref, inv_ref, cos_ref, sin_ref):
    ang = pos_ref[...].astype(F32) * inv_ref[...]
    cos_ref[...] = jnp.cos(ang)
    sin_ref[...] = jnp.sin(ang)


def rope_tables(pos_col, inv128):
    m = pos_col.shape[0]
    tm = min(m, 2048)
    return pl.pallas_call(
        _rope_kernel,
        out_shape=(jax.ShapeDtypeStruct((m, LANES), F32), jax.ShapeDtypeStruct((m, LANES), F32)),
        grid=(m // tm,),
        in_specs=[pl.BlockSpec((tm, 1), lambda i: (i, 0)), pl.BlockSpec((1, LANES), lambda i: (0, 0))],
        out_specs=(pl.BlockSpec((tm, LANES), lambda i: (i, 0)), pl.BlockSpec((tm, LANES), lambda i: (i, 0))),
        compiler_params=_cparams(("parallel",)),
        name="rope_tables",
    )(pos_col, inv128)


def _norm_matmul_kernel(x_ref, g_ref, w_ref, o_ref, h_ref):
    @pl.when(pl.program_id(1) == 0)
    def _():
        h_ref[...] = _rms(x_ref[...], g_ref[...]).astype(BF16)

    o_ref[...] = _dot(h_ref[...], w_ref[...]).astype(o_ref.dtype)


def norm_matmul(x, g, w, tm, tn, out_dtype=BF16, name="norm_matmul"):
    m, k = x.shape
    n = w.shape[1]
    tm = min(tm, m)
    return pl.pallas_call(
        _norm_matmul_kernel,
        out_shape=jax.ShapeDtypeStruct((m, n), out_dtype),
        grid=(m // tm, n // tn),
        in_specs=[pl.BlockSpec((tm, k), lambda i, j: (i, 0)),
                  pl.BlockSpec((1, k), lambda i, j: (0, 0)),
                  pl.BlockSpec((k, tn), lambda i, j: (0, j))],
        out_specs=pl.BlockSpec((tm, tn), lambda i, j: (i, j)),
        scratch_shapes=[pltpu.VMEM((tm, k), BF16)],
        compiler_params=_cparams(("parallel", "arbitrary")),
        name=name,
    )(x, g, w)


def _norm_matmul_rope_kernel(x_ref, g_ref, wx_ref, wr_ref, cos_ref, sin_ref, o_ref, h_ref):
    @pl.when(pl.program_id(1) == 0)
    def _():
        h_ref[...] = _rms(x_ref[...], g_ref[...]).astype(BF16)

    h = h_ref[...]
    proj = _dot(h, wx_ref[...])
    rot = _dot(h, wr_ref[...])
    cos = cos_ref[...]
    sin = sin_ref[...]
    for c in range(o_ref.shape[1] // LANES):
        cols = slice(c * LANES, (c + 1) * LANES)
        o_ref[:, cols] = (proj[:, cols] * cos + rot[:, cols] * sin).astype(o_ref.dtype)


def norm_matmul_rope(x, g, wx, wr, cos, sin, tm, tn):
    m, k = x.shape
    n = wx.shape[1]
    tm = min(tm, m)
    return pl.pallas_call(
        _norm_matmul_rope_kernel,
        out_shape=jax.ShapeDtypeStruct((m, n), BF16),
        grid=(m // tm, n // tn),
        in_specs=[pl.BlockSpec((tm, k), lambda i, j: (i, 0)),
                  pl.BlockSpec((1, k), lambda i, j: (0, 0)),
                  pl.BlockSpec((k, tn), lambda i, j: (0, j)),
                  pl.BlockSpec((k, tn), lambda i, j: (0, j)),
                  pl.BlockSpec((tm, LANES), lambda i, j: (i, 0)),
                  pl.BlockSpec((tm, LANES), lambda i, j: (i, 0))],
        out_specs=pl.BlockSpec((tm, tn), lambda i, j: (i, j)),
        scratch_shapes=[pltpu.VMEM((tm, k), BF16)],
        compiler_params=_cparams(("parallel", "arbitrary")),
        name="in_proj_rope",
    )(x, g, wx, wr, cos, sin)


def _swa_kernel(sink_ref, q_ref, kc_ref, kp_ref, vc_ref, vp_ref, o_ref):
    i = pl.program_id(1)
    per_group = SWA_HEADS // SWA_KV_HEADS
    q_t = (q_ref[...].astype(F32) * LOG2_E).T.astype(BF16)
    key = lax.broadcasted_iota(I32, (2 * BLOCK, 1), 0)
    rel = lax.broadcasted_iota(I32, (1, BLOCK), 1) + BLOCK - key
    band = jnp.logical_and(jnp.logical_and(rel >= 0, rel < BLOCK), jnp.logical_or(key >= BLOCK, i > 0))
    bias = jnp.where(band, 0.0, -jnp.inf).astype(F32)
    bias = jnp.concatenate([bias] * per_group, axis=1)
    outs = []
    for g in range(SWA_KV_HEADS):
        cols = slice(g * LANES, (g + 1) * LANES)
        k_win = jnp.concatenate([kp_ref[:, cols], kc_ref[:, cols]], axis=0)[:, 0:HEAD_DIM]
        v_t = jnp.concatenate([vp_ref[:, cols], vc_ref[:, cols]], axis=0).astype(F32).T[0:HEAD_DIM].astype(BF16)
        heads = range(g * per_group, (g + 1) * per_group)
        q_g = jnp.concatenate([q_t[h * HEAD_DIM:(h + 1) * HEAD_DIM, :] for h in heads], axis=1)
        sink = jnp.concatenate([jnp.full((1, BLOCK), sink_ref[h] * LOG2_E, F32) for h in heads], axis=1)
        s = _dot(k_win, q_g) + bias
        m = jnp.maximum(jnp.max(s, axis=0, keepdims=True), sink)
        p = jnp.exp2(s - m)
        den = jnp.sum(p, axis=0, keepdims=True) + jnp.exp2(sink - m)
        o_t = _dot(v_t, p.astype(BF16)) / den
        outs += [o_t[:, a * BLOCK:(a + 1) * BLOCK] for a in range(per_group)]
    o_ref[...] = jnp.concatenate(outs, axis=0).T.astype(o_ref.dtype)


def swa_attention(pa, pb, sink, batch, seq):
    nb = seq // BLOCK
    cur = lambda col: (lambda b, i: (b * nb + i, col))
    prev = lambda col: (lambda b, i: (b * nb + jnp.maximum(i - 1, 0), col))
    return pl.pallas_call(
        _swa_kernel,
        out_shape=jax.ShapeDtypeStruct((batch * seq, BRANCH_WIDTH), BF16),
        grid=(batch, nb),
        in_specs=[pl.BlockSpec(memory_space=pltpu.SMEM),
                  pl.BlockSpec((BLOCK, 512), cur(PA_AQ // 512)),
                  pl.BlockSpec((BLOCK, 256), cur(PA_AK // 256)),
                  pl.BlockSpec((BLOCK, 256), prev(PA_AK // 256)),
                  pl.BlockSpec((BLOCK, 256), cur(PB_AV // 256)),
                  pl.BlockSpec((BLOCK, 256), prev(PB_AV // 256))],
        out_specs=pl.BlockSpec((BLOCK, BRANCH_WIDTH), cur(0)),
        compiler_params=_cparams(("parallel", "arbitrary")),
        name="swa_attention",
    )(sink, pa, pa, pa, pb, pb)


DSA_TQ = 256
DSA_ONES = 16


def _dsa_kernel(q_ref, iq_ref, iw_ref, k_ref, ik_ref, v_ref, o_ref,
                skey_ref, vt_ref, qt_ref, cut_ref, m_ref, l_ref, acc_ref, alpha_ref, bias_ref, s_ref, p_ref,
                *, n_sel, seq):
    tq = DSA_TQ
    nq = seq // tq
    i = pl.program_id(1)
    n_chunks = i + 1
    q_pos = i * tq + lax.broadcasted_iota(I32, (1, tq), 1)
    key_off = lax.broadcasted_iota(I32, (tq, 1), 0)

    @pl.when(i == 0)
    def _():
        for jj in range(nq):
            vt = v_ref[jj * tq:(jj + 1) * tq, :].astype(F32).T
            ones = jnp.ones((DSA_ONES, tq), BF16)
            for g in range(DSA_KV_HEADS):
                vt_ref[jj, g, 0:HEAD_DIM, :] = vt[g * HEAD_DIM:(g + 1) * HEAD_DIM].astype(BF16)
                vt_ref[jj, g, HEAD_DIM:, :] = ones

    iq_t = iq_ref[...].astype(F32).T.astype(BF16)
    w_t = iw_ref[...].astype(F32).T

    def chunk_keys(j):
        ikc = ik_ref[pl.ds(pl.multiple_of(j * tq, tq), tq), 0:IDX_DIM]
        sc = jnp.zeros((tq, tq), F32)
        for h in range(IDX_HEADS):
            sc = sc + jnp.maximum(_dot(ikc, iq_t[h * IDX_DIM:(h + 1) * IDX_DIM, :]), 0.0) * w_t[h:h + 1, :]
        sc = jnp.where(sc == 0.0, 0.0, sc)
        bits = lax.bitcast_convert_type(sc, I32)
        return jnp.where(bits < 0, bits ^ jnp.int32(0x7FFFFFFF), bits)

    def score_chunk(j, carry):
        skey_ref[j] = chunk_keys(j)
        return carry

    lax.fori_loop(0, i, score_chunk, 0)
    skey_ref[i] = jnp.where((i * tq + key_off) <= q_pos, chunk_keys(i), jnp.int32(INT_MIN))

    view = (tq // 8, 8, tq)
    pos_in_chunk = lax.broadcasted_iota(I32, view, 0) * 8 + lax.broadcasted_iota(I32, view, 1)

    def all_sublanes(x, op):
        for sh in (4, 2, 1):
            x = op(x, pltpu.roll(x, sh, axis=0))
        return x

    def count_keys(pred):
        def body(j, acc):
            x = pred(skey_ref[j].reshape(view), j)
            part = view[0] // 4
            sums = [jnp.sum(x[a * part:(a + 1) * part], axis=0) for a in range(4)]
            return acc + ((sums[0] + sums[1]) + (sums[2] + sums[3]))
        return all_sublanes(lax.fori_loop(0, n_chunks, body, jnp.zeros((8, tq), F32)), jnp.add)

    one = jnp.float32(1.0)
    zero = jnp.float32(0.0)
    k_f = jnp.float32(n_sel)
    cut_bits = max((seq - 1).bit_length(), 1)

    def bit_step(b, carry):
        res_u, c_ge = carry
        cand_u = res_u | lax.shift_left(jnp.int32(1), 31 - b)
        cand_s = cand_u ^ jnp.int32(INT_MIN)
        cnt = count_keys(lambda kc, j: jnp.where(kc >= cand_s, one, zero))
        take = cnt >= k_f
        return jnp.where(take, cand_u, res_u), jnp.where(take, cnt, c_ge)

    res_u, c_ge = lax.fori_loop(0, 32, bit_step, (jnp.zeros((8, tq), I32), jnp.zeros((8, tq), F32)))
    thr = res_u ^ jnp.int32(INT_MIN)
    short = thr == jnp.int32(INT_MIN)
    cut_ref[...] = jnp.where(short, jnp.int32(-1), jnp.int32(seq))
    unresolved = jnp.where(jnp.logical_and(c_ge != k_f, jnp.logical_not(short)), one, zero)

    @pl.when(jnp.max(unresolved) > 0.0)
    def _():
        need = k_f - count_keys(lambda kc, j: jnp.where(kc > thr, one, zero))

        def cut_step(b, ans):
            cand = ans | lax.shift_left(jnp.int32(1), cut_bits - 1 - b)
            lim = cand - 1

            def pred(kc, j):
                return jnp.where(kc == thr, jnp.where((j * tq + pos_in_chunk) <= lim, one, zero), zero)

            return jnp.where(count_keys(pred) < need, cand, ans)

        ans = lax.fori_loop(0, cut_bits, cut_step, jnp.zeros((8, tq), I32))
        cut_ref[...] = jnp.where(short, jnp.int32(-1), ans)

    qt_ref[...] = (q_ref[...].astype(F32) * LOG2_E).T.astype(BF16)
    m_ref[...] = jnp.full(m_ref.shape, NEG_BIG, F32)
    l_ref[...] = jnp.zeros(l_ref.shape, F32)
    acc_ref[...] = jnp.zeros(acc_ref.shape, F32)
    cut = cut_ref[...]
    neg = jnp.float32(NEG_BIG)
    group = DSA_HEADS // DSA_KV_HEADS

    def attend(j, carry):
        kc = skey_ref[j].reshape(view)
        start = pl.multiple_of(j * tq, tq)
        tie = jnp.where(kc == thr, jnp.where((j * tq + pos_in_chunk) <= cut, zero, neg), neg)
        bias_ref[...] = jnp.where(kc > thr, zero, tie).reshape(tq, tq)
        for h in range(DSA_HEADS):
            g = h // group
            kblk = k_ref[pl.ds(start, tq), g * LANES:g * LANES + HEAD_DIM]
            s = _dot(kblk, qt_ref[h * HEAD_DIM:(h + 1) * HEAD_DIM, :]) + bias_ref[...]
            s_ref[h] = s
            m_old = m_ref[h]
            m_new = jnp.maximum(m_old, jnp.max(s, axis=0, keepdims=True))
            alpha_ref[h] = jnp.exp2(m_old - m_new)
            m_ref[h] = m_new
        for h in range(DSA_HEADS):
            p_ref[h] = jnp.exp2(s_ref[h] - m_ref[h]).astype(BF16)
        for h in range(DSA_HEADS):
            pv = _dot(vt_ref[j, h // group], p_ref[h])
            acc_ref[h] = alpha_ref[h] * acc_ref[h] + pv[0:HEAD_DIM]
            l_ref[h] = alpha_ref[h] * l_ref[h] + pv[HEAD_DIM:HEAD_DIM + 1]
        return carry

    lax.fori_loop(0, n_chunks, attend, 0)

    for p in range(DSA_HEADS // 2):
        o_t = jnp.concatenate([acc_ref[2 * p] / l_ref[2 * p], acc_ref[2 * p + 1] / l_ref[2 * p + 1]], axis=0)
        o_ref[:, p * LANES:(p + 1) * LANES] = o_t.T.astype(o_ref.dtype)


def dsa_attention(pa, pb, batch, seq):
    tq = DSA_TQ
    nq = seq // tq
    n_sel = min(DSA_TOPK, seq // 4)
    qmap = lambda col: (lambda b, i: (b * nq + i, col))
    smap = lambda col: (lambda b, i: (b, col))
    return pl.pallas_call(
        functools.partial(_dsa_kernel, n_sel=n_sel, seq=seq),
        out_shape=jax.ShapeDtypeStruct((batch * seq, BRANCH_WIDTH), BF16),
        grid=(batch, nq),
        in_specs=[pl.BlockSpec((tq, 512), qmap(PA_BQ // 512)),
                  pl.BlockSpec((tq, 256), qmap(PA_IQ // 256)),
                  pl.BlockSpec((tq, LANES), qmap(PB_IW // LANES)),
                  pl.BlockSpec((seq, 256), smap(PA_BK // 256)),
                  pl.BlockSpec((seq, LANES), smap(PA_IK // LANES)),
                  pl.BlockSpec((seq, LANES), smap(PB_BV // LANES))],
        out_specs=pl.BlockSpec((tq, BRANCH_WIDTH), qmap(0)),
        scratch_shapes=[pltpu.VMEM((nq, tq, tq), I32),
                        pltpu.VMEM((nq, DSA_KV_HEADS, HEAD_DIM + DSA_ONES, tq), BF16),
                        pltpu.VMEM((DSA_HEADS * HEAD_DIM, tq), BF16),
                        pltpu.VMEM((8, tq), I32),
                        pltpu.VMEM((DSA_HEADS, 1, tq), F32),
                        pltpu.VMEM((DSA_HEADS, 1, tq), F32),
                        pltpu.VMEM((DSA_HEADS, HEAD_DIM, tq), F32),
                        pltpu.VMEM((DSA_HEADS, 1, tq), F32),
                        pltpu.VMEM((tq, tq), F32),
                        pltpu.VMEM((DSA_HEADS, tq, tq), F32),
                        pltpu.VMEM((DSA_HEADS, tq, tq), BF16)],
        compiler_params=_cparams(("parallel", "arbitrary")),
        name="dsa_attention",
    )(pa, pa, pb, pa, pa, pb)


def _retention_kernel(lg_ref, q_ref, k_ref, v_ref, g_ref, o_ref, state_ref):
    cdim = RET_CHUNK

    @pl.when(pl.program_id(1) == 0)
    def _():
        state_ref[...] = jnp.zeros(state_ref.shape, F32)

    n = lax.broadcasted_iota(I32, (cdim, 1), 0).astype(F32)
    mcol = lax.broadcasted_iota(I32, (1, cdim), 1).astype(F32)
    rel = n - mcol
    for h in range(RET_HEADS):
        lg = lg_ref[h]
        decay = jnp.where(rel >= 0, jnp.exp(lg * jnp.maximum(rel, 0.0)), 0.0)
        q_decay = jnp.exp(lg * (n + 1.0))
        k_decay = jnp.exp(lg * (cdim - 1.0 - n))
        chunk_decay = jnp.exp(jnp.full((1, LANES), lg * cdim, F32))
        blk = slice((h // 2) * LANES, (h // 2 + 1) * LANES)
        qm = _keep_half(q_ref[:, blk], h % 2)
        km = _keep_half(k_ref[:, blk], h % 2)
        v = v_ref[:, h * RET_VAL_DIM:(h + 1) * RET_VAL_DIM]
        state = state_ref[h]
        inner = _dot((_dot_nt(qm, km) * decay).astype(BF16), v)
        cross = _dot(qm, state.astype(BF16)) * q_decay
        kd_t = (km.astype(F32) * k_decay).T.astype(BF16)
        state_ref[h] = state * chunk_decay + _dot(kd_t, v)
        out = inner + cross
        out = out * lax.rsqrt(jnp.mean(out * out, axis=-1, keepdims=True) + EPS)
        gate = g_ref[:, h * RET_VAL_DIM:(h + 1) * RET_VAL_DIM].astype(F32)
        silu = gate * (1.0 / (1.0 + jnp.exp(-gate)))
        o_ref[:, h * RET_VAL_DIM:(h + 1) * RET_VAL_DIM] = (silu * out).astype(o_ref.dtype)


def retention(pa, pb, log_gamma, batch, seq):
    nc = seq // RET_CHUNK
    cmap = lambda col: (lambda b, c: (b * nc + c, col))
    return pl.pallas_call(
        _retention_kernel,
        out_shape=jax.ShapeDtypeStruct((batch * seq, BRANCH_WIDTH), BF16),
        grid=(batch, nc),
        in_specs=[pl.BlockSpec(memory_space=pltpu.SMEM),
                  pl.BlockSpec((RET_CHUNK, 256), cmap(PA_RQ // 256)),
                  pl.BlockSpec((RET_CHUNK, 256), cmap(PA_RK // 256)),
                  pl.BlockSpec((RET_CHUNK, 512), cmap(PB_RV // 512)),
                  pl.BlockSpec((RET_CHUNK, 512), cmap(PB_RG // 512))],
        out_specs=pl.BlockSpec((RET_CHUNK, BRANCH_WIDTH), cmap(0)),
        scratch_shapes=[pltpu.VMEM((RET_HEADS, LANES, RET_VAL_DIM), F32)],
        compiler_params=_cparams(("parallel", "arbitrary")),
        name="retention",
    )(log_gamma, pa, pa, pb, pb)


POOL_HALO = 16


def _pool_kernel(cur_ref, prev_ref, w_ref, scale_ref, o_ref, *, tiles_per_seq):
    tm = cur_ref.shape[0]
    it = pl.program_id(0) % tiles_per_seq
    cur = cur_ref[...].astype(F32)
    prev = jnp.where(it > 0, prev_ref[...].astype(F32), 0.0)
    ext = jnp.concatenate([prev, cur], axis=0)
    t = it * tm + lax.broadcasted_iota(I32, (tm, 1), 0)
    for gi, win in enumerate(POOL_WINDOWS):
        cols = slice(gi * POOL_GROUP_DIM, (gi + 1) * POOL_GROUP_DIM)
        s = ext[:, cols]
        sh = 1
        while sh < win:
            s = s + pltpu.roll(s, sh, axis=0)
            sh *= 2
        cnt = jnp.minimum(t + 1, win).astype(F32)
        pooled = s[POOL_HALO:, :] / cnt
        y = _dot((pooled - cur[:, cols]).astype(BF16), w_ref[gi])
        o_ref[:, cols] = (y * scale_ref[:, cols]).astype(o_ref.dtype)


def multiscale_pool(pb, pool_w, pool_scale, batch, seq):
    m = batch * seq
    tm = min(seq, 512)
    tps = seq // tm
    ratio = tm // POOL_HALO
    return pl.pallas_call(
        functools.partial(_pool_kernel, tiles_per_seq=tps),
        out_shape=jax.ShapeDtypeStruct((m, BRANCH_WIDTH), BF16),
        grid=(m // tm,),
        in_specs=[pl.BlockSpec((tm, 512), lambda i: (i, PB_PU // 512)),
                  pl.BlockSpec((POOL_HALO, 512), lambda i: (jnp.maximum(i * ratio - 1, 0), PB_PU // 512)),
                  pl.BlockSpec((len(POOL_WINDOWS), POOL_GROUP_DIM, POOL_GROUP_DIM), lambda i: (0, 0, 0)),
                  pl.BlockSpec((1, 512), lambda i: (0, 0))],
        out_specs=pl.BlockSpec((tm, BRANCH_WIDTH), lambda i: (i, 0)),
        compiler_params=_cparams(("parallel",)),
        name="multiscale_pool",
    )(pb, pb, pool_w, pool_scale)


def _merge_kernel(x_ref, oa_ref, ob_ref, oc_ref, od_ref, gates_ref, wb_ref, wo_ref, o_ref):
    d = x_ref.shape[1]
    merged = jnp.zeros(x_ref.shape, F32)
    for bi, br in enumerate((oa_ref, ob_ref, oc_ref, od_ref)):
        gate = gates_ref[:, bi * d:(bi + 1) * d].astype(F32)
        merged = merged + (1.0 / (1.0 + jnp.exp(-gate))) * _dot(br[...], wb_ref[bi])
    o_ref[...] = x_ref[...] + _dot(merged.astype(BF16), wo_ref[...])


def merge_branches(x, branches, pb, w_branch, w_out):
    m, d = x.shape
    tm = min(m, 512)
    row = lambda i: (i, 0)
    return pl.pallas_call(
        _merge_kernel,
        out_shape=jax.ShapeDtypeStruct((m, d), F32),
        grid=(m // tm,),
        in_specs=[pl.BlockSpec((tm, d), row)]
        + [pl.BlockSpec((tm, BRANCH_WIDTH), row)] * N_BRANCH
        + [pl.BlockSpec((tm, N_BRANCH * d), lambda i: (i, PB_GATES // (N_BRANCH * d))),
           pl.BlockSpec((N_BRANCH, BRANCH_WIDTH, d), lambda i: (0, 0, 0)),
           pl.BlockSpec((d, d), lambda i: (0, 0))],
        out_specs=pl.BlockSpec((tm, d), row),
        compiler_params=_cparams(("parallel",)),
        name="merge_branches",
    )(x, *branches, pb, w_branch, w_out)


def _cross_kernel(x_ref, g_ref, wq_ref, k_ref, v_ref, wo_ref, o_ref):
    x = x_ref[...]
    d = x.shape[1]
    hd = d // X_HEADS
    q = _dot(_rms(x, g_ref[...]).astype(BF16), wq_ref[...]).astype(BF16)
    outs = []
    for h in range(X_HEADS):
        cols = slice(h * hd, (h + 1) * hd)
        s = _dot_nt(q[:, cols], k_ref[:, cols])
        p = jnp.exp(s - jnp.max(s, axis=1, keepdims=True))
        o = _dot(p.astype(BF16), v_ref[:, cols]) / jnp.sum(p, axis=1, keepdims=True)
        outs.append(o.astype(BF16))
    o_ref[...] = x + _dot(jnp.concatenate(outs, axis=1), wo_ref[...])


def cross_attention(x, g, wq, kv, wo, batch, seq):
    m, d = x.shape
    mem_len = kv.shape[0] // batch
    tm = min(seq, 512)
    nt = seq // tm
    return pl.pallas_call(
        _cross_kernel,
        out_shape=jax.ShapeDtypeStruct((m, d), F32),
        grid=(batch, nt),
        in_specs=[pl.BlockSpec((tm, d), lambda b, i: (b * nt + i, 0)),
                  pl.BlockSpec((1, d), lambda b, i: (0, 0)),
                  pl.BlockSpec((d, d), lambda b, i: (0, 0)),
                  pl.BlockSpec((mem_len, d), lambda b, i: (b, 0)),
                  pl.BlockSpec((mem_len, d), lambda b, i: (b, 1)),
                  pl.BlockSpec((d, d), lambda b, i: (0, 0))],
        out_specs=pl.BlockSpec((tm, d), lambda b, i: (b * nt + i, 0)),
        compiler_params=_cparams(("parallel", "arbitrary")),
        name="cross_attention",
    )(x, g, wq, kv, kv, wo)


FFN_TF = 512


def _swiglu(h, wg, wu):
    a = _dot(h, wg)
    return a * (1.0 / (1.0 + jnp.exp(-a))) * _dot(h, wu)


def _ffn_kernel(x_ref, g_ref, wg_ref, wu_ref, wd_ref, o_ref, h_ref, acc_ref):
    f = pl.program_id(1)

    @pl.when(f == 0)
    def _():
        x = x_ref[...]
        h_ref[...] = _rms(x, g_ref[...]).astype(BF16)
        acc_ref[...] = x

    acc_ref[...] += _dot(_swiglu(h_ref[...], wg_ref[...], wu_ref[...]).astype(BF16), wd_ref[...])

    @pl.when(f == pl.num_programs(1) - 1)
    def _():
        o_ref[...] = acc_ref[...]


def ffn_dense(x, g, w_gu, w_down):
    m, d = x.shape
    dff = w_down.shape[0]
    tm = min(m, 1024)
    nf = dff // FFN_TF
    return pl.pallas_call(
        _ffn_kernel,
        out_shape=jax.ShapeDtypeStruct((m, d), F32),
        grid=(m // tm, nf),
        in_specs=[pl.BlockSpec((tm, d), lambda i, f: (i, 0)),
                  pl.BlockSpec((1, d), lambda i, f: (0, 0)),
                  pl.BlockSpec((d, FFN_TF), lambda i, f: (0, f)),
                  pl.BlockSpec((d, FFN_TF), lambda i, f: (0, nf + f)),
                  pl.BlockSpec((FFN_TF, d), lambda i, f: (f, 0))],
        out_specs=pl.BlockSpec((tm, d), lambda i, f: (i, 0)),
        scratch_shapes=[pltpu.VMEM((tm, d), BF16), pltpu.VMEM((tm, d), F32)],
        compiler_params=_cparams(("parallel", "arbitrary")),
        name="ffn_dense",
    )(x, g, w_gu, w_gu, w_down)


MOE_TR = 512
MOE_TG = 1024
MOE_ALIGN = 16
MOE_SUB = 128
MOE_KSUB = 256
R_I1, R_I2, R_W1, R_W2, R_LR1, R_LR2 = range(6)


def _route_dispatch_kernel(x_ref, g_ref, rw_ref, xs_hbm, route_ref, starts_ref, seg_ref,
                           buf_ref, sem_ref, cnt_ref, *, n_experts, region):
    i = pl.program_id(0)
    tr = MOE_TR

    @pl.when(i == 0)
    def _():
        for e in range(n_experts):
            cnt_ref[e] = 0

    hn = _rms(x_ref[...], g_ref[...])
    h = hn.astype(BF16)
    h_rem = (hn - h.astype(F32)).astype(BF16)
    rw = rw_ref[...]
    rw_head = rw.astype(BF16)
    rw_rem = (rw - rw_head.astype(F32)).astype(BF16)
    logits = (_dot(h, rw_head) + _dot(h_rem, rw_head)) + _dot(h, rw_rem)
    lane = lax.broadcasted_iota(I32, (1, LANES), 1).astype(F32)
    ninf = jnp.float32(-jnp.inf)
    lg = jnp.where(lane < n_experts, logits, ninf)
    m1 = jnp.max(lg, axis=1, keepdims=True)
    i1 = jnp.min(jnp.where(lg == m1, lane, float(LANES)), axis=1, keepdims=True)
    lg2 = jnp.where(lane == i1, ninf, lg)
    m2 = jnp.max(lg2, axis=1, keepdims=True)
    i2 = jnp.min(jnp.where(lg2 == m2, lane, float(LANES)), axis=1, keepdims=True)
    e2 = jnp.exp(m2 - m1)
    den = 1.0 + e2
    oh1 = lane == i1
    oh2 = lane == i2
    both = jnp.where(jnp.logical_or(oh1, oh2), 1.0, 0.0)
    tri = (lax.broadcasted_iota(I32, (tr, tr), 0) > lax.broadcasted_iota(I32, (tr, tr), 1))
    prefix = _dot(jnp.where(tri, 1.0, 0.0).astype(BF16), both.astype(BF16))
    lr1 = jnp.sum(jnp.where(oh1, prefix, 0.0), axis=1, keepdims=True)
    lr2 = jnp.sum(jnp.where(oh2, prefix, 0.0), axis=1, keepdims=True)
    record = jnp.zeros((tr, LANES), F32)
    for col, val in ((R_I1, i1), (R_I2, i2), (R_W1, 1.0 / den), (R_W2, e2 / den), (R_LR1, lr1), (R_LR2, lr2)):
        record = jnp.where(lane == col, val, record)
    route_ref[...] = record
    counts = jnp.sum(both, axis=0, keepdims=True)

    rec_t = record.T
    i1_t, i2_t = rec_t[R_I1:R_I1 + 1, :], rec_t[R_I2:R_I2 + 1, :]
    lr1_t, lr2_t = rec_t[R_LR1:R_LR1 + 1, :], rec_t[R_LR2:R_LR2 + 1, :]
    lane_i = lax.broadcasted_iota(I32, (1, LANES), 1)
    starts = jnp.zeros((1, LANES), I32)
    segs = jnp.zeros((1, LANES), I32)

    def block_copy(buf_slot, start):
        return pltpu.make_async_copy(buf_ref.at[buf_slot],
                                     xs_hbm.at[pl.ds(pl.multiple_of(start, MOE_ALIGN), tr), :],
                                     sem_ref.at[buf_slot])

    for e in range(n_experts):
        in_e1 = i1_t == float(e)
        lr = jnp.where(in_e1, lr1_t, jnp.where(i2_t == float(e), lr2_t, -1.0))
        seg = ((counts[0, e].astype(I32) + (MOE_ALIGN - 1)) // MOE_ALIGN) * MOE_ALIGN
        buf_slot = e % 2
        if e >= 2:
            block_copy(buf_slot, 0).wait()
        else:
            @pl.when(i > 0)
            def _(buf_slot=buf_slot):
                block_copy(buf_slot, 0).wait()
        for sb in range(tr // MOE_SUB):
            rows = slice(sb * MOE_SUB, (sb + 1) * MOE_SUB)

            @pl.when(sb * MOE_SUB < seg)
            def _(sb=sb, rows=rows, lr=lr, buf_slot=buf_slot):
                slot = (lax.broadcasted_iota(I32, (MOE_SUB, tr), 0) + sb * MOE_SUB).astype(F32)
                perm = jnp.where(slot == lr, 1.0, 0.0).astype(BF16)
                buf_ref[buf_slot, rows, :] = _dot(perm, h).astype(BF16)

            @pl.when(sb * MOE_SUB >= seg)
            def _(rows=rows, buf_slot=buf_slot):
                buf_ref[buf_slot, rows, :] = jnp.zeros((MOE_SUB, h.shape[1]), BF16)
        start = e * region + cnt_ref[e]
        block_copy(buf_slot, start).start()
        starts = jnp.where(lane_i == e, start, starts)
        segs = jnp.where(lane_i == e, seg, segs)
        cnt_ref[e] = cnt_ref[e] + seg

    starts_ref[0] = starts
    seg_ref[0] = segs

    @pl.when(i == pl.num_programs(0) - 1)
    def _():
        block_copy(0, 0).wait()
        block_copy(1, 0).wait()
        buf_ref[0] = jnp.zeros(buf_ref.shape[1:], BF16)
        for e in range(n_experts):
            for k in range(MOE_TG // tr + 1):
                tail = block_copy(0, e * region + jnp.minimum(cnt_ref[e] + k * tr, region - tr))
                tail.start()
                tail.wait()


def moe_route_dispatch(x, g, router_w_padded, n_experts):
    m, d = x.shape
    tr = MOE_TR
    n_tiles = m // tr
    region = m + MOE_TG
    kern = functools.partial(_route_dispatch_kernel, n_experts=n_experts, region=region)
    return pl.pallas_call(
        kern,
        out_shape=(jax.ShapeDtypeStruct((n_experts * region, d), BF16),
                   jax.ShapeDtypeStruct((m, LANES), F32),
                   jax.ShapeDtypeStruct((n_tiles, 1, LANES), I32),
                   jax.ShapeDtypeStruct((n_tiles, 1, LANES), I32)),
        grid=(n_tiles,),
        in_specs=[pl.BlockSpec((tr, d), lambda i: (i, 0)),
                  pl.BlockSpec((1, d), lambda i: (0, 0)),
                  pl.BlockSpec((d, LANES), lambda i: (0, 0))],
        out_specs=(pl.BlockSpec(memory_space=pl.ANY),
                   pl.BlockSpec((tr, LANES), lambda i: (i, 0)),
                   pl.BlockSpec((1, 1, LANES), lambda i: (i, 0, 0)),
                   pl.BlockSpec((1, 1, LANES), lambda i: (i, 0, 0))),
        scratch_shapes=[pltpu.VMEM((2, tr, d), BF16), pltpu.SemaphoreType.DMA((2,)),
                        pltpu.SMEM((n_experts,), I32)],
        compiler_params=_cparams(("arbitrary",)),
        name="moe_route_dispatch",
    )(x, g, router_w_padded)


def _ffn_grouped_kernel(trow_ref, texp_ref, tval_ref, x_ref, wg_ref, wu_ref, wd_ref, o_ref, acc_ref):
    t = pl.program_id(0)
    f = pl.program_id(1)

    @pl.when(jnp.logical_and(tval_ref[t] == TILE_ZERO, f == 0))
    def _():
        o_ref[...] = jnp.zeros(o_ref.shape, o_ref.dtype)

    @pl.when(tval_ref[t] == TILE_COMPUTE)
    def _():
        part = _dot(_swiglu(x_ref[...], wg_ref[0], wu_ref[0]).astype(BF16), wd_ref[0])

        @pl.when(f == 0)
        def _():
            acc_ref[...] = part

        @pl.when(f > 0)
        def _():
            acc_ref[...] += part

        @pl.when(f == pl.num_programs(1) - 1)
        def _():
            o_ref[...] = acc_ref[...].astype(o_ref.dtype)


def moe_grouped_ffn(xs, w_gu, w_down, trow, texp, tval):
    rows, d = xs.shape
    dff = w_down.shape[1]
    nf = dff // FFN_TF
    n_steps = trow.shape[0]
    col = lambda f, tv, t: jnp.where(tv[t] == TILE_COMPUTE, f, nf - 1)
    grid_spec = pltpu.PrefetchScalarGridSpec(
        num_scalar_prefetch=3,
        grid=(n_steps, nf),
        in_specs=[pl.BlockSpec((MOE_TG, d), lambda t, f, tr_, te, tv: (tr_[t], 0)),
                  pl.BlockSpec((1, d, FFN_TF), lambda t, f, tr_, te, tv: (te[t], 0, col(f, tv, t))),
                  pl.BlockSpec((1, d, FFN_TF), lambda t, f, tr_, te, tv: (te[t], 0, nf + col(f, tv, t))),
                  pl.BlockSpec((1, FFN_TF, d), lambda t, f, tr_, te, tv: (te[t], col(f, tv, t), 0))],
        out_specs=pl.BlockSpec((MOE_TG, d), lambda t, f, tr_, te, tv: (tr_[t], 0)),
        scratch_shapes=[pltpu.VMEM((MOE_TG, d), F32)])
    return pl.pallas_call(
        _ffn_grouped_kernel,
        out_shape=jax.ShapeDtypeStruct((rows, d), BF16),
        grid_spec=grid_spec,
        compiler_params=_cparams(("arbitrary", "arbitrary")),
        name="moe_grouped_ffn",
    )(trow, texp, tval, xs, w_gu, w_gu, w_down)


def _moe_combine_kernel(starts_ref, seg_ref, x_ref, route_ref, gain_ref, ys_hbm, o_ref, buf_ref, sem_ref,
                        *, n_experts, final_norm_fused):
    i = pl.program_id(0)
    tr = MOE_TR

    n_sub = tr // MOE_KSUB
    n_tiles = pl.num_programs(0)
    cur = i % 2

    def sub_copy(tile, e, kb):
        start = starts_ref[tile * n_experts + e] + kb * MOE_KSUB
        return pltpu.make_async_copy(ys_hbm.at[pl.ds(pl.multiple_of(start, MOE_ALIGN), MOE_KSUB), :],
                                     buf_ref.at[tile % 2, e, kb * MOE_KSUB:(kb + 1) * MOE_KSUB, :],
                                     sem_ref.at[(tile % 2) * n_experts * n_sub + e * n_sub + kb])

    def start_tile(tile):
        for e in range(n_experts):
            for kb in range(n_sub):
                @pl.when(kb * MOE_KSUB < seg_ref[tile * n_experts + e])
                def _(e=e, kb=kb):
                    sub_copy(tile, e, kb).start()

    @pl.when(i == 0)
    def _():
        start_tile(i)

    @pl.when(i + 1 < n_tiles)
    def _():
        start_tile(i + 1)

    rec = route_ref[...]
    i1, i2 = rec[:, R_I1:R_I1 + 1], rec[:, R_I2:R_I2 + 1]
    w1, w2 = rec[:, R_W1:R_W1 + 1], rec[:, R_W2:R_W2 + 1]
    lr1, lr2 = rec[:, R_LR1:R_LR1 + 1], rec[:, R_LR2:R_LR2 + 1]
    o_ref[...] = x_ref[...]
    for e in range(n_experts):
        in_e1 = i1 == float(e)
        in_e2 = i2 == float(e)
        lr = jnp.where(in_e1, lr1, jnp.where(in_e2, lr2, -1.0))
        w = jnp.where(in_e1, w1, jnp.where(in_e2, w2, 0.0))
        seg = seg_ref[i * n_experts + e]
        for kb in range(n_sub):
            @pl.when(kb * MOE_KSUB < seg)
            def _(kb=kb, lr=lr, w=w, e=e):
                sub_copy(i, e, kb).wait()
                slot = (lax.broadcasted_iota(I32, (tr, MOE_KSUB), 1) + kb * MOE_KSUB).astype(F32)
                pick = jnp.where(slot == lr, 1.0, 0.0).astype(BF16)
                o_ref[...] += w * _dot(pick, buf_ref[cur, e, kb * MOE_KSUB:(kb + 1) * MOE_KSUB, :])
    if final_norm_fused:
        o_ref[...] = _rms(o_ref[...], gain_ref[...])


def moe_combine(x, route, ys, starts_flat, seg_flat, n_experts, final_gain):
    m, d = x.shape
    tr = MOE_TR
    fused = final_gain is not None
    gain = final_gain if fused else jnp.ones((1, d), F32)
    grid_spec = pltpu.PrefetchScalarGridSpec(
        num_scalar_prefetch=2,
        grid=(m // tr,),
        in_specs=[pl.BlockSpec((tr, d), lambda i, s, g: (i, 0)),
                  pl.BlockSpec((tr, LANES), lambda i, s, g: (i, 0)),
                  pl.BlockSpec((1, d), lambda i, s, g: (0, 0)),
                  pl.BlockSpec(memory_space=pl.ANY)],
        out_specs=pl.BlockSpec((tr, d), lambda i, s, g: (i, 0)),
        scratch_shapes=[pltpu.VMEM((2, n_experts, tr, d), BF16),
                        pltpu.SemaphoreType.DMA((2 * n_experts * (tr // MOE_KSUB),))])
    return pl.pallas_call(
        functools.partial(_moe_combine_kernel, n_experts=n_experts, final_norm_fused=fused),
        out_shape=jax.ShapeDtypeStruct((m, d), F32),
        grid_spec=grid_spec,
        compiler_params=_cparams(("arbitrary",)),
        name="moe_combine",
    )(starts_flat, seg_flat, x, route, gain, ys)


TILE_DEAD, TILE_COMPUTE, TILE_ZERO = 0, 1, 2


def _moe_tile_plan(seg, n_experts, region, n_steps):
    tiles_per_region = region // MOE_TG
    rows = jnp.sum(seg, axis=0)
    n_compute = (rows + MOE_TG - 1) // MOE_TG
    n_live = jnp.minimum((rows + MOE_TR + MOE_TG - 1) // MOE_TG, tiles_per_region)
    ends = jnp.cumsum(n_live)
    total = ends[-1]
    step = jnp.arange(n_steps, dtype=I32)
    t = jnp.minimum(step, total - 1)
    texp = jnp.sum((t[:, None] >= ends[None, :]).astype(I32), axis=1)
    within = t - (ends - n_live)[texp]
    trow = texp * tiles_per_region + within
    kind = jnp.where(step < total, jnp.where(within < n_compute[texp], TILE_COMPUTE, TILE_ZERO), TILE_DEAD)
    return trow.astype(I32), texp.astype(I32), kind.astype(I32)


def moe_ffn(x, g, router_w, w_gu, w_down, final_gain=None):
    m, d = x.shape
    n_experts = router_w.shape[-1]
    region = m + MOE_TG
    n_tiles = m // MOE_TR
    rw = jnp.pad(router_w, ((0, 0), (0, LANES - n_experts)))
    xs, route, starts, seg = moe_route_dispatch(x, g, rw, n_experts)
    starts = starts[:, 0, :n_experts]
    seg = seg[:, 0, :n_experts]
    max_rows = TOP_K * m + n_tiles * n_experts * (MOE_ALIGN - 1)
    n_steps = max_rows // MOE_TG + 2 * n_experts
    trow, texp, tval = _moe_tile_plan(seg, n_experts, region, n_steps)
    ys = moe_grouped_ffn(xs, w_gu, w_down, trow, texp, tval)
    return moe_combine(x, route, ys, starts.reshape(-1), seg.reshape(-1), n_experts, final_gain)


def _final_norm_kernel(x_ref, g_ref, o_ref):
    o_ref[...] = _rms(x_ref[...], g_ref[...])


def final_norm(x, g):
    m, d = x.shape
    tm = min(m, 1024)
    return pl.pallas_call(
        _final_norm_kernel,
        out_shape=jax.ShapeDtypeStruct((m, d), F32),
        grid=(m // tm,),
        in_specs=[pl.BlockSpec((tm, d), lambda i: (i, 0)), pl.BlockSpec((1, d), lambda i: (0, 0))],
        out_specs=pl.BlockSpec((tm, d), lambda i: (i, 0)),
        compiler_params=_cparams(("parallel",)),
        name="final_norm",
    )(x, g)


def _rotate_half_cols(w):
    d = w.shape[0]
    w4 = w.reshape(d, -1, 2, HEAD_DIM // 2)
    return jnp.stack([-w4[:, :, 1], w4[:, :, 0]], axis=2).reshape(d, -1)


def _dup_heads(w):
    d = w.shape[0]
    w3 = w.reshape(d, -1, 1, HEAD_DIM)
    return jnp.concatenate([w3, w3], axis=2).reshape(d, -1)


def _split_w_in(w_in):
    sizes = (SWA_HEADS * HEAD_DIM, SWA_KV_HEADS * HEAD_DIM, SWA_KV_HEADS * HEAD_DIM,
             DSA_HEADS * HEAD_DIM, DSA_KV_HEADS * HEAD_DIM, DSA_KV_HEADS * HEAD_DIM,
             IDX_HEADS * IDX_DIM, IDX_DIM, IDX_HEADS,
             RET_HEADS * RET_KEY_DIM, RET_HEADS * RET_KEY_DIM, RET_HEADS * RET_VAL_DIM, RET_HEADS * RET_VAL_DIM,
             len(POOL_WINDOWS) * POOL_GROUP_DIM, N_BRANCH * w_in.shape[0])
    parts, off = [], 0
    for s in sizes:
        parts.append(w_in[:, off:off + s])
        off += s
    return parts


def _layout_w_in(w_in):
    d = w_in.shape[0]
    (a_q, a_k, a_v, b_q, b_k, b_v, i_q, i_k, i_w, r_q, r_k, r_v, r_g, p_u, gates) = _split_w_in(w_in)
    qs = HEAD_DIM ** -0.5
    rope_cols = [a_q * qs, b_q * qs, i_q * (IDX_DIM ** -0.5), r_q, r_k * (RET_KEY_DIM ** -0.5),
                 _dup_heads(a_k), _dup_heads(b_k), _dup_heads(i_k)]
    pad = jnp.zeros((d, PA_COLS - sum(c.shape[1] for c in rope_cols)), w_in.dtype)
    wx = jnp.concatenate(rope_cols + [pad], axis=1).astype(BF16)
    wr = jnp.concatenate([_rotate_half_cols(c) for c in rope_cols] + [pad], axis=1).astype(BF16)
    iw_pad = jnp.pad(i_w * (IDX_HEADS ** -0.5), ((0, 0), (0, PB_COLS - PB_IW - IDX_HEADS)))
    w_plain = jnp.concatenate([gates, r_v, r_g, p_u, _dup_heads(a_v), b_v, iw_pad], axis=1)
    return wx, wr, w_plain.astype(BF16)


def kernel(x, mem, positions, norm_mix, w_in, attn_sink, w_branch, w_out, pool_w, pool_scale, norm_cross, norm_mem, w_xq, w_xkv, w_xo, norm_ffn, dense_w_gu, dense_w_down, router_w, moe_w_gu, moe_w_down, norm_final):
    batch, seq, d = x.shape
    depth = norm_mix.shape[0]
    m = batch * seq
    mem_len = mem.shape[1]
    xs = x.reshape(m, d)
    mem2 = mem.reshape(batch * mem_len, d)

    inv = 1.0 / (ROPE_THETA ** (jnp.arange(0, HEAD_DIM, 2, dtype=F32) / HEAD_DIM))
    inv128 = jnp.tile(inv, LANES // (HEAD_DIM // 2))[None, :]
    cos, sin = rope_tables(positions.reshape(m, 1), inv128)
    log_gamma = jnp.log1p(-(2.0 ** (-5.0 - jnp.arange(RET_HEADS, dtype=F32))))
    x_scale = (d // X_HEADS) ** -0.5

    for layer in range(depth):
        w_x, w_rot, w_plain = _layout_w_in(w_in[layer])
        pa = norm_matmul_rope(xs, norm_mix[layer][None, :], w_x, w_rot, cos, sin, tm=1024, tn=PA_TN)
        pb = norm_matmul(xs, norm_mix[layer][None, :], w_plain, tm=1024, tn=PB_TN, name="in_proj_plain")
        o_a = swa_attention(pa, pb, attn_sink[layer], batch, seq)
        o_b = dsa_attention(pa, pb, batch, seq)
        o_c = retention(pa, pb, log_gamma, batch, seq)
        o_d = multiscale_pool(pb, pool_w[layer].astype(BF16), pool_scale[layer][None, :], batch, seq)
        xs = merge_branches(xs, (o_a, o_b, o_c, o_d), pb, w_branch[layer].astype(BF16), w_out[layer].astype(BF16))

        kv = norm_matmul(mem2, norm_mem[layer][None, :], w_xkv[layer].astype(BF16),
                         tm=batch * mem_len, tn=512, name="mem_kv_proj")
        xs = cross_attention(xs, norm_cross[layer][None, :], (w_xq[layer] * x_scale).astype(BF16), kv,
                             w_xo[layer].astype(BF16), batch, seq)

        li = layer // 2
        if layer % 2 == 0:
            xs = ffn_dense(xs, norm_ffn[layer][None, :], dense_w_gu[li].astype(BF16), dense_w_down[li].astype(BF16))
        else:
            last = layer == depth - 1
            xs = moe_ffn(xs, norm_ffn[layer][None, :], router_w[li], moe_w_gu[li].astype(BF16),
                         moe_w_down[li].astype(BF16), final_gain=norm_final[None, :] if last else None)

    if depth % 2 == 1:
        xs = final_norm(xs, norm_final[None, :])
    return xs.reshape(batch, seq, d)
```

```python
import functools

import numpy as np
import jax
import jax.numpy as jnp
from jax import lax
from jax.experimental import pallas as pl
from jax.experimental.pallas import tpu as pltpu

F32 = jnp.float32
BF16 = jnp.bfloat16
I32 = jnp.int32

EPS = 1e-6
HEAD_DIM = 64
ROPE_THETA = 10000.0
BLOCK = 128
SWA_HEADS, SWA_KV_HEADS = 8, 2
DSA_HEADS, DSA_KV_HEADS = 8, 2
IDX_HEADS, IDX_DIM = 4, 64
DSA_TOPK = 256
RET_HEADS, RET_KEY_DIM, RET_VAL_DIM = 4, 64, 128
RET_CHUNK = 256
POOL_WINDOWS = (2, 4, 8, 16)
POOL_GROUP_DIM = 128
N_BRANCH = 4
BRANCH_WIDTH = 512
X_HEADS = 4
TOP_K = 2

LANES = 128
VMEM_LIMIT = 56 * 1024 * 1024
NEG_BIG = -1e30
LOG2_E = 1.4426950408889634
INT_MIN = -2 ** 31

PA_AQ, PA_BQ, PA_IQ, PA_RQ, PA_RK, PA_AK, PA_BK, PA_IK = 0, 512, 1024, 1280, 1536, 1792, 2048, 2304
PA_COLS, PA_TN = 2560, 1280
PB_GATES, PB_RV, PB_RG, PB_PU, PB_AV, PB_BV, PB_IW = 0, 4096, 4608, 5120, 5632, 5888, 6016
PB_COLS, PB_TN = 6144, 1536


def _cparams(sem):
    return pltpu.CompilerParams(dimension_semantics=sem, vmem_limit_bytes=VMEM_LIMIT)


def _dot(a, b):
    return jnp.dot(a, b, preferred_element_type=F32)


def _dot_nt(a, b):
    return lax.dot_general(a, b, (((1,), (1,)), ((), ())), preferred_element_type=F32)


def _rms(x, g):
    return x * lax.rsqrt(jnp.mean(x * x, axis=-1, keepdims=True) + EPS) * g


def _half_mask(hh):
    lane = lax.broadcasted_iota(I32, (1, LANES), 1)
    return (lane >= HEAD_DIM) if hh else (lane < HEAD_DIM)


def _keep_half(block, hh):
    return jnp.where(_half_mask(hh), block, jnp.zeros_like(block))


def _rope_kernel(pos_ref, inv_ref, cos_ref, sin_ref):
    ang = pos_ref[...].astype(F32) * inv_ref[...]
    cos_ref[...] = jnp.cos(ang)
    sin_ref[...] = jnp.sin(ang)


def rope_tables(pos_col, inv128):
    m = pos_col.shape[0]
    tm = min(m, 2048)
    return pl.pallas_call(
        _rope_kernel,
        out_shape=(jax.ShapeDtypeStruct((m, LANES), F32), jax.ShapeDtypeStruct((m, LANES), F32)),
        grid=(m // tm,),
        in_specs=[pl.BlockSpec((tm, 1), lambda i: (i, 0)), pl.BlockSpec((1, LANES), lambda i: (0, 0))],
        out_specs=(pl.BlockSpec((tm, LANES), lambda i: (i, 0)), pl.BlockSpec((tm, LANES), lambda i: (i, 0))),
        compiler_params=_cparams(("parallel",)),
        name="rope_tables",
    )(pos_col, inv128)


def _norm_matmul_kernel(x_ref, g_ref, w_ref, o_ref, h_ref):
    @pl.when(pl.program_id(1) == 0)
    def _():
        h_ref[...] = _rms(x_ref[...], g_ref[...]).astype(BF16)

    o_ref[...] = _dot(h_ref[...], w_ref[...]).astype(o_ref.dtype)


def norm_matmul(x, g, w, tm, tn, out_dtype=BF16, name="norm_matmul"):
    m, k = x.shape
    n = w.shape[1]
    tm = min(tm, m)
    return pl.pallas_call(
        _norm_matmul_kernel,
        out_shape=jax.ShapeDtypeStruct((m, n), out_dtype),
        grid=(m // tm, n // tn),
        in_specs=[pl.BlockSpec((tm, k), lambda i, j: (i, 0)),
                  pl.BlockSpec((1, k), lambda i, j: (0, 0)),
                  pl.BlockSpec((k, tn), lambda i, j: (0, j))],
        out_specs=pl.BlockSpec((tm, tn), lambda i, j: (i, j)),
        scratch_shapes=[pltpu.VMEM((tm, k), BF16)],
        compiler_params=_cparams(("parallel", "arbitrary")),
        name=name,
    )(x, g, w)


def _norm_matmul_rope_kernel(x_ref, g_ref, wx_ref, wr_ref, cos_ref, sin_ref, o_ref, h_ref):
    @pl.when(pl.program_id(1) == 0)
    def _():
        h_ref[...] = _rms(x_ref[...], g_ref[...]).astype(BF16)

    h = h_ref[...]
    proj = _dot(h, wx_ref[...])
    rot = _dot(h, wr_ref[...])
    cos = cos_ref[...]
    sin = sin_ref[...]
    for c in range(o_ref.shape[1] // LANES):
        cols = slice(c * LANES, (c + 1) * LANES)
        o_ref[:, cols] = (proj[:, cols] * cos + rot[:, cols] * sin).astype(o_ref.dtype)


def norm_matmul_rope(x, g, wx, wr, cos, sin, tm, tn):
    m, k = x.shape
    n = wx.shape[1]
    tm = min(tm, m)
    return pl.pallas_call(
        _norm_matmul_rope_kernel,
        out_shape=jax.ShapeDtypeStruct((m, n), BF16),
        grid=(m // tm, n // tn),
        in_specs=[pl.BlockSpec((tm, k), lambda i, j: (i, 0)),
                  pl.BlockSpec((1, k), lambda i, j: (0, 0)),
                  pl.BlockSpec((k, tn), lambda i, j: (0, j)),
                  pl.BlockSpec((k, tn), lambda i, j: (0, j)),
                  pl.BlockSpec((tm, LANES), lambda i, j: (i, 0)),
                  pl.BlockSpec((tm, LANES), lambda i, j: (i, 0))],
        out_specs=pl.BlockSpec((tm, tn), lambda i, j: (i, j)),
        scratch_shapes=[pltpu.VMEM((tm, k), BF16)],
        compiler_params=_cparams(("parallel", "arbitrary")),
        name="in_proj_rope",
    )(x, g, wx, wr, cos, sin)


def _swa_kernel(sink_ref, q_ref, kc_ref, kp_ref, vc_ref, vp_ref, o_ref):
    i = pl.program_id(1)
    per_group = SWA_HEADS // SWA_KV_HEADS
    q_t = (q_ref[...].astype(F32) * LOG2_E).T.astype(BF16)
    key = lax.broadcasted_iota(I32, (2 * BLOCK, 1), 0)
    rel = lax.broadcasted_iota(I32, (1, BLOCK), 1) + BLOCK - key
    band = jnp.logical_and(jnp.logical_and(rel >= 0, rel < BLOCK), jnp.logical_or(key >= BLOCK, i > 0))
    bias = jnp.where(band, 0.0, -jnp.inf).astype(F32)
    bias = jnp.concatenate([bias] * per_group, axis=1)
    outs = []
    for g in range(SWA_KV_HEADS):
        cols = slice(g * LANES, (g + 1) * LANES)
        k_win = jnp.concatenate([kp_ref[:, cols], kc_ref[:, cols]], axis=0)[:, 0:HEAD_DIM]
        v_t = jnp.concatenate([vp_ref[:, cols], vc_ref[:, cols]], axis=0).astype(F32).T[0:HEAD_DIM].astype(BF16)
        heads = range(g * per_group, (g + 1) * per_group)
        q_g = jnp.concatenate([q_t[h * HEAD_DIM:(h + 1) * HEAD_DIM, :] for h in heads], axis=1)
        sink = jnp.concatenate([jnp.full((1, BLOCK), sink_ref[h] * LOG2_E, F32) for h in heads], axis=1)
        s = _dot(k_win, q_g) + bias
        m = jnp.maximum(jnp.max(s, axis=0, keepdims=True), sink)
        p = jnp.exp2(s - m)
        den = jnp.sum(p, axis=0, keepdims=True) + jnp.exp2(sink - m)
        o_t = _dot(v_t, p.astype(BF16)) / den
        outs += [o_t[:, a * BLOCK:(a + 1) * BLOCK] for a in range(per_group)]
    o_ref[...] = jnp.concatenate(outs, axis=0).T.astype(o_ref.dtype)


def swa_attention(pa, pb, sink, batch, seq):
    nb = seq // BLOCK
    cur = lambda col: (lambda b, i: (b * nb + i, col))
    prev = lambda col: (lambda b, i: (b * nb + jnp.maximum(i - 1, 0), col))
    return pl.pallas_call(
        _swa_kernel,
        out_shape=jax.ShapeDtypeStruct((batch * seq, BRANCH_WIDTH), BF16),
        grid=(batch, nb),
        in_specs=[pl.BlockSpec(memory_space=pltpu.SMEM),
                  pl.BlockSpec((BLOCK, 512), cur(PA_AQ // 512)),
                  pl.BlockSpec((BLOCK, 256), cur(PA_AK // 256)),
                  pl.BlockSpec((BLOCK, 256), prev(PA_AK // 256)),
                  pl.BlockSpec((BLOCK, 256), cur(PB_AV // 256)),
                  pl.BlockSpec((BLOCK, 256), prev(PB_AV // 256))],
        out_specs=pl.BlockSpec((BLOCK, BRANCH_WIDTH), cur(0)),
        compiler_params=_cparams(("parallel", "arbitrary")),
        name="swa_attention",
    )(sink, pa, pa, pa, pb, pb)


DSA_TQ = 256
DSA_ONES = 16


def _dsa_kernel(q_ref, iq_ref, iw_ref, k_ref, ik_ref, v_ref, o_ref,
                skey_ref, vt_ref, qt_ref, cut_ref, m_ref, l_ref, acc_ref, alpha_ref, bias_ref, s_ref, p_ref,
                *, n_sel, seq):
    tq = DSA_TQ
    nq = seq // tq
    i = pl.program_id(1)
    n_chunks = i + 1
    q_pos = i * tq + lax.broadcasted_iota(I32, (1, tq), 1)
    key_off = lax.broadcasted_iota(I32, (tq, 1), 0)

    @pl.when(i == 0)
    def _():
        for jj in range(nq):
            vt = v_ref[jj * tq:(jj + 1) * tq, :].astype(F32).T
            ones = jnp.ones((DSA_ONES, tq), BF16)
            for g in range(DSA_KV_HEADS):
                vt_ref[jj, g, 0:HEAD_DIM, :] = vt[g * HEAD_DIM:(g + 1) * HEAD_DIM].astype(BF16)
                vt_ref[jj, g, HEAD_DIM:, :] = ones

    iq_t = iq_ref[...].astype(F32).T.astype(BF16)
    w_t = iw_ref[...].astype(F32).T

    def chunk_keys(j):
        ikc = ik_ref[pl.ds(pl.multiple_of(j * tq, tq), tq), 0:IDX_DIM]
        sc = jnp.zeros((tq, tq), F32)
        for h in range(IDX_HEADS):
            sc = sc + jnp.maximum(_dot(ikc, iq_t[h * IDX_DIM:(h + 1) * IDX_DIM, :]), 0.0) * w_t[h:h + 1, :]
        sc = jnp.where(sc == 0.0, 0.0, sc)
        bits = lax.bitcast_convert_type(sc, I32)
        return jnp.where(bits < 0, bits ^ jnp.int32(0x7FFFFFFF), bits)

    def score_chunk(j, carry):
        skey_ref[j] = chunk_keys(j)
        return carry

    lax.fori_loop(0, i, score_chunk, 0)
    skey_ref[i] = jnp.where((i * tq + key_off) <= q_pos, chunk_keys(i), jnp.int32(INT_MIN))

    view = (tq // 8, 8, tq)
    pos_in_chunk = lax.broadcasted_iota(I32, view, 0) * 8 + lax.broadcasted_iota(I32, view, 1)

    def all_sublanes(x, op):
        for sh in (4, 2, 1):
            x = op(x, pltpu.roll(x, sh, axis=0))
        return x

    def count_keys(pred):
        def body(j, acc):
            x = pred(skey_ref[j].reshape(view), j)
            part = view[0] // 4
            sums = [jnp.sum(x[a * part:(a + 1) * part], axis=0) for a in range(4)]
            return acc + ((sums[0] + sums[1]) + (sums[2] + sums[3]))
        return all_sublanes(lax.fori_loop(0, n_chunks, body, jnp.zeros((8, tq), F32)), jnp.add)

    one = jnp.float32(1.0)
    zero = jnp.float32(0.0)
    k_f = jnp.float32(n_sel)
    cut_bits = max((seq - 1).bit_length(), 1)

    def bit_step(b, carry):
        res_u, c_ge = carry
        cand_u = res_u | lax.shift_left(jnp.int32(1), 31 - b)
        cand_s = cand_u ^ jnp.int32(INT_MIN)
        cnt = count_keys(lambda kc, j: jnp.where(kc >= cand_s, one, zero))
        take = cnt >= k_f
        return jnp.where(take, cand_u, res_u), jnp.where(take, cnt, c_ge)

    res_u, c_ge = lax.fori_loop(0, 32, bit_step, (jnp.zeros((8, tq), I32), jnp.zeros((8, tq), F32)))
    thr = res_u ^ jnp.int32(INT_MIN)
    short = thr == jnp.int32(INT_MIN)
    cut_ref[...] = jnp.where(short, jnp.int32(-1), jnp.int32(seq))
    unresolved = jnp.where(jnp.logical_and(c_ge != k_f, jnp.logical_not(short)), one, zero)

    @pl.when(jnp.max(unresolved) > 0.0)
    def _():
        need = k_f - count_keys(lambda kc, j: jnp.where(kc > thr, one, zero))

        def cut_step(b, ans):
            cand = ans | lax.shift_left(jnp.int32(1), cut_bits - 1 - b)
            lim = cand - 1

            def pred(kc, j):
                return jnp.where(kc == thr, jnp.where((j * tq + pos_in_chunk) <= lim, one, zero), zero)

            return jnp.where(count_keys(pred) < need, cand, ans)

        ans = lax.fori_loop(0, cut_bits, cut_step, jnp.zeros((8, tq), I32))
        cut_ref[...] = jnp.where(short, jnp.int32(-1), ans)

    qt_ref[...] = (q_ref[...].astype(F32) * LOG2_E).T.astype(BF16)
    m_ref[...] = jnp.full(m_ref.shape, NEG_BIG, F32)
    l_ref[...] = jnp.zeros(l_ref.shape, F32)
    acc_ref[...] = jnp.zeros(acc_ref.shape, F32)
    cut = cut_ref[...]
    neg = jnp.float32(NEG_BIG)
    group = DSA_HEADS // DSA_KV_HEADS

    def attend(j, carry):
        kc = skey_ref[j].reshape(view)
        start = pl.multiple_of(j * tq, tq)
        tie = jnp.where(kc == thr, jnp.where((j * tq + pos_in_chunk) <= cut, zero, neg), neg)
        bias_ref[...] = jnp.where(kc > thr, zero, tie).reshape(tq, tq)
        for h in range(DSA_HEADS):
            g = h // group
            kblk = k_ref[pl.ds(start, tq), g * LANES:g * LANES + HEAD_DIM]
            s = _dot(kblk, qt_ref[h * HEAD_DIM:(h + 1) * HEAD_DIM, :]) + bias_ref[...]
            s_ref[h] = s
            m_old = m_ref[h]
            m_new = jnp.maximum(m_old, jnp.max(s, axis=0, keepdims=True))
            alpha_ref[h] = jnp.exp2(m_old - m_new)
            m_ref[h] = m_new
        for h in range(DSA_HEADS):
            p_ref[h] = jnp.exp2(s_ref[h] - m_ref[h]).astype(BF16)
        for h in range(DSA_HEADS):
            pv = _dot(vt_ref[j, h // group], p_ref[h])
            acc_ref[h] = alpha_ref[h] * acc_ref[h] + pv[0:HEAD_DIM]
            l_ref[h] = alpha_ref[h] * l_ref[h] + pv[HEAD_DIM:HEAD_DIM + 1]
        return carry

    lax.fori_loop(0, n_chunks, attend, 0)

    for p in range(DSA_HEADS // 2):
        o_t = jnp.concatenate([acc_ref[2 * p] / l_ref[2 * p], acc_ref[2 * p + 1] / l_ref[2 * p + 1]], axis=0)
        o_ref[:, p * LANES:(p + 1) * LANES] = o_t.T.astype(o_ref.dtype)


def dsa_attention(pa, pb, batch, seq):
    tq = DSA_TQ
    nq = seq // tq
    n_sel = min(DSA_TOPK, seq // 4)
    qmap = lambda col: (lambda b, i: (b * nq + i, col))
    smap = lambda col: (lambda b, i: (b, col))
    return pl.pallas_call(
        functools.partial(_dsa_kernel, n_sel=n_sel, seq=seq),
        out_shape=jax.ShapeDtypeStruct((batch * seq, BRANCH_WIDTH), BF16),
        grid=(batch, nq),
        in_specs=[pl.BlockSpec((tq, 512), qmap(PA_BQ // 512)),
                  pl.BlockSpec((tq, 256), qmap(PA_IQ // 256)),
                  pl.BlockSpec((tq, LANES), qmap(PB_IW // LANES)),
                  pl.BlockSpec((seq, 256), smap(PA_BK // 256)),
                  pl.BlockSpec((seq, LANES), smap(PA_IK // LANES)),
                  pl.BlockSpec((seq, LANES), smap(PB_BV // LANES))],
        out_specs=pl.BlockSpec((tq, BRANCH_WIDTH), qmap(0)),
        scratch_shapes=[pltpu.VMEM((nq, tq, tq), I32),
                        pltpu.VMEM((nq, DSA_KV_HEADS, HEAD_DIM + DSA_ONES, tq), BF16),
                        pltpu.VMEM((DSA_HEADS * HEAD_DIM, tq), BF16),
                        pltpu.VMEM((8, tq), I32),
                        pltpu.VMEM((DSA_HEADS, 1, tq), F32),
                        pltpu.VMEM((DSA_HEADS, 1, tq), F32),
                        pltpu.VMEM((DSA_HEADS, HEAD_DIM, tq), F32),
                        pltpu.VMEM((DSA_HEADS, 1, tq), F32),
                        pltpu.VMEM((tq, tq), F32),
                        pltpu.VMEM((DSA_HEADS, tq, tq), F32),
                        pltpu.VMEM((DSA_HEADS, tq, tq), BF16)],
        compiler_params=_cparams(("parallel", "arbitrary")),
        name="dsa_attention",
    )(pa, pa, pb, pa, pa, pb)


def _retention_kernel(lg_ref, q_ref, k_ref, v_ref, g_ref, o_ref, state_ref):
    cdim = RET_CHUNK

    @pl.when(pl.program_id(1) == 0)
    def _():
        state_ref[...] = jnp.zeros(state_ref.shape, F32)

    n = lax.broadcasted_iota(I32, (cdim, 1), 0).astype(F32)
    mcol = lax.broadcasted_iota(I32, (1, cdim), 1).astype(F32)
    rel = n - mcol
    for h in range(RET_HEADS):
        lg = lg_ref[h]
        decay = jnp.where(rel >= 0, jnp.exp(lg * jnp.maximum(rel, 0.0)), 0.0)
        q_decay = jnp.exp(lg * (n + 1.0))
        k_decay = jnp.exp(lg * (cdim - 1.0 - n))
        chunk_decay = jnp.exp(jnp.full((1, LANES), lg * cdim, F32))
        blk = slice((h // 2) * LANES, (h // 2 + 1) * LANES)
        qm = _keep_half(q_ref[:, blk], h % 2)
        km = _keep_half(k_ref[:, blk], h % 2)
        v = v_ref[:, h * RET_VAL_DIM:(h + 1) * RET_VAL_DIM]
        state = state_ref[h]
        inner = _dot((_dot_nt(qm, km) * decay).astype(BF16), v)
        cross = _dot(qm, state.astype(BF16)) * q_decay
        kd_t = (km.astype(F32) * k_decay).T.astype(BF16)
        state_ref[h] = state * chunk_decay + _dot(kd_t, v)
        out = inner + cross
        out = out * lax.rsqrt(jnp.mean(out * out, axis=-1, keepdims=True) + EPS)
        gate = g_ref[:, h * RET_VAL_DIM:(h + 1) * RET_VAL_DIM].astype(F32)
        silu = gate * (1.0 / (1.0 + jnp.exp(-gate)))
        o_ref[:, h * RET_VAL_DIM:(h + 1) * RET_VAL_DIM] = (silu * out).astype(o_ref.dtype)


def retention(pa, pb, log_gamma, batch, seq):
    nc = seq // RET_CHUNK
    cmap = lambda col: (lambda b, c: (b * nc + c, col))
    return pl.pallas_call(
        _retention_kernel,
        out_shape=jax.ShapeDtypeStruct((batch * seq, BRANCH_WIDTH), BF16),
        grid=(batch, nc),
        in_specs=[pl.BlockSpec(memory_space=pltpu.SMEM),
                  pl.BlockSpec((RET_CHUNK, 256), cmap(PA_RQ // 256)),
                  pl.BlockSpec((RET_CHUNK, 256), cmap(PA_RK // 256)),
                  pl.BlockSpec((RET_CHUNK, 512), cmap(PB_RV // 512)),
                  pl.BlockSpec((RET_CHUNK, 512), cmap(PB_RG // 512))],
        out_specs=pl.BlockSpec((RET_CHUNK, BRANCH_WIDTH), cmap(0)),
        scratch_shapes=[pltpu.VMEM((RET_HEADS, LANES, RET_VAL_DIM), F32)],
        compiler_params=_cparams(("parallel", "arbitrary")),
        name="retention",
    )(log_gamma, pa, pa, pb, pb)


POOL_HALO = 16


def _pool_kernel(cur_ref, prev_ref, w_ref, scale_ref, o_ref, *, tiles_per_seq):
    tm = cur_ref.shape[0]
    it = pl.program_id(0) % tiles_per_seq
    cur = cur_ref[...].astype(F32)
    prev = jnp.where(it > 0, prev_ref[...].astype(F32), 0.0)
    ext = jnp.concatenate([prev, cur], axis=0)
    t = it * tm + lax.broadcasted_iota(I32, (tm, 1), 0)
    for gi, win in enumerate(POOL_WINDOWS):
        cols = slice(gi * POOL_GROUP_DIM, (gi + 1) * POOL_GROUP_DIM)
        s = ext[:, cols]
        sh = 1
        while sh < win:
            s = s + pltpu.roll(s, sh, axis=0)
            sh *= 2
        cnt = jnp.minimum(t + 1, win).astype(F32)
        pooled = s[POOL_HALO:, :] / cnt
        y = _dot((pooled - cur[:, cols]).astype(BF16), w_ref[gi])
        o_ref[:, cols] = (y * scale_ref[:, cols]).astype(o_ref.dtype)


def multiscale_pool(pb, pool_w, pool_scale, batch, seq):
    m = batch * seq
    tm = min(seq, 512)
    tps = seq // tm
    ratio = tm // POOL_HALO
    return pl.pallas_call(
        functools.partial(_pool_kernel, tiles_per_seq=tps),
        out_shape=jax.ShapeDtypeStruct((m, BRANCH_WIDTH), BF16),
        grid=(m // tm,),
        in_specs=[pl.BlockSpec((tm, 512), lambda i: (i, PB_PU // 512)),
                  pl.BlockSpec((POOL_HALO, 512), lambda i: (jnp.maximum(i * ratio - 1, 0), PB_PU // 512)),
                  pl.BlockSpec((len(POOL_WINDOWS), POOL_GROUP_DIM, POOL_GROUP_DIM), lambda i: (0, 0, 0)),
                  pl.BlockSpec((1, 512), lambda i: (0, 0))],
        out_specs=pl.BlockSpec((tm, BRANCH_WIDTH), lambda i: (i, 0)),
        compiler_params=_cparams(("parallel",)),
        name="multiscale_pool",
    )(pb, pb, pool_w, pool_scale)


def _merge_kernel(x_ref, oa_ref, ob_ref, oc_ref, od_ref, gates_ref, wb_ref, wo_ref, o_ref):
    d = x_ref.shape[1]
    merged = jnp.zeros(x_ref.shape, F32)
    for bi, br in enumerate((oa_ref, ob_ref, oc_ref, od_ref)):
        gate = gates_ref[:, bi * d:(bi + 1) * d].astype(F32)
        merged = merged + (1.0 / (1.0 + jnp.exp(-gate))) * _dot(br[...], wb_ref[bi])
    o_ref[...] = x_ref[...] + _dot(merged.astype(BF16), wo_ref[...])


def merge_branches(x, branches, pb, w_branch, w_out):
    m, d = x.shape
    tm = min(m, 512)
    row = lambda i: (i, 0)
    return pl.pallas_call(
        _merge_kernel,
        out_shape=jax.ShapeDtypeStruct((m, d), F32),
        grid=(m // tm,),
        in_specs=[pl.BlockSpec((tm, d), row)]
        + [pl.BlockSpec((tm, BRANCH_WIDTH), row)] * N_BRANCH
        + [pl.BlockSpec((tm, N_BRANCH * d), lambda i: (i, PB_GATES // (N_BRANCH * d))),
           pl.BlockSpec((N_BRANCH, BRANCH_WIDTH, d), lambda i: (0, 0, 0)),
           pl.BlockSpec((d, d), lambda i: (0, 0))],
        out_specs=pl.BlockSpec((tm, d), row),
        compiler_params=_cparams(("parallel",)),
        name="merge_branches",
    )(x, *branches, pb, w_branch, w_out)


def _cross_kernel(x_ref, g_ref, wq_ref, k_ref, v_ref, wo_ref, o_ref):
    x = x_ref[...]
    d = x.shape[1]
    hd = d // X_HEADS
    q = _dot(_rms(x, g_ref[...]).astype(BF16), wq_ref[...]).astype(BF16)
    outs = []
    for h in range(X_HEADS):
        cols = slice(h * hd, (h + 1) * hd)
        s = _dot_nt(q[:, cols], k_ref[:, cols])
        p = jnp.exp(s - jnp.max(s, axis=1, keepdims=True))
        o = _dot(p.astype(BF16), v_ref[:, cols]) / jnp.sum(p, axis=1, keepdims=True)
        outs.append(o.astype(BF16))
    o_ref[...] = x + _dot(jnp.concatenate(outs, axis=1), wo_ref[...])


def cross_attention(x, g, wq, kv, wo, batch, seq):
    m, d = x.shape
    mem_len = kv.shape[0] // batch
    tm = min(seq, 512)
    nt = seq // tm
    return pl.pallas_call(
        _cross_kernel,
        out_shape=jax.ShapeDtypeStruct((m, d), F32),
        grid=(batch, nt),
        in_specs=[pl.BlockSpec((tm, d), lambda b, i: (b * nt + i, 0)),
                  pl.BlockSpec((1, d), lambda b, i: (0, 0)),
                  pl.BlockSpec((d, d), lambda b, i: (0, 0)),
                  pl.BlockSpec((mem_len, d), lambda b, i: (b, 0)),
                  pl.BlockSpec((mem_len, d), lambda b, i: (b, 1)),
                  pl.BlockSpec((d, d), lambda b, i: (0, 0))],
        out_specs=pl.BlockSpec((tm, d), lambda b, i: (b * nt + i, 0)),
        compiler_params=_cparams(("parallel", "arbitrary")),
        name="cross_attention",
    )(x, g, wq, kv, kv, wo)


FFN_TF = 512


def _swiglu(h, wg, wu):
    a = _dot(h, wg)
    return a * (1.0 / (1.0 + jnp.exp(-a))) * _dot(h, wu)


def _ffn_kernel(x_ref, g_ref, wg_ref, wu_ref, wd_ref, o_ref, h_ref, acc_ref):
    f = pl.program_id(1)

    @pl.when(f == 0)
    def _():
        x = x_ref[...]
        h_ref[...] = _rms(x, g_ref[...]).astype(BF16)
        acc_ref[...] = x

    acc_ref[...] += _dot(_swiglu(h_ref[...], wg_ref[...], wu_ref[...]).astype(BF16), wd_ref[...])

    @pl.when(f == pl.num_programs(1) - 1)
    def _():
        o_ref[...] = acc_ref[...]


def ffn_dense(x, g, w_gu, w_down):
    m, d = x.shape
    dff = w_down.shape[0]
    tm = min(m, 1024)
    nf = dff // FFN_TF
    return pl.pallas_call(
        _ffn_kernel,
        out_shape=jax.ShapeDtypeStruct((m, d), F32),
        grid=(m // tm, nf),
        in_specs=[pl.BlockSpec((tm, d), lambda i, f: (i, 0)),
                  pl.BlockSpec((1, d), lambda i, f: (0, 0)),
                  pl.BlockSpec((d, FFN_TF), lambda i, f: (0, f)),
                  pl.BlockSpec((d, FFN_TF), lambda i, f: (0, nf + f)),
                  pl.BlockSpec((FFN_TF, d), lambda i, f: (f, 0))],
        out_specs=pl.BlockSpec((tm, d), lambda i, f: (i, 0)),
        scratch_shapes=[pltpu.VMEM((tm, d), BF16), pltpu.VMEM((tm, d), F32)],
        compiler_params=_cparams(("parallel", "arbitrary")),
        name="ffn_dense",
    )(x, g, w_gu, w_gu, w_down)


MOE_TR = 512
MOE_TG = 1024
MOE_ALIGN = 16
MOE_SUB = 128
MOE_KSUB = 256
R_I1, R_I2, R_W1, R_W2, R_LR1, R_LR2 = range(6)


def _route_dispatch_kernel(x_ref, g_ref, rw_ref, xs_hbm, route_ref, starts_ref, seg_ref,
                           buf_ref, sem_ref, cnt_ref, *, n_experts, region):
    i = pl.program_id(0)
    tr = MOE_TR

    @pl.when(i == 0)
    def _():
        for e in range(n_experts):
            cnt_ref[e] = 0

    hn = _rms(x_ref[...], g_ref[...])
    h = hn.astype(BF16)
    h_rem = (hn - h.astype(F32)).astype(BF16)
    rw = rw_ref[...]
    rw_head = rw.astype(BF16)
    rw_rem = (rw - rw_head.astype(F32)).astype(BF16)
    logits = (_dot(h, rw_head) + _dot(h_rem, rw_head)) + _dot(h, rw_rem)
    lane = lax.broadcasted_iota(I32, (1, LANES), 1).astype(F32)
    ninf = jnp.float32(-jnp.inf)
    lg = jnp.where(lane < n_experts, logits, ninf)
    m1 = jnp.max(lg, axis=1, keepdims=True)
    i1 = jnp.min(jnp.where(lg == m1, lane, float(LANES)), axis=1, keepdims=True)
    lg2 = jnp.where(lane == i1, ninf, lg)
    m2 = jnp.max(lg2, axis=1, keepdims=True)
    i2 = jnp.min(jnp.where(lg2 == m2, lane, float(LANES)), axis=1, keepdims=True)
    e2 = jnp.exp(m2 - m1)
    den = 1.0 + e2
    oh1 = lane == i1
    oh2 = lane == i2
    both = jnp.where(jnp.logical_or(oh1, oh2), 1.0, 0.0)
    tri = (lax.broadcasted_iota(I32, (tr, tr), 0) > lax.broadcasted_iota(I32, (tr, tr), 1))
    prefix = _dot(jnp.where(tri, 1.0, 0.0).astype(BF16), both.astype(BF16))
    lr1 = jnp.sum(jnp.where(oh1, prefix, 0.0), axis=1, keepdims=True)
    lr2 = jnp.sum(jnp.where(oh2, prefix, 0.0), axis=1, keepdims=True)
    record = jnp.zeros((tr, LANES), F32)
    for col, val in ((R_I1, i1), (R_I2, i2), (R_W1, 1.0 / den), (R_W2, e2 / den), (R_LR1, lr1), (R_LR2, lr2)):
        record = jnp.where(lane == col, val, record)
    route_ref[...] = record
    counts = jnp.sum(both, axis=0, keepdims=True)

    rec_t = record.T
    i1_t, i2_t = rec_t[R_I1:R_I1 + 1, :], rec_t[R_I2:R_I2 + 1, :]
    lr1_t, lr2_t = rec_t[R_LR1:R_LR1 + 1, :], rec_t[R_LR2:R_LR2 + 1, :]
    lane_i = lax.broadcasted_iota(I32, (1, LANES), 1)
    starts = jnp.zeros((1, LANES), I32)
    segs = jnp.zeros((1, LANES), I32)

    def block_copy(buf_slot, start):
        return pltpu.make_async_copy(buf_ref.at[buf_slot],
                                     xs_hbm.at[pl.ds(pl.multiple_of(start, MOE_ALIGN), tr), :],
                                     sem_ref.at[buf_slot])

    for e in range(n_experts):
        in_e1 = i1_t == float(e)
        lr = jnp.where(in_e1, lr1_t, jnp.where(i2_t == float(e), lr2_t, -1.0))
        seg = ((counts[0, e].astype(I32) + (MOE_ALIGN - 1)) // MOE_ALIGN) * MOE_ALIGN
        buf_slot = e % 2
        if e >= 2:
            block_copy(buf_slot, 0).wait()
        else:
            @pl.when(i > 0)
            def _(buf_slot=buf_slot):
                block_copy(buf_slot, 0).wait()
        for sb in range(tr // MOE_SUB):
            rows = slice(sb * MOE_SUB, (sb + 1) * MOE_SUB)

            @pl.when(sb * MOE_SUB < seg)
            def _(sb=sb, rows=rows, lr=lr, buf_slot=buf_slot):
                slot = (lax.broadcasted_iota(I32, (MOE_SUB, tr), 0) + sb * MOE_SUB).astype(F32)
                perm = jnp.where(slot == lr, 1.0, 0.0).astype(BF16)
                buf_ref[buf_slot, rows, :] = _dot(perm, h).astype(BF16)

            @pl.when(sb * MOE_SUB >= seg)
            def _(rows=rows, buf_slot=buf_slot):
                buf_ref[buf_slot, rows, :] = jnp.zeros((MOE_SUB, h.shape[1]), BF16)
        start = e * region + cnt_ref[e]
        block_copy(buf_slot, start).start()
        starts = jnp.where(lane_i == e, start, starts)
        segs = jnp.where(lane_i == e, seg, segs)
        cnt_ref[e] = cnt_ref[e] + seg

    starts_ref[0] = starts
    seg_ref[0] = segs

    @pl.when(i == pl.num_programs(0) - 1)
    def _():
        block_copy(0, 0).wait()
        block_copy(1, 0).wait()
        buf_ref[0] = jnp.zeros(buf_ref.shape[1:], BF16)
        for e in range(n_experts):
            for k in range(MOE_TG // tr + 1):
                tail = block_copy(0, e * region + jnp.minimum(cnt_ref[e] + k * tr, region - tr))
                tail.start()
                tail.wait()


def moe_route_dispatch(x, g, router_w_padded, n_experts):
    m, d = x.shape
    tr = MOE_TR
    n_tiles = m // tr
    region = m + MOE_TG
    kern = functools.partial(_route_dispatch_kernel, n_experts=n_experts, region=region)
    return pl.pallas_call(
        kern,
        out_shape=(jax.ShapeDtypeStruct((n_experts * region, d), BF16),
                   jax.ShapeDtypeStruct((m, LANES), F32),
                   jax.ShapeDtypeStruct((n_tiles, 1, LANES), I32),
                   jax.ShapeDtypeStruct((n_tiles, 1, LANES), I32)),
        grid=(n_tiles,),
        in_specs=[pl.BlockSpec((tr, d), lambda i: (i, 0)),
                  pl.BlockSpec((1, d), lambda i: (0, 0)),
                  pl.BlockSpec((d, LANES), lambda i: (0, 0))],
        out_specs=(pl.BlockSpec(memory_space=pl.ANY),
                   pl.BlockSpec((tr, LANES), lambda i: (i, 0)),
                   pl.BlockSpec((1, 1, LANES), lambda i: (i, 0, 0)),
                   pl.BlockSpec((1, 1, LANES), lambda i: (i, 0, 0))),
        scratch_shapes=[pltpu.VMEM((2, tr, d), BF16), pltpu.SemaphoreType.DMA((2,)),
                        pltpu.SMEM((n_experts,), I32)],
        compiler_params=_cparams(("arbitrary",)),
        name="moe_route_dispatch",
    )(x, g, router_w_padded)


def _ffn_grouped_kernel(trow_ref, texp_ref, tval_ref, x_ref, wg_ref, wu_ref, wd_ref, o_ref, acc_ref):
    t = pl.program_id(0)
    f = pl.program_id(1)

    @pl.when(jnp.logical_and(tval_ref[t] == TILE_ZERO, f == 0))
    def _():
        o_ref[...] = jnp.zeros(o_ref.shape, o_ref.dtype)

    @pl.when(tval_ref[t] == TILE_COMPUTE)
    def _():
        part = _dot(_swiglu(x_ref[...], wg_ref[0], wu_ref[0]).astype(BF16), wd_ref[0])

        @pl.when(f == 0)
        def _():
            acc_ref[...] = part

        @pl.when(f > 0)
        def _():
            acc_ref[...] += part

        @pl.when(f == pl.num_programs(1) - 1)
        def _():
            o_ref[...] = acc_ref[...].astype(o_ref.dtype)


def moe_grouped_ffn(xs, w_gu, w_down, trow, texp, tval):
    rows, d = xs.shape
    dff = w_down.shape[1]
    nf = dff // FFN_TF
    n_steps = trow.shape[0]
    col = lambda f, tv, t: jnp.where(tv[t] == TILE_COMPUTE, f, nf - 1)
    grid_spec = pltpu.PrefetchScalarGridSpec(
        num_scalar_prefetch=3,
        grid=(n_steps, nf),
        in_specs=[pl.BlockSpec((MOE_TG, d), lambda t, f, tr_, te, tv: (tr_[t], 0)),
                  pl.BlockSpec((1, d, FFN_TF), lambda t, f, tr_, te, tv: (te[t], 0, col(f, tv, t))),
                  pl.BlockSpec((1, d, FFN_TF), lambda t, f, tr_, te, tv: (te[t], 0, nf + col(f, tv, t))),
                  pl.BlockSpec((1, FFN_TF, d), lambda t, f, tr_, te, tv: (te[t], col(f, tv, t), 0))],
        out_specs=pl.BlockSpec((MOE_TG, d), lambda t, f, tr_, te, tv: (tr_[t], 0)),
        scratch_shapes=[pltpu.VMEM((MOE_TG, d), F32)])
    return pl.pallas_call(
        _ffn_grouped_kernel,
        out_shape=jax.ShapeDtypeStruct((rows, d), BF16),
        grid_spec=grid_spec,
        compiler_params=_cparams(("arbitrary", "arbitrary")),
        name="moe_grouped_ffn",
    )(trow, texp, tval, xs, w_gu, w_gu, w_down)


def _moe_combine_kernel(starts_ref, seg_ref, x_ref, route_ref, gain_ref, ys_hbm, o_ref, buf_ref, sem_ref,
                        *, n_experts, final_norm_fused):
    i = pl.program_id(0)
    tr = MOE_TR

    n_sub = tr // MOE_KSUB
    n_tiles = pl.num_programs(0)
    cur = i % 2

    def sub_copy(tile, e, kb):
        start = starts_ref[tile * n_experts + e] + kb * MOE_KSUB
        return pltpu.make_async_copy(ys_hbm.at[pl.ds(pl.multiple_of(start, MOE_ALIGN), MOE_KSUB), :],
                                     buf_ref.at[tile % 2, e, kb * MOE_KSUB:(kb + 1) * MOE_KSUB, :],
                                     sem_ref.at[(tile % 2) * n_experts * n_sub + e * n_sub + kb])

    def start_tile(tile):
        for e in range(n_experts):
            for kb in range(n_sub):
                @pl.when(kb * MOE_KSUB < seg_ref[tile * n_experts + e])
                def _(e=e, kb=kb):
                    sub_copy(tile, e, kb).start()

    @pl.when(i == 0)
    def _():
        start_tile(i)

    @pl.when(i + 1 < n_tiles)
    def _():
        start_tile(i + 1)

    rec = route_ref[...]
    i1, i2 = rec[:, R_I1:R_I1 + 1], rec[:, R_I2:R_I2 + 1]
    w1, w2 = rec[:, R_W1:R_W1 + 1], rec[:, R_W2:R_W2 + 1]
    lr1, lr2 = rec[:, R_LR1:R_LR1 + 1], rec[:, R_LR2:R_LR2 + 1]
    o_ref[...] = x_ref[...]
    for e in range(n_experts):
        in_e1 = i1 == float(e)
        in_e2 = i2 == float(e)
        lr = jnp.where(in_e1, lr1, jnp.where(in_e2, lr2, -1.0))
        w = jnp.where(in_e1, w1, jnp.where(in_e2, w2, 0.0))
        seg = seg_ref[i * n_experts + e]
        for kb in range(n_sub):
            @pl.when(kb * MOE_KSUB < seg)
            def _(kb=kb, lr=lr, w=w, e=e):
                sub_copy(i, e, kb).wait()
                slot = (lax.broadcasted_iota(I32, (tr, MOE_KSUB), 1) + kb * MOE_KSUB).astype(F32)
                pick = jnp.where(slot == lr, 1.0, 0.0).astype(BF16)
                o_ref[...] += w * _dot(pick, buf_ref[cur, e, kb * MOE_KSUB:(kb + 1) * MOE_KSUB, :])
    if final_norm_fused:
        o_ref[...] = _rms(o_ref[...], gain_ref[...])


def moe_combine(x, route, ys, starts_flat, seg_flat, n_experts, final_gain):
    m, d = x.shape
    tr = MOE_TR
    fused = final_gain is not None
    gain = final_gain if fused else jnp.ones((1, d), F32)
    grid_spec = pltpu.PrefetchScalarGridSpec(
        num_scalar_prefetch=2,
        grid=(m // tr,),
        in_specs=[pl.BlockSpec((tr, d), lambda i, s, g: (i, 0)),
                  pl.BlockSpec((tr, LANES), lambda i, s, g: (i, 0)),
                  pl.BlockSpec((1, d), lambda i, s, g: (0, 0)),
                  pl.BlockSpec(memory_space=pl.ANY)],
        out_specs=pl.BlockSpec((tr, d), lambda i, s, g: (i, 0)),
        scratch_shapes=[pltpu.VMEM((2, n_experts, tr, d), BF16),
                        pltpu.SemaphoreType.DMA((2 * n_experts * (tr // MOE_KSUB),))])
    return pl.pallas_call(
        functools.partial(_moe_combine_kernel, n_experts=n_experts, final_norm_fused=fused),
        out_shape=jax.ShapeDtypeStruct((m, d), F32),
        grid_spec=grid_spec,
        compiler_params=_cparams(("arbitrary",)),
        name="moe_combine",
    )(starts_flat, seg_flat, x, route, gain, ys)


TILE_DEAD, TILE_COMPUTE, TILE_ZERO = 0, 1, 2


def _moe_tile_plan(seg, n_experts, region, n_steps):
    tiles_per_region = region // MOE_TG
    rows = jnp.sum(seg, axis=0)
    n_compute = (rows + MOE_TG - 1) // MOE_TG
    n_live = jnp.minimum((rows + MOE_TR + MOE_TG - 1) // MOE_TG, tiles_per_region)
    ends = jnp.cumsum(n_live)
    total = ends[-1]
    step = jnp.arange(n_steps, dtype=I32)
    t = jnp.minimum(step, total - 1)
    texp = jnp.sum((t[:, None] >= ends[None, :]).astype(I32), axis=1)
    within = t - (ends - n_live)[texp]
    trow = texp * tiles_per_region + within
    kind = jnp.where(step < total, jnp.where(within < n_compute[texp], TILE_COMPUTE, TILE_ZERO), TILE_DEAD)
    return trow.astype(I32), texp.astype(I32), kind.astype(I32)


def moe_ffn(x, g, router_w, w_gu, w_down, final_gain=None):
    m, d = x.shape
    n_experts = router_w.shape[-1]
    region = m + MOE_TG
    n_tiles = m // MOE_TR
    rw = jnp.pad(router_w, ((0, 0), (0, LANES - n_experts)))
    xs, route, starts, seg = moe_route_dispatch(x, g, rw, n_experts)
    starts = starts[:, 0, :n_experts]
    seg = seg[:, 0, :n_experts]
    max_rows = TOP_K * m + n_tiles * n_experts * (MOE_ALIGN - 1)
    n_steps = max_rows // MOE_TG + 2 * n_experts
    trow, texp, tval = _moe_tile_plan(seg, n_experts, region, n_steps)
    ys = moe_grouped_ffn(xs, w_gu, w_down, trow, texp, tval)
    return moe_combine(x, route, ys, starts.reshape(-1), seg.reshape(-1), n_experts, final_gain)


def _final_norm_kernel(x_ref, g_ref, o_ref):
    o_ref[...] = _rms(x_ref[...], g_ref[...])


def final_norm(x, g):
    m, d = x.shape
    tm = min(m, 1024)
    return pl.pallas_call(
        _final_norm_kernel,
        out_shape=jax.ShapeDtypeStruct((m, d), F32),
        grid=(m // tm,),
        in_specs=[pl.BlockSpec((tm, d), lambda i: (i, 0)), pl.BlockSpec((1, d), lambda i: (0, 0))],
        out_specs=pl.BlockSpec((tm, d), lambda i: (i, 0)),
        compiler_params=_cparams(("parallel",)),
        name="final_norm",
    )(x, g)


def _column_plan(d_model):
    sizes = (SWA_HEADS * HEAD_DIM, SWA_KV_HEADS * HEAD_DIM, SWA_KV_HEADS * HEAD_DIM,
             DSA_HEADS * HEAD_DIM, DSA_KV_HEADS * HEAD_DIM, DSA_KV_HEADS * HEAD_DIM,
             IDX_HEADS * IDX_DIM, IDX_DIM, IDX_HEADS,
             RET_HEADS * RET_KEY_DIM, RET_HEADS * RET_KEY_DIM, RET_HEADS * RET_VAL_DIM, RET_HEADS * RET_VAL_DIM,
             len(POOL_WINDOWS) * POOL_GROUP_DIM, N_BRANCH * d_model)
    starts = np.cumsum((0,) + sizes[:-1])
    (a_q, a_k, a_v, b_q, b_k, b_v, i_q, i_k, i_w, r_q, r_k, r_v, r_g, p_u, gates) = [
        (np.arange(s, s + n), np.ones(n)) for s, n in zip(starts, sizes)]

    def scaled(c, f):
        return c[0], c[1] * f

    def dup_heads(c):
        return tuple(np.repeat(z.reshape(-1, 1, HEAD_DIM), 2, axis=1).reshape(-1) for z in c)

    def rotate_half(c):
        idx, fac = (z.reshape(-1, 2, HEAD_DIM // 2) for z in c)
        return (np.stack([idx[:, 1], idx[:, 0]], axis=1).reshape(-1),
                np.stack([-fac[:, 1], fac[:, 0]], axis=1).reshape(-1))

    def table(cols, width):
        idx = np.concatenate([c[0] for c in cols])
        fac = np.concatenate([c[1] for c in cols])
        pad = width - idx.size
        return (np.concatenate([idx, np.zeros(pad, idx.dtype)]).astype(np.int32),
                np.concatenate([fac, np.zeros(pad)]).astype(np.float32))

    qs = HEAD_DIM ** -0.5
    rope_cols = [scaled(a_q, qs), scaled(b_q, qs), scaled(i_q, IDX_DIM ** -0.5), r_q,
                 scaled(r_k, RET_KEY_DIM ** -0.5), dup_heads(a_k), dup_heads(b_k), dup_heads(i_k)]
    plain_cols = [gates, r_v, r_g, p_u, dup_heads(a_v), b_v, scaled(i_w, IDX_HEADS ** -0.5)]
    return (table(rope_cols, PA_COLS), table([rotate_half(c) for c in rope_cols], PA_COLS),
            table(plain_cols, PB_COLS))


def _layout_w_in(w_in, plan):
    return tuple((jnp.take(w_in, idx, axis=1) * fac[None, :]).astype(BF16) for idx, fac in plan)


def kernel(x, mem, positions, norm_mix, w_in, attn_sink, w_branch, w_out, pool_w, pool_scale, norm_cross, norm_mem, w_xq, w_xkv, w_xo, norm_ffn, dense_w_gu, dense_w_down, router_w, moe_w_gu, moe_w_down, norm_final):
    batch, seq, d = x.shape
    depth = norm_mix.shape[0]
    m = batch * seq
    mem_len = mem.shape[1]
    xs = x.reshape(m, d)
    mem2 = mem.reshape(batch * mem_len, d)

    inv = 1.0 / (ROPE_THETA ** (jnp.arange(0, HEAD_DIM, 2, dtype=F32) / HEAD_DIM))
    inv128 = jnp.tile(inv, LANES // (HEAD_DIM // 2))[None, :]
    cos, sin = rope_tables(positions.reshape(m, 1), inv128)
    log_gamma = jnp.log1p(-(2.0 ** (-5.0 - jnp.arange(RET_HEADS, dtype=F32))))
    x_scale = (d // X_HEADS) ** -0.5
    plan = _column_plan(d)

    for layer in range(depth):
        w_x, w_rot, w_plain = _layout_w_in(w_in[layer], plan)
        pa = norm_matmul_rope(xs, norm_mix[layer][None, :], w_x, w_rot, cos, sin, tm=1024, tn=PA_TN)
        pb = norm_matmul(xs, norm_mix[layer][None, :], w_plain, tm=1024, tn=PB_TN, name="in_proj_plain")
        o_a = swa_attention(pa, pb, attn_sink[layer], batch, seq)
        o_b = dsa_attention(pa, pb, batch, seq)
        o_c = retention(pa, pb, log_gamma, batch, seq)
        o_d = multiscale_pool(pb, pool_w[layer].astype(BF16), pool_scale[layer][None, :], batch, seq)
        xs = merge_branches(xs, (o_a, o_b, o_c, o_d), pb, w_branch[layer].astype(BF16), w_out[layer].astype(BF16))

        kv = norm_matmul(mem2, norm_mem[layer][None, :], w_xkv[layer].astype(BF16),
                         tm=batch * mem_len, tn=512, name="mem_kv_proj")
        xs = cross_attention(xs, norm_cross[layer][None, :], (w_xq[layer] * x_scale).astype(BF16), kv,
                             w_xo[layer].astype(BF16), batch, seq)

        li = layer // 2
        if layer % 2 == 0:
            xs = ffn_dense(xs, norm_ffn[layer][None, :], dense_w_gu[li].astype(BF16), dense_w_down[li].astype(BF16))
        else:
            last = layer == depth - 1
            xs = moe_ffn(xs, norm_ffn[layer][None, :], router_w[li], moe_w_gu[li].astype(BF16),
                         moe_w_down[li].astype(BF16), final_gain=norm_final[None, :] if last else None)

    if depth % 2 == 1:
        xs = final_norm(xs, norm_final[None, :])
    return xs.reshape(batch, seq, d)
```

```python
import functools

import numpy as np
import jax
import jax.numpy as jnp
from jax import lax
from jax.experimental import pallas as pl
from jax.experimental.pallas import tpu as pltpu

F32 = jnp.float32
BF16 = jnp.bfloat16
I32 = jnp.int32

EPS = 1e-6
HEAD_DIM = 64
ROPE_THETA = 10000.0
BLOCK = 128
SWA_HEADS, SWA_KV_HEADS = 8, 2
DSA_HEADS, DSA_KV_HEADS = 8, 2
IDX_HEADS, IDX_DIM = 4, 64
DSA_TOPK = 256
RET_HEADS, RET_KEY_DIM, RET_VAL_DIM = 4, 64, 128
RET_CHUNK = 256
POOL_WINDOWS = (2, 4, 8, 16)
POOL_GROUP_DIM = 128
N_BRANCH = 4
BRANCH_WIDTH = 512
X_HEADS = 4
TOP_K = 2

LANES = 128
VMEM_LIMIT = 56 * 1024 * 1024
NEG_BIG = -1e30
LOG2_E = 1.4426950408889634
INT_MIN = -2 ** 31

PA_AQ, PA_BQ, PA_IQ, PA_RQ, PA_RK, PA_AK, PA_BK, PA_IK = 0, 512, 1024, 1280, 1536, 1792, 2048, 2304
PA_COLS, PA_TN = 2560, 1280
PB_GATES, PB_RV, PB_RG, PB_PU, PB_AV, PB_BV, PB_IW = 0, 4096, 4608, 5120, 5632, 5888, 6016
PB_COLS, PB_TN = 6144, 1536


def _cparams(sem):
    return pltpu.CompilerParams(dimension_semantics=sem, vmem_limit_bytes=VMEM_LIMIT)


def _dot(a, b):
    return jnp.dot(a, b, preferred_element_type=F32)


def _dot_nt(a, b):
    return lax.dot_general(a, b, (((1,), (1,)), ((), ())), preferred_element_type=F32)


def _rms(x, g):
    return x * lax.rsqrt(jnp.mean(x * x, axis=-1, keepdims=True) + EPS) * g


def _half_mask(hh):
    lane = lax.broadcasted_iota(I32, (1, LANES), 1)
    return (lane >= HEAD_DIM) if hh else (lane < HEAD_DIM)


def _keep_half(block, hh):
    return jnp.where(_half_mask(hh), block, jnp.zeros_like(block))


def _rope_kernel(pos_ref, inv_ref, cos_ref, sin_ref):
    ang = pos_ref[...].astype(F32) * inv_ref[...]
    cos_ref[...] = jnp.cos(ang)
    sin_ref[...] = jnp.sin(ang)


def rope_tables(pos_col, inv128):
    m = pos_col.shape[0]
    tm = min(m, 2048)
    return pl.pallas_call(
        _rope_kernel,
        out_shape=(jax.ShapeDtypeStruct((m, LANES), F32), jax.ShapeDtypeStruct((m, LANES), F32)),
        grid=(m // tm,),
        in_specs=[pl.BlockSpec((tm, 1), lambda i: (i, 0)), pl.BlockSpec((1, LANES), lambda i: (0, 0))],
        out_specs=(pl.BlockSpec((tm, LANES), lambda i: (i, 0)), pl.BlockSpec((tm, LANES), lambda i: (i, 0))),
        compiler_params=_cparams(("parallel",)),
        name="rope_tables",
    )(pos_col, inv128)


def _norm_matmul_kernel(x_ref, g_ref, w_ref, o_ref, h_ref):
    @pl.when(pl.program_id(1) == 0)
    def _():
        h_ref[...] = _rms(x_ref[...], g_ref[...]).astype(BF16)

    o_ref[...] = _dot(h_ref[...], w_ref[...]).astype(o_ref.dtype)


def norm_matmul(x, g, w, tm, tn, out_dtype=BF16, name="norm_matmul"):
    m, k = x.shape
    n = w.shape[1]
    tm = min(tm, m)
    return pl.pallas_call(
        _norm_matmul_kernel,
        out_shape=jax.ShapeDtypeStruct((m, n), out_dtype),
        grid=(m // tm, n // tn),
        in_specs=[pl.BlockSpec((tm, k), lambda i, j: (i, 0)),
                  pl.BlockSpec((1, k), lambda i, j: (0, 0)),
                  pl.BlockSpec((k, tn), lambda i, j: (0, j))],
        out_specs=pl.BlockSpec((tm, tn), lambda i, j: (i, j)),
        scratch_shapes=[pltpu.VMEM((tm, k), BF16)],
        compiler_params=_cparams(("parallel", "arbitrary")),
        name=name,
    )(x, g, w)


def _norm_matmul_rope_kernel(x_ref, g_ref, wx_ref, wr_ref, cos_ref, sin_ref, o_ref, h_ref):
    @pl.when(pl.program_id(1) == 0)
    def _():
        h_ref[...] = _rms(x_ref[...], g_ref[...]).astype(BF16)

    h = h_ref[...]
    proj = _dot(h, wx_ref[...])
    rot = _dot(h, wr_ref[...])
    cos = cos_ref[...]
    sin = sin_ref[...]
    for c in range(o_ref.shape[1] // LANES):
        cols = slice(c * LANES, (c + 1) * LANES)
        o_ref[:, cols] = (proj[:, cols] * cos + rot[:, cols] * sin).astype(o_ref.dtype)


def norm_matmul_rope(x, g, wx, wr, cos, sin, tm, tn):
    m, k = x.shape
    n = wx.shape[1]
    tm = min(tm, m)
    return pl.pallas_call(
        _norm_matmul_rope_kernel,
        out_shape=jax.ShapeDtypeStruct((m, n), BF16),
        grid=(m // tm, n // tn),
        in_specs=[pl.BlockSpec((tm, k), lambda i, j: (i, 0)),
                  pl.BlockSpec((1, k), lambda i, j: (0, 0)),
                  pl.BlockSpec((k, tn), lambda i, j: (0, j)),
                  pl.BlockSpec((k, tn), lambda i, j: (0, j)),
                  pl.BlockSpec((tm, LANES), lambda i, j: (i, 0)),
                  pl.BlockSpec((tm, LANES), lambda i, j: (i, 0))],
        out_specs=pl.BlockSpec((tm, tn), lambda i, j: (i, j)),
        scratch_shapes=[pltpu.VMEM((tm, k), BF16)],
        compiler_params=_cparams(("parallel", "arbitrary")),
        name="in_proj_rope",
    )(x, g, wx, wr, cos, sin)


def _swa_kernel(sink_ref, q_ref, kc_ref, kp_ref, vc_ref, vp_ref, o_ref):
    i = pl.program_id(1)
    per_group = SWA_HEADS // SWA_KV_HEADS
    q_t = (q_ref[...].astype(F32) * LOG2_E).T.astype(BF16)
    key = lax.broadcasted_iota(I32, (2 * BLOCK, 1), 0)
    rel = lax.broadcasted_iota(I32, (1, BLOCK), 1) + BLOCK - key
    band = jnp.logical_and(jnp.logical_and(rel >= 0, rel < BLOCK), jnp.logical_or(key >= BLOCK, i > 0))
    bias = jnp.where(band, 0.0, -jnp.inf).astype(F32)
    bias = jnp.concatenate([bias] * per_group, axis=1)
    outs = []
    for g in range(SWA_KV_HEADS):
        cols = slice(g * LANES, (g + 1) * LANES)
        k_win = jnp.concatenate([kp_ref[:, cols], kc_ref[:, cols]], axis=0)[:, 0:HEAD_DIM]
        v_t = jnp.concatenate([vp_ref[:, cols], vc_ref[:, cols]], axis=0).astype(F32).T[0:HEAD_DIM].astype(BF16)
        heads = range(g * per_group, (g + 1) * per_group)
        q_g = jnp.concatenate([q_t[h * HEAD_DIM:(h + 1) * HEAD_DIM, :] for h in heads], axis=1)
        sink = jnp.concatenate([jnp.full((1, BLOCK), sink_ref[h] * LOG2_E, F32) for h in heads], axis=1)
        s = _dot(k_win, q_g) + bias
        m = jnp.maximum(jnp.max(s, axis=0, keepdims=True), sink)
        p = jnp.exp2(s - m)
        den = jnp.sum(p, axis=0, keepdims=True) + jnp.exp2(sink - m)
        o_t = _dot(v_t, p.astype(BF16)) / den
        outs += [o_t[:, a * BLOCK:(a + 1) * BLOCK] for a in range(per_group)]
    o_ref[...] = jnp.concatenate(outs, axis=0).T.astype(o_ref.dtype)


def swa_attention(pa, pb, sink, batch, seq):
    nb = seq // BLOCK
    cur = lambda col: (lambda b, i: (b * nb + i, col))
    prev = lambda col: (lambda b, i: (b * nb + jnp.maximum(i - 1, 0), col))
    return pl.pallas_call(
        _swa_kernel,
        out_shape=jax.ShapeDtypeStruct((batch * seq, BRANCH_WIDTH), BF16),
        grid=(batch, nb),
        in_specs=[pl.BlockSpec(memory_space=pltpu.SMEM),
                  pl.BlockSpec((BLOCK, 512), cur(PA_AQ // 512)),
                  pl.BlockSpec((BLOCK, 256), cur(PA_AK // 256)),
                  pl.BlockSpec((BLOCK, 256), prev(PA_AK // 256)),
                  pl.BlockSpec((BLOCK, 256), cur(PB_AV // 256)),
                  pl.BlockSpec((BLOCK, 256), prev(PB_AV // 256))],
        out_specs=pl.BlockSpec((BLOCK, BRANCH_WIDTH), cur(0)),
        compiler_params=_cparams(("parallel", "arbitrary")),
        name="swa_attention",
    )(sink, pa, pa, pa, pb, pb)


DSA_TQ = 256
DSA_ONES = 16


def _dsa_kernel(q_ref, iq_ref, iw_ref, k_ref, ik_ref, v_ref, o_ref,
                skey_ref, vt_ref, qt_ref, cut_ref, m_ref, l_ref, acc_ref, alpha_ref, bias_ref, s_ref, p_ref, odd_ref,
                *, n_sel, seq):
    tq = DSA_TQ
    nq = seq // tq
    i = pl.program_id(1)
    n_chunks = i + 1
    q_pos = i * tq + lax.broadcasted_iota(I32, (1, tq), 1)
    key_off = lax.broadcasted_iota(I32, (tq, 1), 0)

    @pl.when(i == 0)
    def _():
        for jj in range(nq):
            vt = v_ref[jj * tq:(jj + 1) * tq, :].astype(F32).T
            ones = jnp.ones((DSA_ONES, tq), BF16)
            for g in range(DSA_KV_HEADS):
                vt_ref[jj, g, 0:HEAD_DIM, :] = vt[g * HEAD_DIM:(g + 1) * HEAD_DIM].astype(BF16)
                vt_ref[jj, g, HEAD_DIM:, :] = ones

    iq_t = iq_ref[...].astype(F32).T.astype(BF16)
    w_t = iw_ref[...].astype(F32).T

    def chunk_keys(j):
        ikc = ik_ref[pl.ds(pl.multiple_of(j * tq, tq), tq), 0:IDX_DIM]
        sc = jnp.zeros((tq, tq), F32)
        for h in range(IDX_HEADS):
            sc = sc + jnp.maximum(_dot(ikc, iq_t[h * IDX_DIM:(h + 1) * IDX_DIM, :]), 0.0) * w_t[h:h + 1, :]
        sc = jnp.where(sc == 0.0, 0.0, sc)
        bits = lax.bitcast_convert_type(sc, I32)
        return jnp.where(bits < 0, bits ^ jnp.int32(0x7FFFFFFF), bits)

    def score_chunk(j, carry):
        skey_ref[j] = chunk_keys(j)
        return carry

    lax.fori_loop(0, i, score_chunk, 0)
    skey_ref[i] = jnp.where((i * tq + key_off) <= q_pos, chunk_keys(i), jnp.int32(INT_MIN))

    view = (tq // 8, 8, tq)
    pos_in_chunk = lax.broadcasted_iota(I32, view, 0) * 8 + lax.broadcasted_iota(I32, view, 1)

    def all_sublanes(x, op):
        for sh in (4, 2, 1):
            x = op(x, pltpu.roll(x, sh, axis=0))
        return x

    def count_keys(pred):
        def chunk_count(j):
            x = pred(skey_ref[j].reshape(view), j)
            part = view[0] // 4
            sums = [jnp.sum(x[a * part:(a + 1) * part], axis=0) for a in range(4)]
            return (sums[0] + sums[1]) + (sums[2] + sums[3])

        def pair(jj, acc):
            return acc + (chunk_count(2 * jj) + chunk_count(2 * jj + 1))

        odd_ref[...] = jnp.zeros((8, tq), F32)

        @pl.when(n_chunks % 2 == 1)
        def _():
            odd_ref[...] = chunk_count(n_chunks - 1)

        acc = lax.fori_loop(0, n_chunks // 2, pair, jnp.zeros((8, tq), F32))
        return all_sublanes(acc + odd_ref[...], jnp.add)

    one = jnp.float32(1.0)
    zero = jnp.float32(0.0)
    k_f = jnp.float32(n_sel)
    cut_bits = max((seq - 1).bit_length(), 1)

    def bit_step(b, carry):
        res_u, c_ge = carry
        cand_u = res_u | lax.shift_left(jnp.int32(1), 31 - b)
        cand_s = cand_u ^ jnp.int32(INT_MIN)
        cnt = count_keys(lambda kc, j: jnp.where(kc >= cand_s, one, zero))
        take = cnt >= k_f
        return jnp.where(take, cand_u, res_u), jnp.where(take, cnt, c_ge)

    res_u, c_ge = lax.fori_loop(0, 32, bit_step, (jnp.zeros((8, tq), I32), jnp.zeros((8, tq), F32)))
    thr = res_u ^ jnp.int32(INT_MIN)
    short = thr == jnp.int32(INT_MIN)
    cut_ref[...] = jnp.where(short, jnp.int32(-1), jnp.int32(seq))
    unresolved = jnp.where(jnp.logical_and(c_ge != k_f, jnp.logical_not(short)), one, zero)

    @pl.when(jnp.max(unresolved) > 0.0)
    def _():
        need = k_f - count_keys(lambda kc, j: jnp.where(kc > thr, one, zero))

        def cut_step(b, ans):
            cand = ans | lax.shift_left(jnp.int32(1), cut_bits - 1 - b)
            lim = cand - 1

            def pred(kc, j):
                return jnp.where(kc == thr, jnp.where((j * tq + pos_in_chunk) <= lim, one, zero), zero)

            return jnp.where(count_keys(pred) < need, cand, ans)

        ans = lax.fori_loop(0, cut_bits, cut_step, jnp.zeros((8, tq), I32))
        cut_ref[...] = jnp.where(short, jnp.int32(-1), ans)

    qt_ref[...] = (q_ref[...].astype(F32) * LOG2_E).T.astype(BF16)
    m_ref[...] = jnp.full(m_ref.shape, NEG_BIG, F32)
    l_ref[...] = jnp.zeros(l_ref.shape, F32)
    acc_ref[...] = jnp.zeros(acc_ref.shape, F32)
    cut = cut_ref[...]
    neg = jnp.float32(NEG_BIG)
    group = DSA_HEADS // DSA_KV_HEADS

    def attend(j, carry):
        kc = skey_ref[j].reshape(view)
        start = pl.multiple_of(j * tq, tq)
        tie = jnp.where(kc == thr, jnp.where((j * tq + pos_in_chunk) <= cut, zero, neg), neg)
        bias_ref[...] = jnp.where(kc > thr, zero, tie).reshape(tq, tq)
        for h in range(DSA_HEADS):
            g = h // group
            kblk = k_ref[pl.ds(start, tq), g * LANES:g * LANES + HEAD_DIM]
            s = _dot(kblk, qt_ref[h * HEAD_DIM:(h + 1) * HEAD_DIM, :]) + bias_ref[...]
            s_ref[h] = s
            m_old = m_ref[h]
            m_new = jnp.maximum(m_old, jnp.max(s, axis=0, keepdims=True))
            alpha_ref[h] = jnp.exp2(m_old - m_new)
            m_ref[h] = m_new
        for h in range(DSA_HEADS):
            p_ref[h] = jnp.exp2(s_ref[h] - m_ref[h]).astype(BF16)
        for h in range(DSA_HEADS):
            pv = _dot(vt_ref[j, h // group], p_ref[h])
            acc_ref[h] = alpha_ref[h] * acc_ref[h] + pv[0:HEAD_DIM]
            l_ref[h] = alpha_ref[h] * l_ref[h] + pv[HEAD_DIM:HEAD_DIM + 1]
        return carry

    lax.fori_loop(0, n_chunks, attend, 0)

    for p in range(DSA_HEADS // 2):
        o_t = jnp.concatenate([acc_ref[2 * p] / l_ref[2 * p], acc_ref[2 * p + 1] / l_ref[2 * p + 1]], axis=0)
        o_ref[:, p * LANES:(p + 1) * LANES] = o_t.T.astype(o_ref.dtype)


def dsa_attention(pa, pb, batch, seq):
    tq = DSA_TQ
    nq = seq // tq
    n_sel = min(DSA_TOPK, seq // 4)
    qmap = lambda col: (lambda b, i: (b * nq + i, col))
    smap = lambda col: (lambda b, i: (b, col))
    return pl.pallas_call(
        functools.partial(_dsa_kernel, n_sel=n_sel, seq=seq),
        out_shape=jax.ShapeDtypeStruct((batch * seq, BRANCH_WIDTH), BF16),
        grid=(batch, nq),
        in_specs=[pl.BlockSpec((tq, 512), qmap(PA_BQ // 512)),
                  pl.BlockSpec((tq, 256), qmap(PA_IQ // 256)),
                  pl.BlockSpec((tq, LANES), qmap(PB_IW // LANES)),
                  pl.BlockSpec((seq, 256), smap(PA_BK // 256)),
                  pl.BlockSpec((seq, LANES), smap(PA_IK // LANES)),
                  pl.BlockSpec((seq, LANES), smap(PB_BV // LANES))],
        out_specs=pl.BlockSpec((tq, BRANCH_WIDTH), qmap(0)),
        scratch_shapes=[pltpu.VMEM((nq, tq, tq), I32),
                        pltpu.VMEM((nq, DSA_KV_HEADS, HEAD_DIM + DSA_ONES, tq), BF16),
                        pltpu.VMEM((DSA_HEADS * HEAD_DIM, tq), BF16),
                        pltpu.VMEM((8, tq), I32),
                        pltpu.VMEM((DSA_HEADS, 1, tq), F32),
                        pltpu.VMEM((DSA_HEADS, 1, tq), F32),
                        pltpu.VMEM((DSA_HEADS, HEAD_DIM, tq), F32),
                        pltpu.VMEM((DSA_HEADS, 1, tq), F32),
                        pltpu.VMEM((tq, tq), F32),
                        pltpu.VMEM((DSA_HEADS, tq, tq), F32),
                        pltpu.VMEM((DSA_HEADS, tq, tq), BF16),
                        pltpu.VMEM((8, tq), F32)],
        compiler_params=_cparams(("parallel", "arbitrary")),
        name="dsa_attention",
    )(pa, pa, pb, pa, pa, pb)


def _retention_kernel(lg_ref, q_ref, k_ref, v_ref, g_ref, o_ref, state_ref):
    cdim = RET_CHUNK

    @pl.when(pl.program_id(1) == 0)
    def _():
        state_ref[...] = jnp.zeros(state_ref.shape, F32)

    n = lax.broadcasted_iota(I32, (cdim, 1), 0).astype(F32)
    mcol = lax.broadcasted_iota(I32, (1, cdim), 1).astype(F32)
    rel = n - mcol
    for h in range(RET_HEADS):
        lg = lg_ref[h]
        decay = jnp.where(rel >= 0, jnp.exp(lg * jnp.maximum(rel, 0.0)), 0.0)
        q_decay = jnp.exp(lg * (n + 1.0))
        k_decay = jnp.exp(lg * (cdim - 1.0 - n))
        chunk_decay = jnp.exp(jnp.full((1, LANES), lg * cdim, F32))
        blk = slice((h // 2) * LANES, (h // 2 + 1) * LANES)
        qm = _keep_half(q_ref[:, blk], h % 2)
        km = _keep_half(k_ref[:, blk], h % 2)
        v = v_ref[:, h * RET_VAL_DIM:(h + 1) * RET_VAL_DIM]
        state = state_ref[h]
        inner = _dot((_dot_nt(qm, km) * decay).astype(BF16), v)
        cross = _dot(qm, state.astype(BF16)) * q_decay
        kd_t = (km.astype(F32) * k_decay).T.astype(BF16)
        state_ref[h] = state * chunk_decay + _dot(kd_t, v)
        out = inner + cross
        out = out * lax.rsqrt(jnp.mean(out * out, axis=-1, keepdims=True) + EPS)
        gate = g_ref[:, h * RET_VAL_DIM:(h + 1) * RET_VAL_DIM].astype(F32)
        silu = gate * (1.0 / (1.0 + jnp.exp(-gate)))
        o_ref[:, h * RET_VAL_DIM:(h + 1) * RET_VAL_DIM] = (silu * out).astype(o_ref.dtype)


def retention(pa, pb, log_gamma, batch, seq):
    nc = seq // RET_CHUNK
    cmap = lambda col: (lambda b, c: (b * nc + c, col))
    return pl.pallas_call(
        _retention_kernel,
        out_shape=jax.ShapeDtypeStruct((batch * seq, BRANCH_WIDTH), BF16),
        grid=(batch, nc),
        in_specs=[pl.BlockSpec(memory_space=pltpu.SMEM),
                  pl.BlockSpec((RET_CHUNK, 256), cmap(PA_RQ // 256)),
                  pl.BlockSpec((RET_CHUNK, 256), cmap(PA_RK // 256)),
                  pl.BlockSpec((RET_CHUNK, 512), cmap(PB_RV // 512)),
                  pl.BlockSpec((RET_CHUNK, 512), cmap(PB_RG // 512))],
        out_specs=pl.BlockSpec((RET_CHUNK, BRANCH_WIDTH), cmap(0)),
        scratch_shapes=[pltpu.VMEM((RET_HEADS, LANES, RET_VAL_DIM), F32)],
        compiler_params=_cparams(("parallel", "arbitrary")),
        name="retention",
    )(log_gamma, pa, pa, pb, pb)


POOL_HALO = 16


def _pool_kernel(cur_ref, prev_ref, w_ref, scale_ref, o_ref, *, tiles_per_seq):
    tm = cur_ref.shape[0]
    it = pl.program_id(0) % tiles_per_seq
    cur = cur_ref[...].astype(F32)
    prev = jnp.where(it > 0, prev_ref[...].astype(F32), 0.0)
    ext = jnp.concatenate([prev, cur], axis=0)
    t = it * tm + lax.broadcasted_iota(I32, (tm, 1), 0)
    for gi, win in enumerate(POOL_WINDOWS):
        cols = slice(gi * POOL_GROUP_DIM, (gi + 1) * POOL_GROUP_DIM)
        s = ext[:, cols]
        sh = 1
        while sh < win:
            s = s + pltpu.roll(s, sh, axis=0)
            sh *= 2
        cnt = jnp.minimum(t + 1, win).astype(F32)
        pooled = s[POOL_HALO:, :] / cnt
        y = _dot((pooled - cur[:, cols]).astype(BF16), w_ref[gi])
        o_ref[:, cols] = (y * scale_ref[:, cols]).astype(o_ref.dtype)


def multiscale_pool(pb, pool_w, pool_scale, batch, seq):
    m = batch * seq
    tm = min(seq, 512)
    tps = seq // tm
    ratio = tm // POOL_HALO
    return pl.pallas_call(
        functools.partial(_pool_kernel, tiles_per_seq=tps),
        out_shape=jax.ShapeDtypeStruct((m, BRANCH_WIDTH), BF16),
        grid=(m // tm,),
        in_specs=[pl.BlockSpec((tm, 512), lambda i: (i, PB_PU // 512)),
                  pl.BlockSpec((POOL_HALO, 512), lambda i: (jnp.maximum(i * ratio - 1, 0), PB_PU // 512)),
                  pl.BlockSpec((len(POOL_WINDOWS), POOL_GROUP_DIM, POOL_GROUP_DIM), lambda i: (0, 0, 0)),
                  pl.BlockSpec((1, 512), lambda i: (0, 0))],
        out_specs=pl.BlockSpec((tm, BRANCH_WIDTH), lambda i: (i, 0)),
        compiler_params=_cparams(("parallel",)),
        name="multiscale_pool",
    )(pb, pb, pool_w, pool_scale)


def _merge_kernel(x_ref, oa_ref, ob_ref, oc_ref, od_ref, gates_ref, wb_ref, wo_ref, o_ref):
    d = x_ref.shape[1]
    merged = jnp.zeros(x_ref.shape, F32)
    for bi, br in enumerate((oa_ref, ob_ref, oc_ref, od_ref)):
        gate = gates_ref[:, bi * d:(bi + 1) * d].astype(F32)
        merged = merged + (1.0 / (1.0 + jnp.exp(-gate))) * _dot(br[...], wb_ref[bi])
    o_ref[...] = x_ref[...] + _dot(merged.astype(BF16), wo_ref[...])


def merge_branches(x, branches, pb, w_branch, w_out):
    m, d = x.shape
    tm = min(m, 512)
    row = lambda i: (i, 0)
    return pl.pallas_call(
        _merge_kernel,
        out_shape=jax.ShapeDtypeStruct((m, d), F32),
        grid=(m // tm,),
        in_specs=[pl.BlockSpec((tm, d), row)]
        + [pl.BlockSpec((tm, BRANCH_WIDTH), row)] * N_BRANCH
        + [pl.BlockSpec((tm, N_BRANCH * d), lambda i: (i, PB_GATES // (N_BRANCH * d))),
           pl.BlockSpec((N_BRANCH, BRANCH_WIDTH, d), lambda i: (0, 0, 0)),
           pl.BlockSpec((d, d), lambda i: (0, 0))],
        out_specs=pl.BlockSpec((tm, d), row),
        compiler_params=_cparams(("parallel",)),
        name="merge_branches",
    )(x, *branches, pb, w_branch, w_out)


def _cross_kernel(x_ref, g_ref, wq_ref, k_ref, v_ref, wo_ref, o_ref):
    x = x_ref[...]
    d = x.shape[1]
    hd = d // X_HEADS
    q = _dot(_rms(x, g_ref[...]).astype(BF16), wq_ref[...]).astype(BF16)
    outs = []
    for h in range(X_HEADS):
        cols = slice(h * hd, (h + 1) * hd)
        s = _dot_nt(q[:, cols], k_ref[:, cols])
        p = jnp.exp(s - jnp.max(s, axis=1, keepdims=True))
        o = _dot(p.astype(BF16), v_ref[:, cols]) / jnp.sum(p, axis=1, keepdims=True)
        outs.append(o.astype(BF16))
    o_ref[...] = x + _dot(jnp.concatenate(outs, axis=1), wo_ref[...])


def cross_attention(x, g, wq, kv, wo, batch, seq):
    m, d = x.shape
    mem_len = kv.shape[0] // batch
    tm = min(seq, 512)
    nt = seq // tm
    return pl.pallas_call(
        _cross_kernel,
        out_shape=jax.ShapeDtypeStruct((m, d), F32),
        grid=(batch, nt),
        in_specs=[pl.BlockSpec((tm, d), lambda b, i: (b * nt + i, 0)),
                  pl.BlockSpec((1, d), lambda b, i: (0, 0)),
                  pl.BlockSpec((d, d), lambda b, i: (0, 0)),
                  pl.BlockSpec((mem_len, d), lambda b, i: (b, 0)),
                  pl.BlockSpec((mem_len, d), lambda b, i: (b, 1)),
                  pl.BlockSpec((d, d), lambda b, i: (0, 0))],
        out_specs=pl.BlockSpec((tm, d), lambda b, i: (b * nt + i, 0)),
        compiler_params=_cparams(("parallel", "arbitrary")),
        name="cross_attention",
    )(x, g, wq, kv, kv, wo)


FFN_TF = 512


def _swiglu(h, wg, wu):
    a = _dot(h, wg)
    return a * (1.0 / (1.0 + jnp.exp(-a))) * _dot(h, wu)


def _ffn_kernel(x_ref, g_ref, wg_ref, wu_ref, wd_ref, o_ref, h_ref, acc_ref):
    f = pl.program_id(1)

    @pl.when(f == 0)
    def _():
        x = x_ref[...]
        h_ref[...] = _rms(x, g_ref[...]).astype(BF16)
        acc_ref[...] = x

    acc_ref[...] += _dot(_swiglu(h_ref[...], wg_ref[...], wu_ref[...]).astype(BF16), wd_ref[...])

    @pl.when(f == pl.num_programs(1) - 1)
    def _():
        o_ref[...] = acc_ref[...]


def ffn_dense(x, g, w_gu, w_down):
    m, d = x.shape
    dff = w_down.shape[0]
    tm = min(m, 1024)
    nf = dff // FFN_TF
    return pl.pallas_call(
        _ffn_kernel,
        out_shape=jax.ShapeDtypeStruct((m, d), F32),
        grid=(m // tm, nf),
        in_specs=[pl.BlockSpec((tm, d), lambda i, f: (i, 0)),
                  pl.BlockSpec((1, d), lambda i, f: (0, 0)),
                  pl.BlockSpec((d, FFN_TF), lambda i, f: (0, f)),
                  pl.BlockSpec((d, FFN_TF), lambda i, f: (0, nf + f)),
                  pl.BlockSpec((FFN_TF, d), lambda i, f: (f, 0))],
        out_specs=pl.BlockSpec((tm, d), lambda i, f: (i, 0)),
        scratch_shapes=[pltpu.VMEM((tm, d), BF16), pltpu.VMEM((tm, d), F32)],
        compiler_params=_cparams(("parallel", "arbitrary")),
        name="ffn_dense",
    )(x, g, w_gu, w_gu, w_down)


MOE_TR = 512
MOE_TG = 1024
MOE_ALIGN = 16
MOE_SUB = 128
MOE_KSUB = 256
R_I1, R_I2, R_W1, R_W2, R_LR1, R_LR2 = range(6)


def _route_dispatch_kernel(x_ref, g_ref, rw_ref, xs_hbm, route_ref, starts_ref, seg_ref,
                           buf_ref, sem_ref, cnt_ref, *, n_experts, region):
    i = pl.program_id(0)
    tr = MOE_TR

    @pl.when(i == 0)
    def _():
        for e in range(n_experts):
            cnt_ref[e] = 0

    hn = _rms(x_ref[...], g_ref[...])
    h = hn.astype(BF16)
    h_rem = (hn - h.astype(F32)).astype(BF16)
    rw = rw_ref[...]
    rw_head = rw.astype(BF16)
    rw_rem = (rw - rw_head.astype(F32)).astype(BF16)
    logits = (_dot(h, rw_head) + _dot(h_rem, rw_head)) + _dot(h, rw_rem)
    lane = lax.broadcasted_iota(I32, (1, LANES), 1).astype(F32)
    ninf = jnp.float32(-jnp.inf)
    lg = jnp.where(lane < n_experts, logits, ninf)
    m1 = jnp.max(lg, axis=1, keepdims=True)
    i1 = jnp.min(jnp.where(lg == m1, lane, float(LANES)), axis=1, keepdims=True)
    lg2 = jnp.where(lane == i1, ninf, lg)
    m2 = jnp.max(lg2, axis=1, keepdims=True)
    i2 = jnp.min(jnp.where(lg2 == m2, lane, float(LANES)), axis=1, keepdims=True)
    e2 = jnp.exp(m2 - m1)
    den = 1.0 + e2
    oh1 = lane == i1
    oh2 = lane == i2
    both = jnp.where(jnp.logical_or(oh1, oh2), 1.0, 0.0)
    tri = (lax.broadcasted_iota(I32, (tr, tr), 0) > lax.broadcasted_iota(I32, (tr, tr), 1))
    prefix = _dot(jnp.where(tri, 1.0, 0.0).astype(BF16), both.astype(BF16))
    lr1 = jnp.sum(jnp.where(oh1, prefix, 0.0), axis=1, keepdims=True)
    lr2 = jnp.sum(jnp.where(oh2, prefix, 0.0), axis=1, keepdims=True)
    record = jnp.zeros((tr, LANES), F32)
    for col, val in ((R_I1, i1), (R_I2, i2), (R_W1, 1.0 / den), (R_W2, e2 / den), (R_LR1, lr1), (R_LR2, lr2)):
        record = jnp.where(lane == col, val, record)
    route_ref[...] = record
    counts = jnp.sum(both, axis=0, keepdims=True)

    rec_t = record.T
    i1_t, i2_t = rec_t[R_I1:R_I1 + 1, :], rec_t[R_I2:R_I2 + 1, :]
    lr1_t, lr2_t = rec_t[R_LR1:R_LR1 + 1, :], rec_t[R_LR2:R_LR2 + 1, :]
    lane_i = lax.broadcasted_iota(I32, (1, LANES), 1)
    starts = jnp.zeros((1, LANES), I32)
    segs = jnp.zeros((1, LANES), I32)

    def block_copy(buf_slot, start):
        return pltpu.make_async_copy(buf_ref.at[buf_slot],
                                     xs_hbm.at[pl.ds(pl.multiple_of(start, MOE_ALIGN), tr), :],
                                     sem_ref.at[buf_slot])

    for e in range(n_experts):
        in_e1 = i1_t == float(e)
        lr = jnp.where(in_e1, lr1_t, jnp.where(i2_t == float(e), lr2_t, -1.0))
        seg = ((counts[0, e].astype(I32) + (MOE_ALIGN - 1)) // MOE_ALIGN) * MOE_ALIGN
        buf_slot = e % 2
        if e >= 2:
            block_copy(buf_slot, 0).wait()
        else:
            @pl.when(i > 0)
            def _(buf_slot=buf_slot):
                block_copy(buf_slot, 0).wait()
        for sb in range(tr // MOE_SUB):
            rows = slice(sb * MOE_SUB, (sb + 1) * MOE_SUB)

            @pl.when(sb * MOE_SUB < seg)
            def _(sb=sb, rows=rows, lr=lr, buf_slot=buf_slot):
                slot = (lax.broadcasted_iota(I32, (MOE_SUB, tr), 0) + sb * MOE_SUB).astype(F32)
                perm = jnp.where(slot == lr, 1.0, 0.0).astype(BF16)
                buf_ref[buf_slot, rows, :] = _dot(perm, h).astype(BF16)

            @pl.when(sb * MOE_SUB >= seg)
            def _(rows=rows, buf_slot=buf_slot):
                buf_ref[buf_slot, rows, :] = jnp.zeros((MOE_SUB, h.shape[1]), BF16)
        start = e * region + cnt_ref[e]
        block_copy(buf_slot, start).start()
        starts = jnp.where(lane_i == e, start, starts)
        segs = jnp.where(lane_i == e, seg, segs)
        cnt_ref[e] = cnt_ref[e] + seg

    starts_ref[0] = starts
    seg_ref[0] = segs

    @pl.when(i == pl.num_programs(0) - 1)
    def _():
        block_copy(0, 0).wait()
        block_copy(1, 0).wait()
        buf_ref[0] = jnp.zeros(buf_ref.shape[1:], BF16)
        for e in range(n_experts):
            for k in range(MOE_TG // tr + 1):
                tail = block_copy(0, e * region + jnp.minimum(cnt_ref[e] + k * tr, region - tr))
                tail.start()
                tail.wait()


def moe_route_dispatch(x, g, router_w_padded, n_experts):
    m, d = x.shape
    tr = MOE_TR
    n_tiles = m // tr
    region = m + MOE_TG
    kern = functools.partial(_route_dispatch_kernel, n_experts=n_experts, region=region)
    return pl.pallas_call(
        kern,
        out_shape=(jax.ShapeDtypeStruct((n_experts * region, d), BF16),
                   jax.ShapeDtypeStruct((m, LANES), F32),
                   jax.ShapeDtypeStruct((n_tiles, 1, LANES), I32),
                   jax.ShapeDtypeStruct((n_tiles, 1, LANES), I32)),
        grid=(n_tiles,),
        in_specs=[pl.BlockSpec((tr, d), lambda i: (i, 0)),
                  pl.BlockSpec((1, d), lambda i: (0, 0)),
                  pl.BlockSpec((d, LANES), lambda i: (0, 0))],
        out_specs=(pl.BlockSpec(memory_space=pl.ANY),
                   pl.BlockSpec((tr, LANES), lambda i: (i, 0)),
                   pl.BlockSpec((1, 1, LANES), lambda i: (i, 0, 0)),
                   pl.BlockSpec((1, 1, LANES), lambda i: (i, 0, 0))),
        scratch_shapes=[pltpu.VMEM((2, tr, d), BF16), pltpu.SemaphoreType.DMA((2,)),
                        pltpu.SMEM((n_experts,), I32)],
        compiler_params=_cparams(("arbitrary",)),
        name="moe_route_dispatch",
    )(x, g, router_w_padded)


def _ffn_grouped_kernel(trow_ref, texp_ref, tval_ref, x_ref, wg_ref, wu_ref, wd_ref, o_ref, acc_ref):
    t = pl.program_id(0)
    f = pl.program_id(1)

    @pl.when(jnp.logical_and(tval_ref[t] == TILE_ZERO, f == 0))
    def _():
        o_ref[...] = jnp.zeros(o_ref.shape, o_ref.dtype)

    @pl.when(tval_ref[t] == TILE_COMPUTE)
    def _():
        part = _dot(_swiglu(x_ref[...], wg_ref[0], wu_ref[0]).astype(BF16), wd_ref[0])

        @pl.when(f == 0)
        def _():
            acc_ref[...] = part

        @pl.when(f > 0)
        def _():
            acc_ref[...] += part

        @pl.when(f == pl.num_programs(1) - 1)
        def _():
            o_ref[...] = acc_ref[...].astype(o_ref.dtype)


def moe_grouped_ffn(xs, w_gu, w_down, trow, texp, tval):
    rows, d = xs.shape
    dff = w_down.shape[1]
    nf = dff // FFN_TF
    n_steps = trow.shape[0]
    col = lambda f, tv, t: jnp.where(tv[t] == TILE_COMPUTE, f, nf - 1)
    grid_spec = pltpu.PrefetchScalarGridSpec(
        num_scalar_prefetch=3,
        grid=(n_steps, nf),
        in_specs=[pl.BlockSpec((MOE_TG, d), lambda t, f, tr_, te, tv: (tr_[t], 0)),
                  pl.BlockSpec((1, d, FFN_TF), lambda t, f, tr_, te, tv: (te[t], 0, col(f, tv, t))),
                  pl.BlockSpec((1, d, FFN_TF), lambda t, f, tr_, te, tv: (te[t], 0, nf + col(f, tv, t))),
                  pl.BlockSpec((1, FFN_TF, d), lambda t, f, tr_, te, tv: (te[t], col(f, tv, t), 0))],
        out_specs=pl.BlockSpec((MOE_TG, d), lambda t, f, tr_, te, tv: (tr_[t], 0)),
        scratch_shapes=[pltpu.VMEM((MOE_TG, d), F32)])
    return pl.pallas_call(
        _ffn_grouped_kernel,
        out_shape=jax.ShapeDtypeStruct((rows, d), BF16),
        grid_spec=grid_spec,
        compiler_params=_cparams(("arbitrary", "arbitrary")),
        name="moe_grouped_ffn",
    )(trow, texp, tval, xs, w_gu, w_gu, w_down)


def _moe_combine_kernel(starts_ref, seg_ref, x_ref, route_ref, gain_ref, ys_hbm, o_ref, buf_ref, sem_ref,
                        *, n_experts, final_norm_fused):
    i = pl.program_id(0)
    tr = MOE_TR

    n_sub = tr // MOE_KSUB
    n_tiles = pl.num_programs(0)
    cur = i % 2

    def sub_copy(tile, e, kb):
        start = starts_ref[tile * n_experts + e] + kb * MOE_KSUB
        return pltpu.make_async_copy(ys_hbm.at[pl.ds(pl.multiple_of(start, MOE_ALIGN), MOE_KSUB), :],
                                     buf_ref.at[tile % 2, e, kb * MOE_KSUB:(kb + 1) * MOE_KSUB, :],
                                     sem_ref.at[(tile % 2) * n_experts * n_sub + e * n_sub + kb])

    def start_tile(tile):
        for e in range(n_experts):
            for kb in range(n_sub):
                @pl.when(kb * MOE_KSUB < seg_ref[tile * n_experts + e])
                def _(e=e, kb=kb):
                    sub_copy(tile, e, kb).start()

    @pl.when(i == 0)
    def _():
        start_tile(i)

    @pl.when(i + 1 < n_tiles)
    def _():
        start_tile(i + 1)

    rec = route_ref[...]
    i1, i2 = rec[:, R_I1:R_I1 + 1], rec[:, R_I2:R_I2 + 1]
    w1, w2 = rec[:, R_W1:R_W1 + 1], rec[:, R_W2:R_W2 + 1]
    lr1, lr2 = rec[:, R_LR1:R_LR1 + 1], rec[:, R_LR2:R_LR2 + 1]
    o_ref[...] = x_ref[...]
    for e in range(n_experts):
        in_e1 = i1 == float(e)
        in_e2 = i2 == float(e)
        lr = jnp.where(in_e1, lr1, jnp.where(in_e2, lr2, -1.0))
        w = jnp.where(in_e1, w1, jnp.where(in_e2, w2, 0.0))
        seg = seg_ref[i * n_experts + e]
        for kb in range(n_sub):
            @pl.when(kb * MOE_KSUB < seg)
            def _(kb=kb, lr=lr, w=w, e=e):
                sub_copy(i, e, kb).wait()
                slot = (lax.broadcasted_iota(I32, (tr, MOE_KSUB), 1) + kb * MOE_KSUB).astype(F32)
                pick = jnp.where(slot == lr, 1.0, 0.0).astype(BF16)
                o_ref[...] += w * _dot(pick, buf_ref[cur, e, kb * MOE_KSUB:(kb + 1) * MOE_KSUB, :])
    if final_norm_fused:
        o_ref[...] = _rms(o_ref[...], gain_ref[...])


def moe_combine(x, route, ys, starts_flat, seg_flat, n_experts, final_gain):
    m, d = x.shape
    tr = MOE_TR
    fused = final_gain is not None
    gain = final_gain if fused else jnp.ones((1, d), F32)
    grid_spec = pltpu.PrefetchScalarGridSpec(
        num_scalar_prefetch=2,
        grid=(m // tr,),
        in_specs=[pl.BlockSpec((tr, d), lambda i, s, g: (i, 0)),
                  pl.BlockSpec((tr, LANES), lambda i, s, g: (i, 0)),
                  pl.BlockSpec((1, d), lambda i, s, g: (0, 0)),
                  pl.BlockSpec(memory_space=pl.ANY)],
        out_specs=pl.BlockSpec((tr, d), lambda i, s, g: (i, 0)),
        scratch_shapes=[pltpu.VMEM((2, n_experts, tr, d), BF16),
                        pltpu.SemaphoreType.DMA((2 * n_experts * (tr // MOE_KSUB),))])
    return pl.pallas_call(
        functools.partial(_moe_combine_kernel, n_experts=n_experts, final_norm_fused=fused),
        out_shape=jax.ShapeDtypeStruct((m, d), F32),
        grid_spec=grid_spec,
        compiler_params=_cparams(("arbitrary",)),
        name="moe_combine",
    )(starts_flat, seg_flat, x, route, gain, ys)


TILE_DEAD, TILE_COMPUTE, TILE_ZERO = 0, 1, 2


def _moe_tile_plan(seg, n_experts, region, n_steps):
    tiles_per_region = region // MOE_TG
    rows = jnp.sum(seg, axis=0)
    n_compute = (rows + MOE_TG - 1) // MOE_TG
    n_live = jnp.minimum((rows + MOE_TR + MOE_TG - 1) // MOE_TG, tiles_per_region)
    ends = jnp.cumsum(n_live)
    total = ends[-1]
    step = jnp.arange(n_steps, dtype=I32)
    t = jnp.minimum(step, total - 1)
    texp = jnp.sum((t[:, None] >= ends[None, :]).astype(I32), axis=1)
    within = t - (ends - n_live)[texp]
    trow = texp * tiles_per_region + within
    kind = jnp.where(step < total, jnp.where(within < n_compute[texp], TILE_COMPUTE, TILE_ZERO), TILE_DEAD)
    return trow.astype(I32), texp.astype(I32), kind.astype(I32)


def moe_ffn(x, g, router_w, w_gu, w_down, final_gain=None):
    m, d = x.shape
    n_experts = router_w.shape[-1]
    region = m + MOE_TG
    n_tiles = m // MOE_TR
    rw = jnp.pad(router_w, ((0, 0), (0, LANES - n_experts)))
    xs, route, starts, seg = moe_route_dispatch(x, g, rw, n_experts)
    starts = starts[:, 0, :n_experts]
    seg = seg[:, 0, :n_experts]
    max_rows = TOP_K * m + n_tiles * n_experts * (MOE_ALIGN - 1)
    n_steps = max_rows // MOE_TG + 2 * n_experts
    trow, texp, tval = _moe_tile_plan(seg, n_experts, region, n_steps)
    ys = moe_grouped_ffn(xs, w_gu, w_down, trow, texp, tval)
    return moe_combine(x, route, ys, starts.reshape(-1), seg.reshape(-1), n_experts, final_gain)


def _final_norm_kernel(x_ref, g_ref, o_ref):
    o_ref[...] = _rms(x_ref[...], g_ref[...])


def final_norm(x, g):
    m, d = x.shape
    tm = min(m, 1024)
    return pl.pallas_call(
        _final_norm_kernel,
        out_shape=jax.ShapeDtypeStruct((m, d), F32),
        grid=(m // tm,),
        in_specs=[pl.BlockSpec((tm, d), lambda i: (i, 0)), pl.BlockSpec((1, d), lambda i: (0, 0))],
        out_specs=pl.BlockSpec((tm, d), lambda i: (i, 0)),
        compiler_params=_cparams(("parallel",)),
        name="final_norm",
    )(x, g)


def _column_plan(d_model):
    sizes = (SWA_HEADS * HEAD_DIM, SWA_KV_HEADS * HEAD_DIM, SWA_KV_HEADS * HEAD_DIM,
             DSA_HEADS * HEAD_DIM, DSA_KV_HEADS * HEAD_DIM, DSA_KV_HEADS * HEAD_DIM,
             IDX_HEADS * IDX_DIM, IDX_DIM, IDX_HEADS,
             RET_HEADS * RET_KEY_DIM, RET_HEADS * RET_KEY_DIM, RET_HEADS * RET_VAL_DIM, RET_HEADS * RET_VAL_DIM,
             len(POOL_WINDOWS) * POOL_GROUP_DIM, N_BRANCH * d_model)
    starts = np.cumsum((0,) + sizes[:-1])
    (a_q, a_k, a_v, b_q, b_k, b_v, i_q, i_k, i_w, r_q, r_k, r_v, r_g, p_u, gates) = [
        (np.arange(s, s + n), np.ones(n)) for s, n in zip(starts, sizes)]

    def scaled(c, f):
        return c[0], c[1] * f

    def dup_heads(c):
        return tuple(np.repeat(z.reshape(-1, 1, HEAD_DIM), 2, axis=1).reshape(-1) for z in c)

    def rotate_half(c):
        idx, fac = (z.reshape(-1, 2, HEAD_DIM // 2) for z in c)
        return (np.stack([idx[:, 1], idx[:, 0]], axis=1).reshape(-1),
                np.stack([-fac[:, 1], fac[:, 0]], axis=1).reshape(-1))

    def table(cols, width):
        idx = np.concatenate([c[0] for c in cols])
        fac = np.concatenate([c[1] for c in cols])
        pad = width - idx.size
        return (np.concatenate([idx, np.zeros(pad, idx.dtype)]).astype(np.int32),
                np.concatenate([fac, np.zeros(pad)]).astype(np.float32))

    qs = HEAD_DIM ** -0.5
    rope_cols = [scaled(a_q, qs), scaled(b_q, qs), scaled(i_q, IDX_DIM ** -0.5), r_q,
                 scaled(r_k, RET_KEY_DIM ** -0.5), dup_heads(a_k), dup_heads(b_k), dup_heads(i_k)]
    plain_cols = [gates, r_v, r_g, p_u, dup_heads(a_v), b_v, scaled(i_w, IDX_HEADS ** -0.5)]
    return (table(rope_cols, PA_COLS), table([rotate_half(c) for c in rope_cols], PA_COLS),
            table(plain_cols, PB_COLS))


def _layout_w_in(w_in, plan):
    return tuple((jnp.take(w_in, idx, axis=1) * fac[None, :]).astype(BF16) for idx, fac in plan)


def kernel(x, mem, positions, norm_mix, w_in, attn_sink, w_branch, w_out, pool_w, pool_scale, norm_cross, norm_mem, w_xq, w_xkv, w_xo, norm_ffn, dense_w_gu, dense_w_down, router_w, moe_w_gu, moe_w_down, norm_final):
    batch, seq, d = x.shape
    depth = norm_mix.shape[0]
    m = batch * seq
    mem_len = mem.shape[1]
    xs = x.reshape(m, d)
    mem2 = mem.reshape(batch * mem_len, d)

    inv = 1.0 / (ROPE_THETA ** (jnp.arange(0, HEAD_DIM, 2, dtype=F32) / HEAD_DIM))
    inv128 = jnp.tile(inv, LANES // (HEAD_DIM // 2))[None, :]
    cos, sin = rope_tables(positions.reshape(m, 1), inv128)
    log_gamma = jnp.log1p(-(2.0 ** (-5.0 - jnp.arange(RET_HEADS, dtype=F32))))
    x_scale = (d // X_HEADS) ** -0.5
    plan = _column_plan(d)

    for layer in range(depth):
        w_x, w_rot, w_plain = _layout_w_in(w_in[layer], plan)
        pa = norm_matmul_rope(xs, norm_mix[layer][None, :], w_x, w_rot, cos, sin, tm=1024, tn=PA_TN)
        pb = norm_matmul(xs, norm_mix[layer][None, :], w_plain, tm=1024, tn=PB_TN, name="in_proj_plain")
        o_a = swa_attention(pa, pb, attn_sink[layer], batch, seq)
        o_b = dsa_attention(pa, pb, batch, seq)
        o_c = retention(pa, pb, log_gamma, batch, seq)
        o_d = multiscale_pool(pb, pool_w[layer].astype(BF16), pool_scale[layer][None, :], batch, seq)
        xs = merge_branches(xs, (o_a, o_b, o_c, o_d), pb, w_branch[layer].astype(BF16), w_out[layer].astype(BF16))

        kv = norm_matmul(mem2, norm_mem[layer][None, :], w_xkv[layer].astype(BF16),
                         tm=batch * mem_len, tn=512, name="mem_kv_proj")
        xs = cross_attention(xs, norm_cross[layer][None, :], (w_xq[layer] * x_scale).astype(BF16), kv,
                             w_xo[layer].astype(BF16), batch, seq)

        li = layer // 2
        if layer % 2 == 0:
            xs = ffn_dense(xs, norm_ffn[layer][None, :], dense_w_gu[li].astype(BF16), dense_w_down[li].astype(BF16))
        else:
            last = layer == depth - 1
            xs = moe_ffn(xs, norm_ffn[layer][None, :], router_w[li], moe_w_gu[li].astype(BF16),
                         moe_w_down[li].astype(BF16), final_gain=norm_final[None, :] if last else None)

    if depth % 2 == 1:
        xs = final_norm(xs, norm_final[None, :])
    return xs.reshape(batch, seq, d)
```

```python
import functools

import numpy as np
import jax
import jax.numpy as jnp
from jax import lax
from jax.experimental import pallas as pl
from jax.experimental.pallas import tpu as pltpu

F32 = jnp.float32
BF16 = jnp.bfloat16
I32 = jnp.int32

EPS = 1e-6
HEAD_DIM = 64
ROPE_THETA = 10000.0
BLOCK = 128
SWA_HEADS, SWA_KV_HEADS = 8, 2
DSA_HEADS, DSA_KV_HEADS = 8, 2
IDX_HEADS, IDX_DIM = 4, 64
DSA_TOPK = 256
RET_HEADS, RET_KEY_DIM, RET_VAL_DIM = 4, 64, 128
RET_CHUNK = 256
POOL_WINDOWS = (2, 4, 8, 16)
POOL_GROUP_DIM = 128
N_BRANCH = 4
BRANCH_WIDTH = 512
X_HEADS = 4
TOP_K = 2

LANES = 128
VMEM_LIMIT = 56 * 1024 * 1024
NEG_BIG = -1e30
LOG2_E = 1.4426950408889634
INT_MIN = -2 ** 31

PA_AQ, PA_BQ, PA_IQ, PA_RQ, PA_RK, PA_AK, PA_BK, PA_IK = 0, 512, 1024, 1280, 1536, 1792, 2048, 2304
PA_COLS, PA_TN = 2560, 1280
PB_GATES, PB_RV, PB_RG, PB_PU, PB_AV, PB_BV, PB_IW = 0, 4096, 4608, 5120, 5632, 5888, 6016
PB_COLS, PB_TN = 6144, 1536


def _cparams(sem):
    return pltpu.CompilerParams(dimension_semantics=sem, vmem_limit_bytes=VMEM_LIMIT)


def _dot(a, b):
    return jnp.dot(a, b, preferred_element_type=F32)


def _dot_nt(a, b):
    return lax.dot_general(a, b, (((1,), (1,)), ((), ())), preferred_element_type=F32)


def _rms(x, g):
    return x * lax.rsqrt(jnp.mean(x * x, axis=-1, keepdims=True) + EPS) * g


def _half_mask(hh):
    lane = lax.broadcasted_iota(I32, (1, LANES), 1)
    return (lane >= HEAD_DIM) if hh else (lane < HEAD_DIM)


def _keep_half(block, hh):
    return jnp.where(_half_mask(hh), block, jnp.zeros_like(block))


def _rope_kernel(pos_ref, inv_ref, cos_ref, sin_ref):
    ang = pos_ref[...].astype(F32) * inv_ref[...]
    cos_ref[...] = jnp.cos(ang)
    sin_ref[...] = jnp.sin(ang)


def rope_tables(pos_col, inv128):
    m = pos_col.shape[0]
    tm = min(m, 2048)
    return pl.pallas_call(
        _rope_kernel,
        out_shape=(jax.ShapeDtypeStruct((m, LANES), F32), jax.ShapeDtypeStruct((m, LANES), F32)),
        grid=(m // tm,),
        in_specs=[pl.BlockSpec((tm, 1), lambda i: (i, 0)), pl.BlockSpec((1, LANES), lambda i: (0, 0))],
        out_specs=(pl.BlockSpec((tm, LANES), lambda i: (i, 0)), pl.BlockSpec((tm, LANES), lambda i: (i, 0))),
        compiler_params=_cparams(("parallel",)),
        name="rope_tables",
    )(pos_col, inv128)


def _norm_matmul_kernel(x_ref, g_ref, w_ref, o_ref, h_ref):
    @pl.when(pl.program_id(1) == 0)
    def _():
        h_ref[...] = _rms(x_ref[...], g_ref[...]).astype(BF16)

    o_ref[...] = _dot(h_ref[...], w_ref[...]).astype(o_ref.dtype)


def norm_matmul(x, g, w, tm, tn, out_dtype=BF16, name="norm_matmul"):
    m, k = x.shape
    n = w.shape[1]
    tm = min(tm, m)
    return pl.pallas_call(
        _norm_matmul_kernel,
        out_shape=jax.ShapeDtypeStruct((m, n), out_dtype),
        grid=(m // tm, n // tn),
        in_specs=[pl.BlockSpec((tm, k), lambda i, j: (i, 0)),
                  pl.BlockSpec((1, k), lambda i, j: (0, 0)),
                  pl.BlockSpec((k, tn), lambda i, j: (0, j))],
        out_specs=pl.BlockSpec((tm, tn), lambda i, j: (i, j)),
        scratch_shapes=[pltpu.VMEM((tm, k), BF16)],
        compiler_params=_cparams(("parallel", "arbitrary")),
        name=name,
    )(x, g, w)


def _norm_matmul_rope_kernel(x_ref, g_ref, wx_ref, wr_ref, cos_ref, sin_ref, o_ref, h_ref):
    @pl.when(pl.program_id(1) == 0)
    def _():
        h_ref[...] = _rms(x_ref[...], g_ref[...]).astype(BF16)

    h = h_ref[...]
    proj = _dot(h, wx_ref[...])
    rot = _dot(h, wr_ref[...])
    cos = cos_ref[...]
    sin = sin_ref[...]
    for c in range(o_ref.shape[1] // LANES):
        cols = slice(c * LANES, (c + 1) * LANES)
        o_ref[:, cols] = (proj[:, cols] * cos + rot[:, cols] * sin).astype(o_ref.dtype)


def norm_matmul_rope(x, g, wx, wr, cos, sin, tm, tn):
    m, k = x.shape
    n = wx.shape[1]
    tm = min(tm, m)
    return pl.pallas_call(
        _norm_matmul_rope_kernel,
        out_shape=jax.ShapeDtypeStruct((m, n), BF16),
        grid=(m // tm, n // tn),
        in_specs=[pl.BlockSpec((tm, k), lambda i, j: (i, 0)),
                  pl.BlockSpec((1, k), lambda i, j: (0, 0)),
                  pl.BlockSpec((k, tn), lambda i, j: (0, j)),
                  pl.BlockSpec((k, tn), lambda i, j: (0, j)),
                  pl.BlockSpec((tm, LANES), lambda i, j: (i, 0)),
                  pl.BlockSpec((tm, LANES), lambda i, j: (i, 0))],
        out_specs=pl.BlockSpec((tm, tn), lambda i, j: (i, j)),
        scratch_shapes=[pltpu.VMEM((tm, k), BF16)],
        compiler_params=_cparams(("parallel", "arbitrary")),
        name="in_proj_rope",
    )(x, g, wx, wr, cos, sin)


def _swa_kernel(sink_ref, q_ref, kc_ref, kp_ref, vc_ref, vp_ref, o_ref):
    i = pl.program_id(1)
    per_group = SWA_HEADS // SWA_KV_HEADS
    q_t = (q_ref[...].astype(F32) * LOG2_E).T.astype(BF16)
    key = lax.broadcasted_iota(I32, (2 * BLOCK, 1), 0)
    rel = lax.broadcasted_iota(I32, (1, BLOCK), 1) + BLOCK - key
    band = jnp.logical_and(jnp.logical_and(rel >= 0, rel < BLOCK), jnp.logical_or(key >= BLOCK, i > 0))
    bias = jnp.where(band, 0.0, -jnp.inf).astype(F32)
    bias = jnp.concatenate([bias] * per_group, axis=1)
    outs = []
    for g in range(SWA_KV_HEADS):
        cols = slice(g * LANES, (g + 1) * LANES)
        k_win = jnp.concatenate([kp_ref[:, cols], kc_ref[:, cols]], axis=0)[:, 0:HEAD_DIM]
        v_t = jnp.concatenate([vp_ref[:, cols], vc_ref[:, cols]], axis=0).astype(F32).T[0:HEAD_DIM].astype(BF16)
        heads = range(g * per_group, (g + 1) * per_group)
        q_g = jnp.concatenate([q_t[h * HEAD_DIM:(h + 1) * HEAD_DIM, :] for h in heads], axis=1)
        sink = jnp.concatenate([jnp.full((1, BLOCK), sink_ref[h] * LOG2_E, F32) for h in heads], axis=1)
        s = _dot(k_win, q_g) + bias
        m = jnp.maximum(jnp.max(s, axis=0, keepdims=True), sink)
        p = jnp.exp2(s - m)
        den = jnp.sum(p, axis=0, keepdims=True) + jnp.exp2(sink - m)
        o_t = _dot(v_t, p.astype(BF16)) / den
        outs += [o_t[:, a * BLOCK:(a + 1) * BLOCK] for a in range(per_group)]
    o_ref[...] = jnp.concatenate(outs, axis=0).T.astype(o_ref.dtype)


def swa_attention(pa, pb, sink, batch, seq):
    nb = seq // BLOCK
    cur = lambda col: (lambda b, i: (b * nb + i, col))
    prev = lambda col: (lambda b, i: (b * nb + jnp.maximum(i - 1, 0), col))
    return pl.pallas_call(
        _swa_kernel,
        out_shape=jax.ShapeDtypeStruct((batch * seq, BRANCH_WIDTH), BF16),
        grid=(batch, nb),
        in_specs=[pl.BlockSpec(memory_space=pltpu.SMEM),
                  pl.BlockSpec((BLOCK, 512), cur(PA_AQ // 512)),
                  pl.BlockSpec((BLOCK, 256), cur(PA_AK // 256)),
                  pl.BlockSpec((BLOCK, 256), prev(PA_AK // 256)),
                  pl.BlockSpec((BLOCK, 256), cur(PB_AV // 256)),
                  pl.BlockSpec((BLOCK, 256), prev(PB_AV // 256))],
        out_specs=pl.BlockSpec((BLOCK, BRANCH_WIDTH), cur(0)),
        compiler_params=_cparams(("parallel", "arbitrary")),
        name="swa_attention",
    )(sink, pa, pa, pa, pb, pb)


DSA_TQ = 256
DSA_ONES = 16


def _dsa_kernel(q_ref, iq_ref, iw_ref, k_ref, ik_ref, v_ref, o_ref,
                skey_ref, vt_ref, qt_ref, cut_ref, m_ref, l_ref, acc_ref, alpha_ref, bias_ref, s_ref, p_ref, odd_ref,
                *, n_sel, seq):
    tq = DSA_TQ
    nq = seq // tq
    i = pl.program_id(1)
    n_chunks = i + 1
    q_pos = i * tq + lax.broadcasted_iota(I32, (1, tq), 1)
    key_off = lax.broadcasted_iota(I32, (tq, 1), 0)

    @pl.when(i == 0)
    def _():
        for jj in range(nq):
            vt = v_ref[jj * tq:(jj + 1) * tq, :].astype(F32).T
            ones = jnp.ones((DSA_ONES, tq), BF16)
            for g in range(DSA_KV_HEADS):
                vt_ref[jj, g, 0:HEAD_DIM, :] = vt[g * HEAD_DIM:(g + 1) * HEAD_DIM].astype(BF16)
                vt_ref[jj, g, HEAD_DIM:, :] = ones

    iq_t = iq_ref[...].astype(F32).T.astype(BF16)
    w_t = iw_ref[...].astype(F32).T

    def chunk_keys(j):
        ikc = ik_ref[pl.ds(pl.multiple_of(j * tq, tq), tq), 0:IDX_DIM]
        sc = jnp.zeros((tq, tq), F32)
        for h in range(IDX_HEADS):
            sc = sc + jnp.maximum(_dot(ikc, iq_t[h * IDX_DIM:(h + 1) * IDX_DIM, :]), 0.0) * w_t[h:h + 1, :]
        sc = jnp.where(sc == 0.0, 0.0, sc)
        bits = lax.bitcast_convert_type(sc, I32)
        return jnp.where(bits < 0, bits ^ jnp.int32(0x7FFFFFFF), bits)

    def score_chunk(j, carry):
        skey_ref[j] = chunk_keys(j)
        return carry

    lax.fori_loop(0, i, score_chunk, 0)
    skey_ref[i] = jnp.where((i * tq + key_off) <= q_pos, chunk_keys(i), jnp.int32(INT_MIN))

    view = (tq // 8, 8, tq)
    pos_in_chunk = lax.broadcasted_iota(I32, view, 0) * 8 + lax.broadcasted_iota(I32, view, 1)

    def all_sublanes(x, op):
        for sh in (4, 2, 1):
            x = op(x, pltpu.roll(x, sh, axis=0))
        return x

    def count_keys(pred):
        def chunk_count(j):
            x = pred(skey_ref[j].reshape(view), j)
            part = view[0] // 4
            sums = [jnp.sum(x[a * part:(a + 1) * part], axis=0) for a in range(4)]
            return (sums[0] + sums[1]) + (sums[2] + sums[3])

        def quad(jq, acc):
            j0 = 4 * jq
            return acc + ((chunk_count(j0) + chunk_count(j0 + 1)) + (chunk_count(j0 + 2) + chunk_count(j0 + 3)))

        n_quads = n_chunks // 4
        rest = n_chunks - 4 * n_quads
        odd_ref[...] = jnp.zeros((8, tq), F32)

        @pl.when(rest >= 2)
        def _():
            odd_ref[...] = chunk_count(4 * n_quads) + chunk_count(4 * n_quads + 1)

        @pl.when(rest % 2 == 1)
        def _():
            odd_ref[...] += chunk_count(n_chunks - 1)

        acc = lax.fori_loop(0, n_quads, quad, jnp.zeros((8, tq), F32))
        return all_sublanes(acc + odd_ref[...], jnp.add)

    one = jnp.float32(1.0)
    zero = jnp.float32(0.0)
    k_f = jnp.float32(n_sel)
    cut_bits = max((seq - 1).bit_length(), 1)

    def bit_step(b, carry):
        res_u, c_ge = carry
        cand_u = res_u | lax.shift_left(jnp.int32(1), 31 - b)
        cand_s = cand_u ^ jnp.int32(INT_MIN)
        cnt = count_keys(lambda kc, j: jnp.where(kc >= cand_s, one, zero))
        take = cnt >= k_f
        return jnp.where(take, cand_u, res_u), jnp.where(take, cnt, c_ge)

    res_u, c_ge = lax.fori_loop(0, 32, bit_step, (jnp.zeros((8, tq), I32), jnp.zeros((8, tq), F32)))
    thr = res_u ^ jnp.int32(INT_MIN)
    short = thr == jnp.int32(INT_MIN)
    cut_ref[...] = jnp.where(short, jnp.int32(-1), jnp.int32(seq))
    unresolved = jnp.where(jnp.logical_and(c_ge != k_f, jnp.logical_not(short)), one, zero)

    @pl.when(jnp.max(unresolved) > 0.0)
    def _():
        need = k_f - count_keys(lambda kc, j: jnp.where(kc > thr, one, zero))

        def cut_step(b, ans):
            cand = ans | lax.shift_left(jnp.int32(1), cut_bits - 1 - b)
            lim = cand - 1

            def pred(kc, j):
                return jnp.where(kc == thr, jnp.where((j * tq + pos_in_chunk) <= lim, one, zero), zero)

            return jnp.where(count_keys(pred) < need, cand, ans)

        ans = lax.fori_loop(0, cut_bits, cut_step, jnp.zeros((8, tq), I32))
        cut_ref[...] = jnp.where(short, jnp.int32(-1), ans)

    qt_ref[...] = (q_ref[...].astype(F32) * LOG2_E).T.astype(BF16)
    m_ref[...] = jnp.full(m_ref.shape, NEG_BIG, F32)
    l_ref[...] = jnp.zeros(l_ref.shape, F32)
    acc_ref[...] = jnp.zeros(acc_ref.shape, F32)
    cut = cut_ref[...]
    neg = jnp.float32(NEG_BIG)
    group = DSA_HEADS // DSA_KV_HEADS

    def attend(j, carry):
        kc = skey_ref[j].reshape(view)
        start = pl.multiple_of(j * tq, tq)
        tie = jnp.where(kc == thr, jnp.where((j * tq + pos_in_chunk) <= cut, zero, neg), neg)
        bias_ref[...] = jnp.where(kc > thr, zero, tie).reshape(tq, tq)
        for h in range(DSA_HEADS):
            g = h // group
            kblk = k_ref[pl.ds(start, tq), g * LANES:g * LANES + HEAD_DIM]
            s = _dot(kblk, qt_ref[h * HEAD_DIM:(h + 1) * HEAD_DIM, :]) + bias_ref[...]
            s_ref[h] = s
            m_old = m_ref[h]
            m_new = jnp.maximum(m_old, jnp.max(s, axis=0, keepdims=True))
            alpha_ref[h] = jnp.exp2(m_old - m_new)
            m_ref[h] = m_new
        for h in range(DSA_HEADS):
            p_ref[h] = jnp.exp2(s_ref[h] - m_ref[h]).astype(BF16)
        for h in range(DSA_HEADS):
            pv = _dot(vt_ref[j, h // group], p_ref[h])
            acc_ref[h] = alpha_ref[h] * acc_ref[h] + pv[0:HEAD_DIM]
            l_ref[h] = alpha_ref[h] * l_ref[h] + pv[HEAD_DIM:HEAD_DIM + 1]
        return carry

    lax.fori_loop(0, n_chunks, attend, 0)

    for p in range(DSA_HEADS // 2):
        o_t = jnp.concatenate([acc_ref[2 * p] / l_ref[2 * p], acc_ref[2 * p + 1] / l_ref[2 * p + 1]], axis=0)
        o_ref[:, p * LANES:(p + 1) * LANES] = o_t.T.astype(o_ref.dtype)


def dsa_attention(pa, pb, batch, seq):
    tq = DSA_TQ
    nq = seq // tq
    n_sel = min(DSA_TOPK, seq // 4)
    qmap = lambda col: (lambda b, i: (b * nq + i, col))
    smap = lambda col: (lambda b, i: (b, col))
    return pl.pallas_call(
        functools.partial(_dsa_kernel, n_sel=n_sel, seq=seq),
        out_shape=jax.ShapeDtypeStruct((batch * seq, BRANCH_WIDTH), BF16),
        grid=(batch, nq),
        in_specs=[pl.BlockSpec((tq, 512), qmap(PA_BQ // 512)),
                  pl.BlockSpec((tq, 256), qmap(PA_IQ // 256)),
                  pl.BlockSpec((tq, LANES), qmap(PB_IW // LANES)),
                  pl.BlockSpec((seq, 256), smap(PA_BK // 256)),
                  pl.BlockSpec((seq, LANES), smap(PA_IK // LANES)),
                  pl.BlockSpec((seq, LANES), smap(PB_BV // LANES))],
        out_specs=pl.BlockSpec((tq, BRANCH_WIDTH), qmap(0)),
        scratch_shapes=[pltpu.VMEM((nq, tq, tq), I32),
                        pltpu.VMEM((nq, DSA_KV_HEADS, HEAD_DIM + DSA_ONES, tq), BF16),
                        pltpu.VMEM((DSA_HEADS * HEAD_DIM, tq), BF16),
                        pltpu.VMEM((8, tq), I32),
                        pltpu.VMEM((DSA_HEADS, 1, tq), F32),
                        pltpu.VMEM((DSA_HEADS, 1, tq), F32),
                        pltpu.VMEM((DSA_HEADS, HEAD_DIM, tq), F32),
                        pltpu.VMEM((DSA_HEADS, 1, tq), F32),
                        pltpu.VMEM((tq, tq), F32),
                        pltpu.VMEM((DSA_HEADS, tq, tq), F32),
                        pltpu.VMEM((DSA_HEADS, tq, tq), BF16),
                        pltpu.VMEM((8, tq), F32)],
        compiler_params=_cparams(("parallel", "arbitrary")),
        name="dsa_attention",
    )(pa, pa, pb, pa, pa, pb)


def _retention_kernel(lg_ref, q_ref, k_ref, v_ref, g_ref, o_ref, state_ref):
    cdim = RET_CHUNK

    @pl.when(pl.program_id(1) == 0)
    def _():
        state_ref[...] = jnp.zeros(state_ref.shape, F32)

    n = lax.broadcasted_iota(I32, (cdim, 1), 0).astype(F32)
    mcol = lax.broadcasted_iota(I32, (1, cdim), 1).astype(F32)
    rel = n - mcol
    for h in range(RET_HEADS):
        lg = lg_ref[h]
        decay = jnp.where(rel >= 0, jnp.exp(lg * jnp.maximum(rel, 0.0)), 0.0)
        q_decay = jnp.exp(lg * (n + 1.0))
        k_decay = jnp.exp(lg * (cdim - 1.0 - n))
        chunk_decay = jnp.exp(jnp.full((1, LANES), lg * cdim, F32))
        blk = slice((h // 2) * LANES, (h // 2 + 1) * LANES)
        qm = _keep_half(q_ref[:, blk], h % 2)
        km = _keep_half(k_ref[:, blk], h % 2)
        v = v_ref[:, h * RET_VAL_DIM:(h + 1) * RET_VAL_DIM]
        state = state_ref[h]
        inner = _dot((_dot_nt(qm, km) * decay).astype(BF16), v)
        cross = _dot(qm, state.astype(BF16)) * q_decay
        kd_t = (km.astype(F32) * k_decay).T.astype(BF16)
        state_ref[h] = state * chunk_decay + _dot(kd_t, v)
        out = inner + cross
        out = out * lax.rsqrt(jnp.mean(out * out, axis=-1, keepdims=True) + EPS)
        gate = g_ref[:, h * RET_VAL_DIM:(h + 1) * RET_VAL_DIM].astype(F32)
        silu = gate * (1.0 / (1.0 + jnp.exp(-gate)))
        o_ref[:, h * RET_VAL_DIM:(h + 1) * RET_VAL_DIM] = (silu * out).astype(o_ref.dtype)


def retention(pa, pb, log_gamma, batch, seq):
    nc = seq // RET_CHUNK
    cmap = lambda col: (lambda b, c: (b * nc + c, col))
    return pl.pallas_call(
        _retention_kernel,
        out_shape=jax.ShapeDtypeStruct((batch * seq, BRANCH_WIDTH), BF16),
        grid=(batch, nc),
        in_specs=[pl.BlockSpec(memory_space=pltpu.SMEM),
                  pl.BlockSpec((RET_CHUNK, 256), cmap(PA_RQ // 256)),
                  pl.BlockSpec((RET_CHUNK, 256), cmap(PA_RK // 256)),
                  pl.BlockSpec((RET_CHUNK, 512), cmap(PB_RV // 512)),
                  pl.BlockSpec((RET_CHUNK, 512), cmap(PB_RG // 512))],
        out_specs=pl.BlockSpec((RET_CHUNK, BRANCH_WIDTH), cmap(0)),
        scratch_shapes=[pltpu.VMEM((RET_HEADS, LANES, RET_VAL_DIM), F32)],
        compiler_params=_cparams(("parallel", "arbitrary")),
        name="retention",
    )(log_gamma, pa, pa, pb, pb)


POOL_HALO = 16


def _pool_kernel(cur_ref, prev_ref, w_ref, scale_ref, o_ref, *, tiles_per_seq):
    tm = cur_ref.shape[0]
    it = pl.program_id(0) % tiles_per_seq
    cur = cur_ref[...].astype(F32)
    prev = jnp.where(it > 0, prev_ref[...].astype(F32), 0.0)
    ext = jnp.concatenate([prev, cur], axis=0)
    t = it * tm + lax.broadcasted_iota(I32, (tm, 1), 0)
    for gi, win in enumerate(POOL_WINDOWS):
        cols = slice(gi * POOL_GROUP_DIM, (gi + 1) * POOL_GROUP_DIM)
        s = ext[:, cols]
        sh = 1
        while sh < win:
            s = s + pltpu.roll(s, sh, axis=0)
            sh *= 2
        cnt = jnp.minimum(t + 1, win).astype(F32)
        pooled = s[POOL_HALO:, :] / cnt
        y = _dot((pooled - cur[:, cols]).astype(BF16), w_ref[gi])
        o_ref[:, cols] = (y * scale_ref[:, cols]).astype(o_ref.dtype)


def multiscale_pool(pb, pool_w, pool_scale, batch, seq):
    m = batch * seq
    tm = min(seq, 512)
    tps = seq // tm
    ratio = tm // POOL_HALO
    return pl.pallas_call(
        functools.partial(_pool_kernel, tiles_per_seq=tps),
        out_shape=jax.ShapeDtypeStruct((m, BRANCH_WIDTH), BF16),
        grid=(m // tm,),
        in_specs=[pl.BlockSpec((tm, 512), lambda i: (i, PB_PU // 512)),
                  pl.BlockSpec((POOL_HALO, 512), lambda i: (jnp.maximum(i * ratio - 1, 0), PB_PU // 512)),
                  pl.BlockSpec((len(POOL_WINDOWS), POOL_GROUP_DIM, POOL_GROUP_DIM), lambda i: (0, 0, 0)),
                  pl.BlockSpec((1, 512), lambda i: (0, 0))],
        out_specs=pl.BlockSpec((tm, BRANCH_WIDTH), lambda i: (i, 0)),
        compiler_params=_cparams(("parallel",)),
        name="multiscale_pool",
    )(pb, pb, pool_w, pool_scale)


def _merge_kernel(x_ref, oa_ref, ob_ref, oc_ref, od_ref, gates_ref, wb_ref, wo_ref, o_ref):
    d = x_ref.shape[1]
    merged = jnp.zeros(x_ref.shape, F32)
    for bi, br in enumerate((oa_ref, ob_ref, oc_ref, od_ref)):
        gate = gates_ref[:, bi * d:(bi + 1) * d].astype(F32)
        merged = merged + (1.0 / (1.0 + jnp.exp(-gate))) * _dot(br[...], wb_ref[bi])
    o_ref[...] = x_ref[...] + _dot(merged.astype(BF16), wo_ref[...])


def merge_branches(x, branches, pb, w_branch, w_out):
    m, d = x.shape
    tm = min(m, 512)
    row = lambda i: (i, 0)
    return pl.pallas_call(
        _merge_kernel,
        out_shape=jax.ShapeDtypeStruct((m, d), F32),
        grid=(m // tm,),
        in_specs=[pl.BlockSpec((tm, d), row)]
        + [pl.BlockSpec((tm, BRANCH_WIDTH), row)] * N_BRANCH
        + [pl.BlockSpec((tm, N_BRANCH * d), lambda i: (i, PB_GATES // (N_BRANCH * d))),
           pl.BlockSpec((N_BRANCH, BRANCH_WIDTH, d), lambda i: (0, 0, 0)),
           pl.BlockSpec((d, d), lambda i: (0, 0))],
        out_specs=pl.BlockSpec((tm, d), row),
        compiler_params=_cparams(("parallel",)),
        name="merge_branches",
    )(x, *branches, pb, w_branch, w_out)


def _cross_kernel(x_ref, g_ref, wq_ref, k_ref, v_ref, wo_ref, o_ref):
    x = x_ref[...]
    d = x.shape[1]
    hd = d // X_HEADS
    q = _dot(_rms(x, g_ref[...]).astype(BF16), wq_ref[...]).astype(BF16)
    outs = []
    for h in range(X_HEADS):
        cols = slice(h * hd, (h + 1) * hd)
        s = _dot_nt(q[:, cols], k_ref[:, cols])
        p = jnp.exp(s - jnp.max(s, axis=1, keepdims=True))
        o = _dot(p.astype(BF16), v_ref[:, cols]) / jnp.sum(p, axis=1, keepdims=True)
        outs.append(o.astype(BF16))
    o_ref[...] = x + _dot(jnp.concatenate(outs, axis=1), wo_ref[...])


def cross_attention(x, g, wq, kv, wo, batch, seq):
    m, d = x.shape
    mem_len = kv.shape[0] // batch
    tm = min(seq, 512)
    nt = seq // tm
    return pl.pallas_call(
        _cross_kernel,
        out_shape=jax.ShapeDtypeStruct((m, d), F32),
        grid=(batch, nt),
        in_specs=[pl.BlockSpec((tm, d), lambda b, i: (b * nt + i, 0)),
                  pl.BlockSpec((1, d), lambda b, i: (0, 0)),
                  pl.BlockSpec((d, d), lambda b, i: (0, 0)),
                  pl.BlockSpec((mem_len, d), lambda b, i: (b, 0)),
                  pl.BlockSpec((mem_len, d), lambda b, i: (b, 1)),
                  pl.BlockSpec((d, d), lambda b, i: (0, 0))],
        out_specs=pl.BlockSpec((tm, d), lambda b, i: (b * nt + i, 0)),
        compiler_params=_cparams(("parallel", "arbitrary")),
        name="cross_attention",
    )(x, g, wq, kv, kv, wo)


FFN_TF = 512


def _swiglu(h, wg, wu):
    a = _dot(h, wg)
    return a * (1.0 / (1.0 + jnp.exp(-a))) * _dot(h, wu)


def _ffn_kernel(x_ref, g_ref, wg_ref, wu_ref, wd_ref, o_ref, h_ref, acc_ref):
    f = pl.program_id(1)

    @pl.when(f == 0)
    def _():
        x = x_ref[...]
        h_ref[...] = _rms(x, g_ref[...]).astype(BF16)
        acc_ref[...] = x

    acc_ref[...] += _dot(_swiglu(h_ref[...], wg_ref[...], wu_ref[...]).astype(BF16), wd_ref[...])

    @pl.when(f == pl.num_programs(1) - 1)
    def _():
        o_ref[...] = acc_ref[...]


def ffn_dense(x, g, w_gu, w_down):
    m, d = x.shape
    dff = w_down.shape[0]
    tm = min(m, 1024)
    nf = dff // FFN_TF
    return pl.pallas_call(
        _ffn_kernel,
        out_shape=jax.ShapeDtypeStruct((m, d), F32),
        grid=(m // tm, nf),
        in_specs=[pl.BlockSpec((tm, d), lambda i, f: (i, 0)),
                  pl.BlockSpec((1, d), lambda i, f: (0, 0)),
                  pl.BlockSpec((d, FFN_TF), lambda i, f: (0, f)),
                  pl.BlockSpec((d, FFN_TF), lambda i, f: (0, nf + f)),
                  pl.BlockSpec((FFN_TF, d), lambda i, f: (f, 0))],
        out_specs=pl.BlockSpec((tm, d), lambda i, f: (i, 0)),
        scratch_shapes=[pltpu.VMEM((tm, d), BF16), pltpu.VMEM((tm, d), F32)],
        compiler_params=_cparams(("parallel", "arbitrary")),
        name="ffn_dense",
    )(x, g, w_gu, w_gu, w_down)


MOE_TR = 512
MOE_TG = 1024
MOE_ALIGN = 16
MOE_SUB = 128
MOE_KSUB = 256
R_I1, R_I2, R_W1, R_W2, R_LR1, R_LR2 = range(6)


def _route_dispatch_kernel(x_ref, g_ref, rw_ref, xs_hbm, route_ref, starts_ref, seg_ref,
                           buf_ref, sem_ref, cnt_ref, *, n_experts, region):
    i = pl.program_id(0)
    tr = MOE_TR

    @pl.when(i == 0)
    def _():
        for e in range(n_experts):
            cnt_ref[e] = 0

    hn = _rms(x_ref[...], g_ref[...])
    h = hn.astype(BF16)
    h_rem = (hn - h.astype(F32)).astype(BF16)
    rw = rw_ref[...]
    rw_head = rw.astype(BF16)
    rw_rem = (rw - rw_head.astype(F32)).astype(BF16)
    logits = (_dot(h, rw_head) + _dot(h_rem, rw_head)) + _dot(h, rw_rem)
    lane = lax.broadcasted_iota(I32, (1, LANES), 1).astype(F32)
    ninf = jnp.float32(-jnp.inf)
    lg = jnp.where(lane < n_experts, logits, ninf)
    m1 = jnp.max(lg, axis=1, keepdims=True)
    i1 = jnp.min(jnp.where(lg == m1, lane, float(LANES)), axis=1, keepdims=True)
    lg2 = jnp.where(lane == i1, ninf, lg)
    m2 = jnp.max(lg2, axis=1, keepdims=True)
    i2 = jnp.min(jnp.where(lg2 == m2, lane, float(LANES)), axis=1, keepdims=True)
    e2 = jnp.exp(m2 - m1)
    den = 1.0 + e2
    oh1 = lane == i1
    oh2 = lane == i2
    both = jnp.where(jnp.logical_or(oh1, oh2), 1.0, 0.0)
    tri = (lax.broadcasted_iota(I32, (tr, tr), 0) > lax.broadcasted_iota(I32, (tr, tr), 1))
    prefix = _dot(jnp.where(tri, 1.0, 0.0).astype(BF16), both.astype(BF16))
    lr1 = jnp.sum(jnp.where(oh1, prefix, 0.0), axis=1, keepdims=True)
    lr2 = jnp.sum(jnp.where(oh2, prefix, 0.0), axis=1, keepdims=True)
    record = jnp.zeros((tr, LANES), F32)
    for col, val in ((R_I1, i1), (R_I2, i2), (R_W1, 1.0 / den), (R_W2, e2 / den), (R_LR1, lr1), (R_LR2, lr2)):
        record = jnp.where(lane == col, val, record)
    route_ref[...] = record
    counts = jnp.sum(both, axis=0, keepdims=True)

    rec_t = record.T
    i1_t, i2_t = rec_t[R_I1:R_I1 + 1, :], rec_t[R_I2:R_I2 + 1, :]
    lr1_t, lr2_t = rec_t[R_LR1:R_LR1 + 1, :], rec_t[R_LR2:R_LR2 + 1, :]
    lane_i = lax.broadcasted_iota(I32, (1, LANES), 1)
    starts = jnp.zeros((1, LANES), I32)
    segs = jnp.zeros((1, LANES), I32)

    def block_copy(buf_slot, start):
        return pltpu.make_async_copy(buf_ref.at[buf_slot],
                                     xs_hbm.at[pl.ds(pl.multiple_of(start, MOE_ALIGN), tr), :],
                                     sem_ref.at[buf_slot])

    for e in range(n_experts):
        in_e1 = i1_t == float(e)
        lr = jnp.where(in_e1, lr1_t, jnp.where(i2_t == float(e), lr2_t, -1.0))
        seg = ((counts[0, e].astype(I32) + (MOE_ALIGN - 1)) // MOE_ALIGN) * MOE_ALIGN
        buf_slot = e % 2
        if e >= 2:
            block_copy(buf_slot, 0).wait()
        else:
            @pl.when(i > 0)
            def _(buf_slot=buf_slot):
                block_copy(buf_slot, 0).wait()
        for sb in range(tr // MOE_SUB):
            rows = slice(sb * MOE_SUB, (sb + 1) * MOE_SUB)

            @pl.when(sb * MOE_SUB < seg)
            def _(sb=sb, rows=rows, lr=lr, buf_slot=buf_slot):
                slot = (lax.broadcasted_iota(I32, (MOE_SUB, tr), 0) + sb * MOE_SUB).astype(F32)
                perm = jnp.where(slot == lr, 1.0, 0.0).astype(BF16)
                buf_ref[buf_slot, rows, :] = _dot(perm, h).astype(BF16)

            @pl.when(sb * MOE_SUB >= seg)
            def _(rows=rows, buf_slot=buf_slot):
                buf_ref[buf_slot, rows, :] = jnp.zeros((MOE_SUB, h.shape[1]), BF16)
        start = e * region + cnt_ref[e]
        block_copy(buf_slot, start).start()
        starts = jnp.where(lane_i == e, start, starts)
        segs = jnp.where(lane_i == e, seg, segs)
        cnt_ref[e] = cnt_ref[e] + seg

    starts_ref[0] = starts
    seg_ref[0] = segs

    @pl.when(i == pl.num_programs(0) - 1)
    def _():
        block_copy(0, 0).wait()
        block_copy(1, 0).wait()
        buf_ref[0] = jnp.zeros(buf_ref.shape[1:], BF16)
        for e in range(n_experts):
            for k in range(MOE_TG // tr + 1):
                tail = block_copy(0, e * region + jnp.minimum(cnt_ref[e] + k * tr, region - tr))
                tail.start()
                tail.wait()


def moe_route_dispatch(x, g, router_w_padded, n_experts):
    m, d = x.shape
    tr = MOE_TR
    n_tiles = m // tr
    region = m + MOE_TG
    kern = functools.partial(_route_dispatch_kernel, n_experts=n_experts, region=region)
    return pl.pallas_call(
        kern,
        out_shape=(jax.ShapeDtypeStruct((n_experts * region, d), BF16),
                   jax.ShapeDtypeStruct((m, LANES), F32),
                   jax.ShapeDtypeStruct((n_tiles, 1, LANES), I32),
                   jax.ShapeDtypeStruct((n_tiles, 1, LANES), I32)),
        grid=(n_tiles,),
        in_specs=[pl.BlockSpec((tr, d), lambda i: (i, 0)),
                  pl.BlockSpec((1, d), lambda i: (0, 0)),
                  pl.BlockSpec((d, LANES), lambda i: (0, 0))],
        out_specs=(pl.BlockSpec(memory_space=pl.ANY),
                   pl.BlockSpec((tr, LANES), lambda i: (i, 0)),
                   pl.BlockSpec((1, 1, LANES), lambda i: (i, 0, 0)),
                   pl.BlockSpec((1, 1, LANES), lambda i: (i, 0, 0))),
        scratch_shapes=[pltpu.VMEM((2, tr, d), BF16), pltpu.SemaphoreType.DMA((2,)),
                        pltpu.SMEM((n_experts,), I32)],
        compiler_params=_cparams(("arbitrary",)),
        name="moe_route_dispatch",
    )(x, g, router_w_padded)


def _ffn_grouped_kernel(trow_ref, texp_ref, tval_ref, x_ref, wg_ref, wu_ref, wd_ref, o_ref, acc_ref):
    t = pl.program_id(0)
    f = pl.program_id(1)

    @pl.when(jnp.logical_and(tval_ref[t] == TILE_ZERO, f == 0))
    def _():
        o_ref[...] = jnp.zeros(o_ref.shape, o_ref.dtype)

    @pl.when(tval_ref[t] == TILE_COMPUTE)
    def _():
        part = _dot(_swiglu(x_ref[...], wg_ref[0], wu_ref[0]).astype(BF16), wd_ref[0])

        @pl.when(f == 0)
        def _():
            acc_ref[...] = part

        @pl.when(f > 0)
        def _():
            acc_ref[...] += part

        @pl.when(f == pl.num_programs(1) - 1)
        def _():
            o_ref[...] = acc_ref[...].astype(o_ref.dtype)


def moe_grouped_ffn(xs, w_gu, w_down, trow, texp, tval):
    rows, d = xs.shape
    dff = w_down.shape[1]
    nf = dff // FFN_TF
    n_steps = trow.shape[0]
    col = lambda f, tv, t: jnp.where(tv[t] == TILE_COMPUTE, f, nf - 1)
    grid_spec = pltpu.PrefetchScalarGridSpec(
        num_scalar_prefetch=3,
        grid=(n_steps, nf),
        in_specs=[pl.BlockSpec((MOE_TG, d), lambda t, f, tr_, te, tv: (tr_[t], 0)),
                  pl.BlockSpec((1, d, FFN_TF), lambda t, f, tr_, te, tv: (te[t], 0, col(f, tv, t))),
                  pl.BlockSpec((1, d, FFN_TF), lambda t, f, tr_, te, tv: (te[t], 0, nf + col(f, tv, t))),
                  pl.BlockSpec((1, FFN_TF, d), lambda t, f, tr_, te, tv: (te[t], col(f, tv, t), 0))],
        out_specs=pl.BlockSpec((MOE_TG, d), lambda t, f, tr_, te, tv: (tr_[t], 0)),
        scratch_shapes=[pltpu.VMEM((MOE_TG, d), F32)])
    return pl.pallas_call(
        _ffn_grouped_kernel,
        out_shape=jax.ShapeDtypeStruct((rows, d), BF16),
        grid_spec=grid_spec,
        compiler_params=_cparams(("arbitrary", "arbitrary")),
        name="moe_grouped_ffn",
    )(trow, texp, tval, xs, w_gu, w_gu, w_down)


def _moe_combine_kernel(starts_ref, seg_ref, x_ref, route_ref, gain_ref, ys_hbm, o_ref, buf_ref, sem_ref,
                        *, n_experts, final_norm_fused):
    i = pl.program_id(0)
    tr = MOE_TR

    n_sub = tr // MOE_KSUB
    n_tiles = pl.num_programs(0)
    cur = i % 2

    def sub_copy(tile, e, kb):
        start = starts_ref[tile * n_experts + e] + kb * MOE_KSUB
        return pltpu.make_async_copy(ys_hbm.at[pl.ds(pl.multiple_of(start, MOE_ALIGN), MOE_KSUB), :],
                                     buf_ref.at[tile % 2, e, kb * MOE_KSUB:(kb + 1) * MOE_KSUB, :],
                                     sem_ref.at[(tile % 2) * n_experts * n_sub + e * n_sub + kb])

    def start_tile(tile):
        for e in range(n_experts):
            for kb in range(n_sub):
                @pl.when(kb * MOE_KSUB < seg_ref[tile * n_experts + e])
                def _(e=e, kb=kb):
                    sub_copy(tile, e, kb).start()

    @pl.when(i == 0)
    def _():
        start_tile(i)

    @pl.when(i + 1 < n_tiles)
    def _():
        start_tile(i + 1)

    rec = route_ref[...]
    i1, i2 = rec[:, R_I1:R_I1 + 1], rec[:, R_I2:R_I2 + 1]
    w1, w2 = rec[:, R_W1:R_W1 + 1], rec[:, R_W2:R_W2 + 1]
    lr1, lr2 = rec[:, R_LR1:R_LR1 + 1], rec[:, R_LR2:R_LR2 + 1]
    o_ref[...] = x_ref[...]
    for e in range(n_experts):
        in_e1 = i1 == float(e)
        in_e2 = i2 == float(e)
        lr = jnp.where(in_e1, lr1, jnp.where(in_e2, lr2, -1.0))
        w = jnp.where(in_e1, w1, jnp.where(in_e2, w2, 0.0))
        seg = seg_ref[i * n_experts + e]
        for kb in range(n_sub):
            @pl.when(kb * MOE_KSUB < seg)
            def _(kb=kb, lr=lr, w=w, e=e):
                sub_copy(i, e, kb).wait()
                slot = (lax.broadcasted_iota(I32, (tr, MOE_KSUB), 1) + kb * MOE_KSUB).astype(F32)
                pick = jnp.where(slot == lr, 1.0, 0.0).astype(BF16)
                o_ref[...] += w * _dot(pick, buf_ref[cur, e, kb * MOE_KSUB:(kb + 1) * MOE_KSUB, :])
    if final_norm_fused:
        o_ref[...] = _rms(o_ref[...], gain_ref[...])


def moe_combine(x, route, ys, starts_flat, seg_flat, n_experts, final_gain):
    m, d = x.shape
    tr = MOE_TR
    fused = final_gain is not None
    gain = final_gain if fused else jnp.ones((1, d), F32)
    grid_spec = pltpu.PrefetchScalarGridSpec(
        num_scalar_prefetch=2,
        grid=(m // tr,),
        in_specs=[pl.BlockSpec((tr, d), lambda i, s, g: (i, 0)),
                  pl.BlockSpec((tr, LANES), lambda i, s, g: (i, 0)),
                  pl.BlockSpec((1, d), lambda i, s, g: (0, 0)),
                  pl.BlockSpec(memory_space=pl.ANY)],
        out_specs=pl.BlockSpec((tr, d), lambda i, s, g: (i, 0)),
        scratch_shapes=[pltpu.VMEM((2, n_experts, tr, d), BF16),
                        pltpu.SemaphoreType.DMA((2 * n_experts * (tr // MOE_KSUB),))])
    return pl.pallas_call(
        functools.partial(_moe_combine_kernel, n_experts=n_experts, final_norm_fused=fused),
        out_shape=jax.ShapeDtypeStruct((m, d), F32),
        grid_spec=grid_spec,
        compiler_params=_cparams(("arbitrary",)),
        name="moe_combine",
    )(starts_flat, seg_flat, x, route, gain, ys)


TILE_DEAD, TILE_COMPUTE, TILE_ZERO = 0, 1, 2


def _moe_tile_plan(seg, n_experts, region, n_steps):
    tiles_per_region = region // MOE_TG
    rows = jnp.sum(seg, axis=0)
    n_compute = (rows + MOE_TG - 1) // MOE_TG
    n_live = jnp.minimum((rows + MOE_TR + MOE_TG - 1) // MOE_TG, tiles_per_region)
    ends = jnp.cumsum(n_live)
    total = ends[-1]
    step = jnp.arange(n_steps, dtype=I32)
    t = jnp.minimum(step, total - 1)
    texp = jnp.sum((t[:, None] >= ends[None, :]).astype(I32), axis=1)
    within = t - (ends - n_live)[texp]
    trow = texp * tiles_per_region + within
    kind = jnp.where(step < total, jnp.where(within < n_compute[texp], TILE_COMPUTE, TILE_ZERO), TILE_DEAD)
    return trow.astype(I32), texp.astype(I32), kind.astype(I32)


def moe_ffn(x, g, router_w, w_gu, w_down, final_gain=None):
    m, d = x.shape
    n_experts = router_w.shape[-1]
    region = m + MOE_TG
    n_tiles = m // MOE_TR
    rw = jnp.pad(router_w, ((0, 0), (0, LANES - n_experts)))
    xs, route, starts, seg = moe_route_dispatch(x, g, rw, n_experts)
    starts = starts[:, 0, :n_experts]
    seg = seg[:, 0, :n_experts]
    max_rows = TOP_K * m + n_tiles * n_experts * (MOE_ALIGN - 1)
    n_steps = max_rows // MOE_TG + 2 * n_experts
    trow, texp, tval = _moe_tile_plan(seg, n_experts, region, n_steps)
    ys = moe_grouped_ffn(xs, w_gu, w_down, trow, texp, tval)
    return moe_combine(x, route, ys, starts.reshape(-1), seg.reshape(-1), n_experts, final_gain)


def _final_norm_kernel(x_ref, g_ref, o_ref):
    o_ref[...] = _rms(x_ref[...], g_ref[...])


def final_norm(x, g):
    m, d = x.shape
    tm = min(m, 1024)
    return pl.pallas_call(
        _final_norm_kernel,
        out_shape=jax.ShapeDtypeStruct((m, d), F32),
        grid=(m // tm,),
        in_specs=[pl.BlockSpec((tm, d), lambda i: (i, 0)), pl.BlockSpec((1, d), lambda i: (0, 0))],
        out_specs=pl.BlockSpec((tm, d), lambda i: (i, 0)),
        compiler_params=_cparams(("parallel",)),
        name="final_norm",
    )(x, g)


def _column_plan(d_model):
    sizes = (SWA_HEADS * HEAD_DIM, SWA_KV_HEADS * HEAD_DIM, SWA_KV_HEADS * HEAD_DIM,
             DSA_HEADS * HEAD_DIM, DSA_KV_HEADS * HEAD_DIM, DSA_KV_HEADS * HEAD_DIM,
             IDX_HEADS * IDX_DIM, IDX_DIM, IDX_HEADS,
             RET_HEADS * RET_KEY_DIM, RET_HEADS * RET_KEY_DIM, RET_HEADS * RET_VAL_DIM, RET_HEADS * RET_VAL_DIM,
             len(POOL_WINDOWS) * POOL_GROUP_DIM, N_BRANCH * d_model)
    starts = np.cumsum((0,) + sizes[:-1])
    (a_q, a_k, a_v, b_q, b_k, b_v, i_q, i_k, i_w, r_q, r_k, r_v, r_g, p_u, gates) = [
        (np.arange(s, s + n), np.ones(n)) for s, n in zip(starts, sizes)]

    def scaled(c, f):
        return c[0], c[1] * f

    def dup_heads(c):
        return tuple(np.repeat(z.reshape(-1, 1, HEAD_DIM), 2, axis=1).reshape(-1) for z in c)

    def rotate_half(c):
        idx, fac = (z.reshape(-1, 2, HEAD_DIM // 2) for z in c)
        return (np.stack([idx[:, 1], idx[:, 0]], axis=1).reshape(-1),
                np.stack([-fac[:, 1], fac[:, 0]], axis=1).reshape(-1))

    def table(cols, width):
        idx = np.concatenate([c[0] for c in cols])
        fac = np.concatenate([c[1] for c in cols])
        pad = width - idx.size
        return (np.concatenate([idx, np.zeros(pad, idx.dtype)]).astype(np.int32),
                np.concatenate([fac, np.zeros(pad)]).astype(np.float32))

    qs = HEAD_DIM ** -0.5
    rope_cols = [scaled(a_q, qs), scaled(b_q, qs), scaled(i_q, IDX_DIM ** -0.5), r_q,
                 scaled(r_k, RET_KEY_DIM ** -0.5), dup_heads(a_k), dup_heads(b_k), dup_heads(i_k)]
    plain_cols = [gates, r_v, r_g, p_u, dup_heads(a_v), b_v, scaled(i_w, IDX_HEADS ** -0.5)]
    return (table(rope_cols, PA_COLS), table([rotate_half(c) for c in rope_cols], PA_COLS),
            table(plain_cols, PB_COLS))


def _layout_w_in(w_in, plan):
    return tuple((jnp.take(w_in, idx, axis=1) * fac[None, :]).astype(BF16) for idx, fac in plan)


def kernel(x, mem, positions, norm_mix, w_in, attn_sink, w_branch, w_out, pool_w, pool_scale, norm_cross, norm_mem, w_xq, w_xkv, w_xo, norm_ffn, dense_w_gu, dense_w_down, router_w, moe_w_gu, moe_w_down, norm_final):
    batch, seq, d = x.shape
    depth = norm_mix.shape[0]
    m = batch * seq
    mem_len = mem.shape[1]
    xs = x.reshape(m, d)
    mem2 = mem.reshape(batch * mem_len, d)

    inv = 1.0 / (ROPE_THETA ** (jnp.arange(0, HEAD_DIM, 2, dtype=F32) / HEAD_DIM))
    inv128 = jnp.tile(inv, LANES // (HEAD_DIM // 2))[None, :]
    cos, sin = rope_tables(positions.reshape(m, 1), inv128)
    log_gamma = jnp.log1p(-(2.0 ** (-5.0 - jnp.arange(RET_HEADS, dtype=F32))))
    x_scale = (d // X_HEADS) ** -0.5
    plan = _column_plan(d)

    for layer in range(depth):
        w_x, w_rot, w_plain = _layout_w_in(w_in[layer], plan)
        pa = norm_matmul_rope(xs, norm_mix[layer][None, :], w_x, w_rot, cos, sin, tm=1024, tn=PA_TN)
        pb = norm_matmul(xs, norm_mix[layer][None, :], w_plain, tm=1024, tn=PB_TN, name="in_proj_plain")
        o_a = swa_attention(pa, pb, attn_sink[layer], batch, seq)
        o_b = dsa_attention(pa, pb, batch, seq)
        o_c = retention(pa, pb, log_gamma, batch, seq)
        o_d = multiscale_pool(pb, pool_w[layer].astype(BF16), pool_scale[layer][None, :], batch, seq)
        xs = merge_branches(xs, (o_a, o_b, o_c, o_d), pb, w_branch[layer].astype(BF16), w_out[layer].astype(BF16))

        kv = norm_matmul(mem2, norm_mem[layer][None, :], w_xkv[layer].astype(BF16),
                         tm=batch * mem_len, tn=512, name="mem_kv_proj")
        xs = cross_attention(xs, norm_cross[layer][None, :], (w_xq[layer] * x_scale).astype(BF16), kv,
                             w_xo[layer].astype(BF16), batch, seq)

        li = layer // 2
        if layer % 2 == 0:
            xs = ffn_dense(xs, norm_ffn[layer][None, :], dense_w_gu[li].astype(BF16), dense_w_down[li].astype(BF16))
        else:
            last = layer == depth - 1
            xs = moe_ffn(xs, norm_ffn[layer][None, :], router_w[li], moe_w_gu[li].astype(BF16),
                         moe_w_down[li].astype(BF16), final_gain=norm_final[None, :] if last else None)

    if depth % 2 == 1:
        xs = final_norm(xs, norm_final[None, :])
    return xs.reshape(batch, seq, d)
```

```python
import functools

import numpy as np
import jax
import jax.numpy as jnp
from jax import lax
from jax.experimental import pallas as pl
from jax.experimental.pallas import tpu as pltpu

F32 = jnp.float32
BF16 = jnp.bfloat16
I32 = jnp.int32

EPS = 1e-6
HEAD_DIM = 64
ROPE_THETA = 10000.0
BLOCK = 128
SWA_HEADS, SWA_KV_HEADS = 8, 2
SWA_STEP_BLOCKS = 2
DSA_HEADS, DSA_KV_HEADS = 8, 2
IDX_HEADS, IDX_DIM = 4, 64
DSA_TOPK = 256
RET_HEADS, RET_KEY_DIM, RET_VAL_DIM = 4, 64, 128
RET_CHUNK = 256
POOL_WINDOWS = (2, 4, 8, 16)
POOL_GROUP_DIM = 128
N_BRANCH = 4
BRANCH_WIDTH = 512
X_HEADS = 4
TOP_K = 2

LANES = 128
VMEM_LIMIT = 56 * 1024 * 1024
NEG_BIG = -1e30
LOG2_E = 1.4426950408889634
INT_MIN = -2 ** 31

PA_AQ, PA_BQ, PA_IQ, PA_RQ, PA_RK, PA_AK, PA_BK, PA_IK = 0, 512, 1024, 1280, 1536, 1792, 2048, 2304
PA_COLS, PA_TN = 2560, 1280
PB_GATES, PB_RV, PB_RG, PB_PU, PB_AV, PB_BV, PB_IW = 0, 4096, 4608, 5120, 5632, 5888, 6016
PB_COLS, PB_TN = 6144, 1536


def _cparams(sem):
    return pltpu.CompilerParams(dimension_semantics=sem, vmem_limit_bytes=VMEM_LIMIT)


def _dot(a, b):
    return jnp.dot(a, b, preferred_element_type=F32)


def _dot_nt(a, b):
    return lax.dot_general(a, b, (((1,), (1,)), ((), ())), preferred_element_type=F32)


def _rms(x, g):
    return x * lax.rsqrt(jnp.mean(x * x, axis=-1, keepdims=True) + EPS) * g


def _half_mask(hh):
    lane = lax.broadcasted_iota(I32, (1, LANES), 1)
    return (lane >= HEAD_DIM) if hh else (lane < HEAD_DIM)


def _keep_half(block, hh):
    return jnp.where(_half_mask(hh), block, jnp.zeros_like(block))


def _rope_kernel(pos_ref, inv_ref, cos_ref, sin_ref):
    ang = pos_ref[...].astype(F32) * inv_ref[...]
    cos_ref[...] = jnp.cos(ang)
    sin_ref[...] = jnp.sin(ang)


def rope_tables(pos_col, inv128):
    m = pos_col.shape[0]
    tm = min(m, 2048)
    return pl.pallas_call(
        _rope_kernel,
        out_shape=(jax.ShapeDtypeStruct((m, LANES), F32), jax.ShapeDtypeStruct((m, LANES), F32)),
        grid=(m // tm,),
        in_specs=[pl.BlockSpec((tm, 1), lambda i: (i, 0)), pl.BlockSpec((1, LANES), lambda i: (0, 0))],
        out_specs=(pl.BlockSpec((tm, LANES), lambda i: (i, 0)), pl.BlockSpec((tm, LANES), lambda i: (i, 0))),
        compiler_params=_cparams(("parallel",)),
        name="rope_tables",
    )(pos_col, inv128)


def _norm_matmul_kernel(x_ref, g_ref, w_ref, o_ref, h_ref):
    @pl.when(pl.program_id(1) == 0)
    def _():
        h_ref[...] = _rms(x_ref[...], g_ref[...]).astype(BF16)

    o_ref[...] = _dot(h_ref[...], w_ref[...]).astype(o_ref.dtype)


def norm_matmul(x, g, w, tm, tn, out_dtype=BF16, name="norm_matmul"):
    m, k = x.shape
    n = w.shape[1]
    tm = min(tm, m)
    return pl.pallas_call(
        _norm_matmul_kernel,
        out_shape=jax.ShapeDtypeStruct((m, n), out_dtype),
        grid=(m // tm, n // tn),
        in_specs=[pl.BlockSpec((tm, k), lambda i, j: (i, 0)),
                  pl.BlockSpec((1, k), lambda i, j: (0, 0)),
                  pl.BlockSpec((k, tn), lambda i, j: (0, j))],
        out_specs=pl.BlockSpec((tm, tn), lambda i, j: (i, j)),
        scratch_shapes=[pltpu.VMEM((tm, k), BF16)],
        compiler_params=_cparams(("parallel", "arbitrary")),
        name=name,
    )(x, g, w)


def _norm_matmul_rope_kernel(x_ref, g_ref, wx_ref, wr_ref, cos_ref, sin_ref, o_ref, h_ref):
    @pl.when(pl.program_id(1) == 0)
    def _():
        h_ref[...] = _rms(x_ref[...], g_ref[...]).astype(BF16)

    h = h_ref[...]
    proj = _dot(h, wx_ref[...])
    rot = _dot(h, wr_ref[...])
    cos = cos_ref[...]
    sin = sin_ref[...]
    for c in range(o_ref.shape[1] // LANES):
        cols = slice(c * LANES, (c + 1) * LANES)
        o_ref[:, cols] = (proj[:, cols] * cos + rot[:, cols] * sin).astype(o_ref.dtype)


def norm_matmul_rope(x, g, wx, wr, cos, sin, tm, tn):
    m, k = x.shape
    n = wx.shape[1]
    tm = min(tm, m)
    return pl.pallas_call(
        _norm_matmul_rope_kernel,
        out_shape=jax.ShapeDtypeStruct((m, n), BF16),
        grid=(m // tm, n // tn),
        in_specs=[pl.BlockSpec((tm, k), lambda i, j: (i, 0)),
                  pl.BlockSpec((1, k), lambda i, j: (0, 0)),
                  pl.BlockSpec((k, tn), lambda i, j: (0, j)),
                  pl.BlockSpec((k, tn), lambda i, j: (0, j)),
                  pl.BlockSpec((tm, LANES), lambda i, j: (i, 0)),
                  pl.BlockSpec((tm, LANES), lambda i, j: (i, 0))],
        out_specs=pl.BlockSpec((tm, tn), lambda i, j: (i, j)),
        scratch_shapes=[pltpu.VMEM((tm, k), BF16)],
        compiler_params=_cparams(("parallel", "arbitrary")),
        name="in_proj_rope",
    )(x, g, wx, wr, cos, sin)


def _swa_kernel(sink_ref, q_ref, kc_ref, kp_ref, vc_ref, vp_ref, o_ref):
    i = pl.program_id(1)
    per_group = SWA_HEADS // SWA_KV_HEADS
    q_t_all = (q_ref[...].astype(F32) * LOG2_E).T.astype(BF16)
    key = lax.broadcasted_iota(I32, (2 * BLOCK, 1), 0)
    rel = lax.broadcasted_iota(I32, (1, BLOCK), 1) + BLOCK - key
    in_window = jnp.logical_and(rel >= 0, rel < BLOCK)
    for sub in range(SWA_STEP_BLOCKS):
        rows = slice(sub * BLOCK, (sub + 1) * BLOCK)
        before = slice((sub - 1) * BLOCK, sub * BLOCK)
        q_t = q_t_all[:, rows]
        band = in_window if sub else jnp.logical_and(in_window, jnp.logical_or(key >= BLOCK, i > 0))
        bias = jnp.where(band, 0.0, -jnp.inf).astype(F32)
        bias = jnp.concatenate([bias] * per_group, axis=1)
        outs = []
        for g in range(SWA_KV_HEADS):
            cols = slice(g * LANES, (g + 1) * LANES)
            k_prev = kc_ref[before, cols] if sub else kp_ref[:, cols]
            v_prev = vc_ref[before, cols] if sub else vp_ref[:, cols]
            k_win = jnp.concatenate([k_prev, kc_ref[rows, cols]], axis=0)[:, 0:HEAD_DIM]
            v_t = jnp.concatenate([v_prev, vc_ref[rows, cols]], axis=0).astype(F32).T[0:HEAD_DIM].astype(BF16)
            heads = range(g * per_group, (g + 1) * per_group)
            q_g = jnp.concatenate([q_t[h * HEAD_DIM:(h + 1) * HEAD_DIM, :] for h in heads], axis=1)
            sink = jnp.concatenate([jnp.full((1, BLOCK), sink_ref[h] * LOG2_E, F32) for h in heads], axis=1)
            s = _dot(k_win, q_g) + bias
            m = jnp.maximum(jnp.max(s, axis=0, keepdims=True), sink)
            p = jnp.exp2(s - m)
            den = jnp.sum(p, axis=0, keepdims=True) + jnp.exp2(sink - m)
            o_t = _dot(v_t, p.astype(BF16)) / den
            outs += [o_t[:, a * BLOCK:(a + 1) * BLOCK] for a in range(per_group)]
        o_ref[rows, :] = jnp.concatenate(outs, axis=0).T.astype(o_ref.dtype)


def swa_attention(pa, pb, sink, batch, seq):
    nb = seq // BLOCK
    steps = nb // SWA_STEP_BLOCKS
    rows = SWA_STEP_BLOCKS * BLOCK
    cur = lambda col: (lambda b, i: (b * steps + i, col))
    prev = lambda col: (lambda b, i: (b * nb + jnp.maximum(i * SWA_STEP_BLOCKS - 1, 0), col))
    return pl.pallas_call(
        _swa_kernel,
        out_shape=jax.ShapeDtypeStruct((batch * seq, BRANCH_WIDTH), BF16),
        grid=(batch, steps),
        in_specs=[pl.BlockSpec(memory_space=pltpu.SMEM),
                  pl.BlockSpec((rows, 512), cur(PA_AQ // 512)),
                  pl.BlockSpec((rows, 256), cur(PA_AK // 256)),
                  pl.BlockSpec((BLOCK, 256), prev(PA_AK // 256)),
                  pl.BlockSpec((rows, 256), cur(PB_AV // 256)),
                  pl.BlockSpec((BLOCK, 256), prev(PB_AV // 256))],
        out_specs=pl.BlockSpec((rows, BRANCH_WIDTH), cur(0)),
        compiler_params=_cparams(("parallel", "arbitrary")),
        name="swa_attention",
    )(sink, pa, pa, pa, pb, pb)


DSA_TQ = 256
DSA_ONES = 16


def _dsa_kernel(q_ref, iq_ref, iw_ref, k_ref, ik_ref, v_ref, o_ref,
                skey_ref, vt_ref, qt_ref, cut_ref, m_ref, l_ref, acc_ref, alpha_ref, bias_ref, s_ref, p_ref, odd_ref,
                *, n_sel, seq):
    tq = DSA_TQ
    nq = seq // tq
    i = pl.program_id(1)
    n_chunks = i + 1
    q_pos = i * tq + lax.broadcasted_iota(I32, (1, tq), 1)
    key_off = lax.broadcasted_iota(I32, (tq, 1), 0)

    @pl.when(i == 0)
    def _():
        for jj in range(nq):
            vt = v_ref[jj * tq:(jj + 1) * tq, :].astype(F32).T
            ones = jnp.ones((DSA_ONES, tq), BF16)
            for g in range(DSA_KV_HEADS):
                vt_ref[jj, g, 0:HEAD_DIM, :] = vt[g * HEAD_DIM:(g + 1) * HEAD_DIM].astype(BF16)
                vt_ref[jj, g, HEAD_DIM:, :] = ones

    iq_t = iq_ref[...].astype(F32).T.astype(BF16)
    w_t = iw_ref[...].astype(F32).T

    def chunk_keys(j):
        ikc = ik_ref[pl.ds(pl.multiple_of(j * tq, tq), tq), 0:IDX_DIM]
        sc = jnp.zeros((tq, tq), F32)
        for h in range(IDX_HEADS):
            sc = sc + jnp.maximum(_dot(ikc, iq_t[h * IDX_DIM:(h + 1) * IDX_DIM, :]), 0.0) * w_t[h:h + 1, :]
        sc = jnp.where(sc == 0.0, 0.0, sc)
        bits = lax.bitcast_convert_type(sc, I32)
        return jnp.where(bits < 0, bits ^ jnp.int32(0x7FFFFFFF), bits)

    def score_chunk(j, carry):
        skey_ref[j] = chunk_keys(j)
        return carry

    lax.fori_loop(0, i, score_chunk, 0)
    skey_ref[i] = jnp.where((i * tq + key_off) <= q_pos, chunk_keys(i), jnp.int32(INT_MIN))

    view = (tq // 8, 8, tq)
    pos_in_chunk = lax.broadcasted_iota(I32, view, 0) * 8 + lax.broadcasted_iota(I32, view, 1)

    def all_sublanes(x, op):
        for sh in (4, 2, 1):
            x = op(x, pltpu.roll(x, sh, axis=0))
        return x

    def count_keys(pred):
        def chunk_count(j):
            x = pred(skey_ref[j].reshape(view), j)
            part = view[0] // 4
            sums = [jnp.sum(x[a * part:(a + 1) * part], axis=0) for a in range(4)]
            return (sums[0] + sums[1]) + (sums[2] + sums[3])

        def quad(jq, acc):
            j0 = 4 * jq
            return acc + ((chunk_count(j0) + chunk_count(j0 + 1)) + (chunk_count(j0 + 2) + chunk_count(j0 + 3)))

        n_quads = n_chunks // 4
        rest = n_chunks - 4 * n_quads
        odd_ref[...] = jnp.zeros((8, tq), F32)

        @pl.when(rest >= 2)
        def _():
            odd_ref[...] = chunk_count(4 * n_quads) + chunk_count(4 * n_quads + 1)

        @pl.when(rest % 2 == 1)
        def _():
            odd_ref[...] += chunk_count(n_chunks - 1)

        acc = lax.fori_loop(0, n_quads, quad, jnp.zeros((8, tq), F32))
        return all_sublanes(acc + odd_ref[...], jnp.add)

    one = jnp.float32(1.0)
    zero = jnp.float32(0.0)
    k_f = jnp.float32(n_sel)
    cut_bits = max((seq - 1).bit_length(), 1)

    def bit_step(b, carry):
        res_u, c_ge = carry
        cand_u = res_u | lax.shift_left(jnp.int32(1), 31 - b)
        cand_s = cand_u ^ jnp.int32(INT_MIN)
        cnt = count_keys(lambda kc, j: jnp.where(kc >= cand_s, one, zero))
        take = cnt >= k_f
        return jnp.where(take, cand_u, res_u), jnp.where(take, cnt, c_ge)

    res_u, c_ge = lax.fori_loop(0, 32, bit_step, (jnp.zeros((8, tq), I32), jnp.zeros((8, tq), F32)))
    thr = res_u ^ jnp.int32(INT_MIN)
    short = thr == jnp.int32(INT_MIN)
    cut_ref[...] = jnp.where(short, jnp.int32(-1), jnp.int32(seq))
    unresolved = jnp.where(jnp.logical_and(c_ge != k_f, jnp.logical_not(short)), one, zero)

    @pl.when(jnp.max(unresolved) > 0.0)
    def _():
        need = k_f - count_keys(lambda kc, j: jnp.where(kc > thr, one, zero))

        def cut_step(b, ans):
            cand = ans | lax.shift_left(jnp.int32(1), cut_bits - 1 - b)
            lim = cand - 1

            def pred(kc, j):
                return jnp.where(kc == thr, jnp.where((j * tq + pos_in_chunk) <= lim, one, zero), zero)

            return jnp.where(count_keys(pred) < need, cand, ans)

        ans = lax.fori_loop(0, cut_bits, cut_step, jnp.zeros((8, tq), I32))
        cut_ref[...] = jnp.where(short, jnp.int32(-1), ans)

    qt_ref[...] = (q_ref[...].astype(F32) * LOG2_E).T.astype(BF16)
    m_ref[...] = jnp.full(m_ref.shape, NEG_BIG, F32)
    l_ref[...] = jnp.zeros(l_ref.shape, F32)
    acc_ref[...] = jnp.zeros(acc_ref.shape, F32)
    cut = cut_ref[...]
    neg = jnp.float32(NEG_BIG)
    group = DSA_HEADS // DSA_KV_HEADS

    def attend(j, carry):
        kc = skey_ref[j].reshape(view)
        start = pl.multiple_of(j * tq, tq)
        tie = jnp.where(kc == thr, jnp.where((j * tq + pos_in_chunk) <= cut, zero, neg), neg)
        bias_ref[...] = jnp.where(kc > thr, zero, tie).reshape(tq, tq)
        for h in range(DSA_HEADS):
            g = h // group
            kblk = k_ref[pl.ds(start, tq), g * LANES:g * LANES + HEAD_DIM]
            s = _dot(kblk, qt_ref[h * HEAD_DIM:(h + 1) * HEAD_DIM, :]) + bias_ref[...]
            s_ref[h] = s
            m_old = m_ref[h]
            m_new = jnp.maximum(m_old, jnp.max(s, axis=0, keepdims=True))
            alpha_ref[h] = jnp.exp2(m_old - m_new)
            m_ref[h] = m_new
        for h in range(DSA_HEADS):
            p_ref[h] = jnp.exp2(s_ref[h] - m_ref[h]).astype(BF16)
        for h in range(DSA_HEADS):
            pv = _dot(vt_ref[j, h // group], p_ref[h])
            acc_ref[h] = alpha_ref[h] * acc_ref[h] + pv[0:HEAD_DIM]
            l_ref[h] = alpha_ref[h] * l_ref[h] + pv[HEAD_DIM:HEAD_DIM + 1]
        return carry

    lax.fori_loop(0, n_chunks, attend, 0)

    for p in range(DSA_HEADS // 2):
        o_t = jnp.concatenate([acc_ref[2 * p] / l_ref[2 * p], acc_ref[2 * p + 1] / l_ref[2 * p + 1]], axis=0)
        o_ref[:, p * LANES:(p + 1) * LANES] = o_t.T.astype(o_ref.dtype)


def dsa_attention(pa, pb, batch, seq):
    tq = DSA_TQ
    nq = seq // tq
    n_sel = min(DSA_TOPK, seq // 4)
    qmap = lambda col: (lambda b, i: (b * nq + i, col))
    smap = lambda col: (lambda b, i: (b, col))
    return pl.pallas_call(
        functools.partial(_dsa_kernel, n_sel=n_sel, seq=seq),
        out_shape=jax.ShapeDtypeStruct((batch * seq, BRANCH_WIDTH), BF16),
        grid=(batch, nq),
        in_specs=[pl.BlockSpec((tq, 512), qmap(PA_BQ // 512)),
                  pl.BlockSpec((tq, 256), qmap(PA_IQ // 256)),
                  pl.BlockSpec((tq, LANES), qmap(PB_IW // LANES)),
                  pl.BlockSpec((seq, 256), smap(PA_BK // 256)),
                  pl.BlockSpec((seq, LANES), smap(PA_IK // LANES)),
                  pl.BlockSpec((seq, LANES), smap(PB_BV // LANES))],
        out_specs=pl.BlockSpec((tq, BRANCH_WIDTH), qmap(0)),
        scratch_shapes=[pltpu.VMEM((nq, tq, tq), I32),
                        pltpu.VMEM((nq, DSA_KV_HEADS, HEAD_DIM + DSA_ONES, tq), BF16),
                        pltpu.VMEM((DSA_HEADS * HEAD_DIM, tq), BF16),
                        pltpu.VMEM((8, tq), I32),
                        pltpu.VMEM((DSA_HEADS, 1, tq), F32),
                        pltpu.VMEM((DSA_HEADS, 1, tq), F32),
                        pltpu.VMEM((DSA_HEADS, HEAD_DIM, tq), F32),
                        pltpu.VMEM((DSA_HEADS, 1, tq), F32),
                        pltpu.VMEM((tq, tq), F32),
                        pltpu.VMEM((DSA_HEADS, tq, tq), F32),
                        pltpu.VMEM((DSA_HEADS, tq, tq), BF16),
                        pltpu.VMEM((8, tq), F32)],
        compiler_params=_cparams(("parallel", "arbitrary")),
        name="dsa_attention",
    )(pa, pa, pb, pa, pa, pb)


def _retention_kernel(lg_ref, q_ref, k_ref, v_ref, g_ref, o_ref, state_ref):
    cdim = RET_CHUNK

    @pl.when(pl.program_id(1) == 0)
    def _():
        state_ref[...] = jnp.zeros(state_ref.shape, F32)

    n = lax.broadcasted_iota(I32, (cdim, 1), 0).astype(F32)
    mcol = lax.broadcasted_iota(I32, (1, cdim), 1).astype(F32)
    rel = n - mcol
    for h in range(RET_HEADS):
        lg = lg_ref[h]
        decay = jnp.where(rel >= 0, jnp.exp(lg * jnp.maximum(rel, 0.0)), 0.0)
        q_decay = jnp.exp(lg * (n + 1.0))
        k_decay = jnp.exp(lg * (cdim - 1.0 - n))
        chunk_decay = jnp.exp(jnp.full((1, LANES), lg * cdim, F32))
        blk = slice((h // 2) * LANES, (h // 2 + 1) * LANES)
        qm = _keep_half(q_ref[:, blk], h % 2)
        km = _keep_half(k_ref[:, blk], h % 2)
        v = v_ref[:, h * RET_VAL_DIM:(h + 1) * RET_VAL_DIM]
        state = state_ref[h]
        inner = _dot((_dot_nt(qm, km) * decay).astype(BF16), v)
        cross = _dot(qm, state.astype(BF16)) * q_decay
        kd_t = (km.astype(F32) * k_decay).T.astype(BF16)
        state_ref[h] = state * chunk_decay + _dot(kd_t, v)
        out = inner + cross
        out = out * lax.rsqrt(jnp.mean(out * out, axis=-1, keepdims=True) + EPS)
        gate = g_ref[:, h * RET_VAL_DIM:(h + 1) * RET_VAL_DIM].astype(F32)
        silu = gate * (1.0 / (1.0 + jnp.exp(-gate)))
        o_ref[:, h * RET_VAL_DIM:(h + 1) * RET_VAL_DIM] = (silu * out).astype(o_ref.dtype)


def retention(pa, pb, log_gamma, batch, seq):
    nc = seq // RET_CHUNK
    cmap = lambda col: (lambda b, c: (b * nc + c, col))
    return pl.pallas_call(
        _retention_kernel,
        out_shape=jax.ShapeDtypeStruct((batch * seq, BRANCH_WIDTH), BF16),
        grid=(batch, nc),
        in_specs=[pl.BlockSpec(memory_space=pltpu.SMEM),
                  pl.BlockSpec((RET_CHUNK, 256), cmap(PA_RQ // 256)),
                  pl.BlockSpec((RET_CHUNK, 256), cmap(PA_RK // 256)),
                  pl.BlockSpec((RET_CHUNK, 512), cmap(PB_RV // 512)),
                  pl.BlockSpec((RET_CHUNK, 512), cmap(PB_RG // 512))],
        out_specs=pl.BlockSpec((RET_CHUNK, BRANCH_WIDTH), cmap(0)),
        scratch_shapes=[pltpu.VMEM((RET_HEADS, LANES, RET_VAL_DIM), F32)],
        compiler_params=_cparams(("parallel", "arbitrary")),
        name="retention",
    )(log_gamma, pa, pa, pb, pb)


POOL_HALO = 16


def _pool_kernel(cur_ref, prev_ref, w_ref, scale_ref, o_ref, *, tiles_per_seq):
    tm = cur_ref.shape[0]
    it = pl.program_id(0) % tiles_per_seq
    cur = cur_ref[...].astype(F32)
    prev = jnp.where(it > 0, prev_ref[...].astype(F32), 0.0)
    ext = jnp.concatenate([prev, cur], axis=0)
    t = it * tm + lax.broadcasted_iota(I32, (tm, 1), 0)
    for gi, win in enumerate(POOL_WINDOWS):
        cols = slice(gi * POOL_GROUP_DIM, (gi + 1) * POOL_GROUP_DIM)
        s = ext[:, cols]
        sh = 1
        while sh < win:
            s = s + pltpu.roll(s, sh, axis=0)
            sh *= 2
        cnt = jnp.minimum(t + 1, win).astype(F32)
        pooled = s[POOL_HALO:, :] / cnt
        y = _dot((pooled - cur[:, cols]).astype(BF16), w_ref[gi])
        o_ref[:, cols] = (y * scale_ref[:, cols]).astype(o_ref.dtype)


def multiscale_pool(pb, pool_w, pool_scale, batch, seq):
    m = batch * seq
    tm = min(seq, 512)
    tps = seq // tm
    ratio = tm // POOL_HALO
    return pl.pallas_call(
        functools.partial(_pool_kernel, tiles_per_seq=tps),
        out_shape=jax.ShapeDtypeStruct((m, BRANCH_WIDTH), BF16),
        grid=(m // tm,),
        in_specs=[pl.BlockSpec((tm, 512), lambda i: (i, PB_PU // 512)),
                  pl.BlockSpec((POOL_HALO, 512), lambda i: (jnp.maximum(i * ratio - 1, 0), PB_PU // 512)),
                  pl.BlockSpec((len(POOL_WINDOWS), POOL_GROUP_DIM, POOL_GROUP_DIM), lambda i: (0, 0, 0)),
                  pl.BlockSpec((1, 512), lambda i: (0, 0))],
        out_specs=pl.BlockSpec((tm, BRANCH_WIDTH), lambda i: (i, 0)),
        compiler_params=_cparams(("parallel",)),
        name="multiscale_pool",
    )(pb, pb, pool_w, pool_scale)


def _merge_kernel(x_ref, oa_ref, ob_ref, oc_ref, od_ref, gates_ref, wb_ref, wo_ref, o_ref):
    d = x_ref.shape[1]
    merged = jnp.zeros(x_ref.shape, F32)
    for bi, br in enumerate((oa_ref, ob_ref, oc_ref, od_ref)):
        gate = gates_ref[:, bi * d:(bi + 1) * d].astype(F32)
        merged = merged + (1.0 / (1.0 + jnp.exp(-gate))) * _dot(br[...], wb_ref[bi])
    o_ref[...] = x_ref[...] + _dot(merged.astype(BF16), wo_ref[...])


def merge_branches(x, branches, pb, w_branch, w_out):
    m, d = x.shape
    tm = min(m, 512)
    row = lambda i: (i, 0)
    return pl.pallas_call(
        _merge_kernel,
        out_shape=jax.ShapeDtypeStruct((m, d), F32),
        grid=(m // tm,),
        in_specs=[pl.BlockSpec((tm, d), row)]
        + [pl.BlockSpec((tm, BRANCH_WIDTH), row)] * N_BRANCH
        + [pl.BlockSpec((tm, N_BRANCH * d), lambda i: (i, PB_GATES // (N_BRANCH * d))),
           pl.BlockSpec((N_BRANCH, BRANCH_WIDTH, d), lambda i: (0, 0, 0)),
           pl.BlockSpec((d, d), lambda i: (0, 0))],
        out_specs=pl.BlockSpec((tm, d), row),
        compiler_params=_cparams(("parallel",)),
        name="merge_branches",
    )(x, *branches, pb, w_branch, w_out)


def _cross_kernel(x_ref, g_ref, wq_ref, k_ref, v_ref, wo_ref, o_ref):
    x = x_ref[...]
    d = x.shape[1]
    hd = d // X_HEADS
    q = _dot(_rms(x, g_ref[...]).astype(BF16), wq_ref[...]).astype(BF16)
    outs = []
    for h in range(X_HEADS):
        cols = slice(h * hd, (h + 1) * hd)
        s = _dot_nt(q[:, cols], k_ref[:, cols])
        p = jnp.exp(s - jnp.max(s, axis=1, keepdims=True))
        o = _dot(p.astype(BF16), v_ref[:, cols]) / jnp.sum(p, axis=1, keepdims=True)
        outs.append(o.astype(BF16))
    o_ref[...] = x + _dot(jnp.concatenate(outs, axis=1), wo_ref[...])


def cross_attention(x, g, wq, kv, wo, batch, seq):
    m, d = x.shape
    mem_len = kv.shape[0] // batch
    tm = min(seq, 512)
    nt = seq // tm
    return pl.pallas_call(
        _cross_kernel,
        out_shape=jax.ShapeDtypeStruct((m, d), F32),
        grid=(batch, nt),
        in_specs=[pl.BlockSpec((tm, d), lambda b, i: (b * nt + i, 0)),
                  pl.BlockSpec((1, d), lambda b, i: (0, 0)),
                  pl.BlockSpec((d, d), lambda b, i: (0, 0)),
                  pl.BlockSpec((mem_len, d), lambda b, i: (b, 0)),
                  pl.BlockSpec((mem_len, d), lambda b, i: (b, 1)),
                  pl.BlockSpec((d, d), lambda b, i: (0, 0))],
        out_specs=pl.BlockSpec((tm, d), lambda b, i: (b * nt + i, 0)),
        compiler_params=_cparams(("parallel", "arbitrary")),
        name="cross_attention",
    )(x, g, wq, kv, kv, wo)


FFN_TF = 512


def _swiglu(h, wg, wu):
    a = _dot(h, wg)
    return a * (1.0 / (1.0 + jnp.exp(-a))) * _dot(h, wu)


def _ffn_kernel(x_ref, g_ref, wg_ref, wu_ref, wd_ref, o_ref, h_ref, acc_ref):
    f = pl.program_id(1)

    @pl.when(f == 0)
    def _():
        x = x_ref[...]
        h_ref[...] = _rms(x, g_ref[...]).astype(BF16)
        acc_ref[...] = x

    acc_ref[...] += _dot(_swiglu(h_ref[...], wg_ref[...], wu_ref[...]).astype(BF16), wd_ref[...])

    @pl.when(f == pl.num_programs(1) - 1)
    def _():
        o_ref[...] = acc_ref[...]


def ffn_dense(x, g, w_gu, w_down):
    m, d = x.shape
    dff = w_down.shape[0]
    tm = min(m, 1024)
    nf = dff // FFN_TF
    return pl.pallas_call(
        _ffn_kernel,
        out_shape=jax.ShapeDtypeStruct((m, d), F32),
        grid=(m // tm, nf),
        in_specs=[pl.BlockSpec((tm, d), lambda i, f: (i, 0)),
                  pl.BlockSpec((1, d), lambda i, f: (0, 0)),
                  pl.BlockSpec((d, FFN_TF), lambda i, f: (0, f)),
                  pl.BlockSpec((d, FFN_TF), lambda i, f: (0, nf + f)),
                  pl.BlockSpec((FFN_TF, d), lambda i, f: (f, 0))],
        out_specs=pl.BlockSpec((tm, d), lambda i, f: (i, 0)),
        scratch_shapes=[pltpu.VMEM((tm, d), BF16), pltpu.VMEM((tm, d), F32)],
        compiler_params=_cparams(("parallel", "arbitrary")),
        name="ffn_dense",
    )(x, g, w_gu, w_gu, w_down)


MOE_TR = 512
MOE_TG = 1024
MOE_ALIGN = 16
MOE_SUB = 128
MOE_KSUB = 256
R_I1, R_I2, R_W1, R_W2, R_LR1, R_LR2 = range(6)


def _route_dispatch_kernel(x_ref, g_ref, rw_ref, xs_hbm, route_ref, starts_ref, seg_ref,
                           buf_ref, sem_ref, cnt_ref, *, n_experts, region):
    i = pl.program_id(0)
    tr = MOE_TR

    @pl.when(i == 0)
    def _():
        for e in range(n_experts):
            cnt_ref[e] = 0

    hn = _rms(x_ref[...], g_ref[...])
    h = hn.astype(BF16)
    h_rem = (hn - h.astype(F32)).astype(BF16)
    rw = rw_ref[...]
    rw_head = rw.astype(BF16)
    rw_rem = (rw - rw_head.astype(F32)).astype(BF16)
    logits = (_dot(h, rw_head) + _dot(h_rem, rw_head)) + _dot(h, rw_rem)
    lane = lax.broadcasted_iota(I32, (1, LANES), 1).astype(F32)
    ninf = jnp.float32(-jnp.inf)
    lg = jnp.where(lane < n_experts, logits, ninf)
    m1 = jnp.max(lg, axis=1, keepdims=True)
    i1 = jnp.min(jnp.where(lg == m1, lane, float(LANES)), axis=1, keepdims=True)
    lg2 = jnp.where(lane == i1, ninf, lg)
    m2 = jnp.max(lg2, axis=1, keepdims=True)
    i2 = jnp.min(jnp.where(lg2 == m2, lane, float(LANES)), axis=1, keepdims=True)
    e2 = jnp.exp(m2 - m1)
    den = 1.0 + e2
    oh1 = lane == i1
    oh2 = lane == i2
    both = jnp.where(jnp.logical_or(oh1, oh2), 1.0, 0.0)
    tri = (lax.broadcasted_iota(I32, (tr, tr), 0) > lax.broadcasted_iota(I32, (tr, tr), 1))
    prefix = _dot(jnp.where(tri, 1.0, 0.0).astype(BF16), both.astype(BF16))
    lr1 = jnp.sum(jnp.where(oh1, prefix, 0.0), axis=1, keepdims=True)
    lr2 = jnp.sum(jnp.where(oh2, prefix, 0.0), axis=1, keepdims=True)
    record = jnp.zeros((tr, LANES), F32)
    for col, val in ((R_I1, i1), (R_I2, i2), (R_W1, 1.0 / den), (R_W2, e2 / den), (R_LR1, lr1), (R_LR2, lr2)):
        record = jnp.where(lane == col, val, record)
    route_ref[...] = record
    counts = jnp.sum(both, axis=0, keepdims=True)

    rec_t = record.T
    i1_t, i2_t = rec_t[R_I1:R_I1 + 1, :], rec_t[R_I2:R_I2 + 1, :]
    lr1_t, lr2_t = rec_t[R_LR1:R_LR1 + 1, :], rec_t[R_LR2:R_LR2 + 1, :]
    lane_i = lax.broadcasted_iota(I32, (1, LANES), 1)
    starts = jnp.zeros((1, LANES), I32)
    segs = jnp.zeros((1, LANES), I32)

    def block_copy(buf_slot, start):
        return pltpu.make_async_copy(buf_ref.at[buf_slot],
                                     xs_hbm.at[pl.ds(pl.multiple_of(start, MOE_ALIGN), tr), :],
                                     sem_ref.at[buf_slot])

    for e in range(n_experts):
        in_e1 = i1_t == float(e)
        lr = jnp.where(in_e1, lr1_t, jnp.where(i2_t == float(e), lr2_t, -1.0))
        seg = ((counts[0, e].astype(I32) + (MOE_ALIGN - 1)) // MOE_ALIGN) * MOE_ALIGN
        buf_slot = e % 2
        if e >= 2:
            block_copy(buf_slot, 0).wait()
        else:
            @pl.when(i > 0)
            def _(buf_slot=buf_slot):
                block_copy(buf_slot, 0).wait()
        for sb in range(tr // MOE_SUB):
            rows = slice(sb * MOE_SUB, (sb + 1) * MOE_SUB)

            @pl.when(sb * MOE_SUB < seg)
            def _(sb=sb, rows=rows, lr=lr, buf_slot=buf_slot):
                slot = (lax.broadcasted_iota(I32, (MOE_SUB, tr), 0) + sb * MOE_SUB).astype(F32)
                perm = jnp.where(slot == lr, 1.0, 0.0).astype(BF16)
                buf_ref[buf_slot, rows, :] = _dot(perm, h).astype(BF16)

            @pl.when(sb * MOE_SUB >= seg)
            def _(rows=rows, buf_slot=buf_slot):
                buf_ref[buf_slot, rows, :] = jnp.zeros((MOE_SUB, h.shape[1]), BF16)
        start = e * region + cnt_ref[e]
        block_copy(buf_slot, start).start()
        starts = jnp.where(lane_i == e, start, starts)
        segs = jnp.where(lane_i == e, seg, segs)
        cnt_ref[e] = cnt_ref[e] + seg

    starts_ref[0] = starts
    seg_ref[0] = segs

    @pl.when(i == pl.num_programs(0) - 1)
    def _():
        block_copy(0, 0).wait()
        block_copy(1, 0).wait()
        buf_ref[0] = jnp.zeros(buf_ref.shape[1:], BF16)
        for e in range(n_experts):
            for k in range(MOE_TG // tr + 1):
                tail = block_copy(0, e * region + jnp.minimum(cnt_ref[e] + k * tr, region - tr))
                tail.start()
                tail.wait()


def moe_route_dispatch(x, g, router_w_padded, n_experts):
    m, d = x.shape
    tr = MOE_TR
    n_tiles = m // tr
    region = m + MOE_TG
    kern = functools.partial(_route_dispatch_kernel, n_experts=n_experts, region=region)
    return pl.pallas_call(
        kern,
        out_shape=(jax.ShapeDtypeStruct((n_experts * region, d), BF16),
                   jax.ShapeDtypeStruct((m, LANES), F32),
                   jax.ShapeDtypeStruct((n_tiles, 1, LANES), I32),
                   jax.ShapeDtypeStruct((n_tiles, 1, LANES), I32)),
        grid=(n_tiles,),
        in_specs=[pl.BlockSpec((tr, d), lambda i: (i, 0)),
                  pl.BlockSpec((1, d), lambda i: (0, 0)),
                  pl.BlockSpec((d, LANES), lambda i: (0, 0))],
        out_specs=(pl.BlockSpec(memory_space=pl.ANY),
                   pl.BlockSpec((tr, LANES), lambda i: (i, 0)),
                   pl.BlockSpec((1, 1, LANES), lambda i: (i, 0, 0)),
                   pl.BlockSpec((1, 1, LANES), lambda i: (i, 0, 0))),
        scratch_shapes=[pltpu.VMEM((2, tr, d), BF16), pltpu.SemaphoreType.DMA((2,)),
                        pltpu.SMEM((n_experts,), I32)],
        compiler_params=_cparams(("arbitrary",)),
        name="moe_route_dispatch",
    )(x, g, router_w_padded)


def _ffn_grouped_kernel(trow_ref, texp_ref, tval_ref, x_ref, wg_ref, wu_ref, wd_ref, o_ref, acc_ref):
    t = pl.program_id(0)
    f = pl.program_id(1)

    @pl.when(jnp.logical_and(tval_ref[t] == TILE_ZERO, f == 0))
    def _():
        o_ref[...] = jnp.zeros(o_ref.shape, o_ref.dtype)

    @pl.when(tval_ref[t] == TILE_COMPUTE)
    def _():
        part = _dot(_swiglu(x_ref[...], wg_ref[0], wu_ref[0]).astype(BF16), wd_ref[0])

        @pl.when(f == 0)
        def _():
            acc_ref[...] = part

        @pl.when(f > 0)
        def _():
            acc_ref[...] += part

        @pl.when(f == pl.num_programs(1) - 1)
        def _():
            o_ref[...] = acc_ref[...].astype(o_ref.dtype)


def moe_grouped_ffn(xs, w_gu, w_down, trow, texp, tval):
    rows, d = xs.shape
    dff = w_down.shape[1]
    nf = dff // FFN_TF
    n_steps = trow.shape[0]
    col = lambda f, tv, t: jnp.where(tv[t] == TILE_COMPUTE, f, nf - 1)
    grid_spec = pltpu.PrefetchScalarGridSpec(
        num_scalar_prefetch=3,
        grid=(n_steps, nf),
        in_specs=[pl.BlockSpec((MOE_TG, d), lambda t, f, tr_, te, tv: (tr_[t], 0)),
                  pl.BlockSpec((1, d, FFN_TF), lambda t, f, tr_, te, tv: (te[t], 0, col(f, tv, t))),
                  pl.BlockSpec((1, d, FFN_TF), lambda t, f, tr_, te, tv: (te[t], 0, nf + col(f, tv, t))),
                  pl.BlockSpec((1, FFN_TF, d), lambda t, f, tr_, te, tv: (te[t], col(f, tv, t), 0))],
        out_specs=pl.BlockSpec((MOE_TG, d), lambda t, f, tr_, te, tv: (tr_[t], 0)),
        scratch_shapes=[pltpu.VMEM((MOE_TG, d), F32)])
    return pl.pallas_call(
        _ffn_grouped_kernel,
        out_shape=jax.ShapeDtypeStruct((rows, d), BF16),
        grid_spec=grid_spec,
        compiler_params=_cparams(("arbitrary", "arbitrary")),
        name="moe_grouped_ffn",
    )(trow, texp, tval, xs, w_gu, w_gu, w_down)


def _moe_combine_kernel(starts_ref, seg_ref, x_ref, route_ref, gain_ref, ys_hbm, o_ref, buf_ref, sem_ref,
                        *, n_experts, final_norm_fused):
    i = pl.program_id(0)
    tr = MOE_TR

    n_sub = tr // MOE_KSUB
    n_tiles = pl.num_programs(0)
    cur = i % 2

    def sub_copy(tile, e, kb):
        start = starts_ref[tile * n_experts + e] + kb * MOE_KSUB
        return pltpu.make_async_copy(ys_hbm.at[pl.ds(pl.multiple_of(start, MOE_ALIGN), MOE_KSUB), :],
                                     buf_ref.at[tile % 2, e, kb * MOE_KSUB:(kb + 1) * MOE_KSUB, :],
                                     sem_ref.at[(tile % 2) * n_experts * n_sub + e * n_sub + kb])

    def start_tile(tile):
        for e in range(n_experts):
            for kb in range(n_sub):
                @pl.when(kb * MOE_KSUB < seg_ref[tile * n_experts + e])
                def _(e=e, kb=kb):
                    sub_copy(tile, e, kb).start()

    @pl.when(i == 0)
    def _():
        start_tile(i)

    @pl.when(i + 1 < n_tiles)
    def _():
        start_tile(i + 1)

    rec = route_ref[...]
    i1, i2 = rec[:, R_I1:R_I1 + 1], rec[:, R_I2:R_I2 + 1]
    w1, w2 = rec[:, R_W1:R_W1 + 1], rec[:, R_W2:R_W2 + 1]
    lr1, lr2 = rec[:, R_LR1:R_LR1 + 1], rec[:, R_LR2:R_LR2 + 1]
    o_ref[...] = x_ref[...]
    for e in range(n_experts):
        in_e1 = i1 == float(e)
        in_e2 = i2 == float(e)
        lr = jnp.where(in_e1, lr1, jnp.where(in_e2, lr2, -1.0))
        w = jnp.where(in_e1, w1, jnp.where(in_e2, w2, 0.0))
        seg = seg_ref[i * n_experts + e]
        for kb in range(n_sub):
            @pl.when(kb * MOE_KSUB < seg)
            def _(kb=kb, lr=lr, w=w, e=e):
                sub_copy(i, e, kb).wait()
                slot = (lax.broadcasted_iota(I32, (tr, MOE_KSUB), 1) + kb * MOE_KSUB).astype(F32)
                pick = jnp.where(slot == lr, 1.0, 0.0).astype(BF16)
                o_ref[...] += w * _dot(pick, buf_ref[cur, e, kb * MOE_KSUB:(kb + 1) * MOE_KSUB, :])
    if final_norm_fused:
        o_ref[...] = _rms(o_ref[...], gain_ref[...])


def moe_combine(x, route, ys, starts_flat, seg_flat, n_experts, final_gain):
    m, d = x.shape
    tr = MOE_TR
    fused = final_gain is not None
    gain = final_gain if fused else jnp.ones((1, d), F32)
    grid_spec = pltpu.PrefetchScalarGridSpec(
        num_scalar_prefetch=2,
        grid=(m // tr,),
        in_specs=[pl.BlockSpec((tr, d), lambda i, s, g: (i, 0)),
                  pl.BlockSpec((tr, LANES), lambda i, s, g: (i, 0)),
                  pl.BlockSpec((1, d), lambda i, s, g: (0, 0)),
                  pl.BlockSpec(memory_space=pl.ANY)],
        out_specs=pl.BlockSpec((tr, d), lambda i, s, g: (i, 0)),
        scratch_shapes=[pltpu.VMEM((2, n_experts, tr, d), BF16),
                        pltpu.SemaphoreType.DMA((2 * n_experts * (tr // MOE_KSUB),))])
    return pl.pallas_call(
        functools.partial(_moe_combine_kernel, n_experts=n_experts, final_norm_fused=fused),
        out_shape=jax.ShapeDtypeStruct((m, d), F32),
        grid_spec=grid_spec,
        compiler_params=_cparams(("arbitrary",)),
        name="moe_combine",
    )(starts_flat, seg_flat, x, route, gain, ys)


TILE_DEAD, TILE_COMPUTE, TILE_ZERO = 0, 1, 2


def _moe_tile_plan(seg, n_experts, region, n_steps):
    tiles_per_region = region // MOE_TG
    rows = jnp.sum(seg, axis=0)
    n_compute = (rows + MOE_TG - 1) // MOE_TG
    n_live = jnp.minimum((rows + MOE_TR + MOE_TG - 1) // MOE_TG, tiles_per_region)
    ends = jnp.cumsum(n_live)
    total = ends[-1]
    step = jnp.arange(n_steps, dtype=I32)
    t = jnp.minimum(step, total - 1)
    texp = jnp.sum((t[:, None] >= ends[None, :]).astype(I32), axis=1)
    within = t - (ends - n_live)[texp]
    trow = texp * tiles_per_region + within
    kind = jnp.where(step < total, jnp.where(within < n_compute[texp], TILE_COMPUTE, TILE_ZERO), TILE_DEAD)
    return trow.astype(I32), texp.astype(I32), kind.astype(I32)


def moe_ffn(x, g, router_w, w_gu, w_down, final_gain=None):
    m, d = x.shape
    n_experts = router_w.shape[-1]
    region = m + MOE_TG
    n_tiles = m // MOE_TR
    rw = jnp.pad(router_w, ((0, 0), (0, LANES - n_experts)))
    xs, route, starts, seg = moe_route_dispatch(x, g, rw, n_experts)
    starts = starts[:, 0, :n_experts]
    seg = seg[:, 0, :n_experts]
    max_rows = TOP_K * m + n_tiles * n_experts * (MOE_ALIGN - 1)
    n_steps = max_rows // MOE_TG + 2 * n_experts
    trow, texp, tval = _moe_tile_plan(seg, n_experts, region, n_steps)
    ys = moe_grouped_ffn(xs, w_gu, w_down, trow, texp, tval)
    return moe_combine(x, route, ys, starts.reshape(-1), seg.reshape(-1), n_experts, final_gain)


def _final_norm_kernel(x_ref, g_ref, o_ref):
    o_ref[...] = _rms(x_ref[...], g_ref[...])


def final_norm(x, g):
    m, d = x.shape
    tm = min(m, 1024)
    return pl.pallas_call(
        _final_norm_kernel,
        out_shape=jax.ShapeDtypeStruct((m, d), F32),
        grid=(m // tm,),
        in_specs=[pl.BlockSpec((tm, d), lambda i: (i, 0)), pl.BlockSpec((1, d), lambda i: (0, 0))],
        out_specs=pl.BlockSpec((tm, d), lambda i: (i, 0)),
        compiler_params=_cparams(("parallel",)),
        name="final_norm",
    )(x, g)


def _column_plan(d_model):
    sizes = (SWA_HEADS * HEAD_DIM, SWA_KV_HEADS * HEAD_DIM, SWA_KV_HEADS * HEAD_DIM,
             DSA_HEADS * HEAD_DIM, DSA_KV_HEADS * HEAD_DIM, DSA_KV_HEADS * HEAD_DIM,
             IDX_HEADS * IDX_DIM, IDX_DIM, IDX_HEADS,
             RET_HEADS * RET_KEY_DIM, RET_HEADS * RET_KEY_DIM, RET_HEADS * RET_VAL_DIM, RET_HEADS * RET_VAL_DIM,
             len(POOL_WINDOWS) * POOL_GROUP_DIM, N_BRANCH * d_model)
    starts = np.cumsum((0,) + sizes[:-1])
    (a_q, a_k, a_v, b_q, b_k, b_v, i_q, i_k, i_w, r_q, r_k, r_v, r_g, p_u, gates) = [
        (np.arange(s, s + n), np.ones(n)) for s, n in zip(starts, sizes)]

    def scaled(c, f):
        return c[0], c[1] * f

    def dup_heads(c):
        return tuple(np.repeat(z.reshape(-1, 1, HEAD_DIM), 2, axis=1).reshape(-1) for z in c)

    def rotate_half(c):
        idx, fac = (z.reshape(-1, 2, HEAD_DIM // 2) for z in c)
        return (np.stack([idx[:, 1], idx[:, 0]], axis=1).reshape(-1),
                np.stack([-fac[:, 1], fac[:, 0]], axis=1).reshape(-1))

    def table(cols, width):
        idx = np.concatenate([c[0] for c in cols])
        fac = np.concatenate([c[1] for c in cols])
        pad = width - idx.size
        return (np.concatenate([idx, np.zeros(pad, idx.dtype)]).astype(np.int32),
                np.concatenate([fac, np.zeros(pad)]).astype(np.float32))

    qs = HEAD_DIM ** -0.5
    rope_cols = [scaled(a_q, qs), scaled(b_q, qs), scaled(i_q, IDX_DIM ** -0.5), r_q,
                 scaled(r_k, RET_KEY_DIM ** -0.5), dup_heads(a_k), dup_heads(b_k), dup_heads(i_k)]
    plain_cols = [gates, r_v, r_g, p_u, dup_heads(a_v), b_v, scaled(i_w, IDX_HEADS ** -0.5)]
    return (table(rope_cols, PA_COLS), table([rotate_half(c) for c in rope_cols], PA_COLS),
            table(plain_cols, PB_COLS))


def _layout_w_in(w_in, plan):
    return tuple((jnp.take(w_in, idx, axis=1) * fac[None, :]).astype(BF16) for idx, fac in plan)


def kernel(x, mem, positions, norm_mix, w_in, attn_sink, w_branch, w_out, pool_w, pool_scale, norm_cross, norm_mem, w_xq, w_xkv, w_xo, norm_ffn, dense_w_gu, dense_w_down, router_w, moe_w_gu, moe_w_down, norm_final):
    batch, seq, d = x.shape
    depth = norm_mix.shape[0]
    m = batch * seq
    mem_len = mem.shape[1]
    xs = x.reshape(m, d)
    mem2 = mem.reshape(batch * mem_len, d)

    inv = 1.0 / (ROPE_THETA ** (jnp.arange(0, HEAD_DIM, 2, dtype=F32) / HEAD_DIM))
    inv128 = jnp.tile(inv, LANES // (HEAD_DIM // 2))[None, :]
    cos, sin = rope_tables(positions.reshape(m, 1), inv128)
    log_gamma = jnp.log1p(-(2.0 ** (-5.0 - jnp.arange(RET_HEADS, dtype=F32))))
    x_scale = (d // X_HEADS) ** -0.5
    plan = _column_plan(d)

    for layer in range(depth):
        w_x, w_rot, w_plain = _layout_w_in(w_in[layer], plan)
        pa = norm_matmul_rope(xs, norm_mix[layer][None, :], w_x, w_rot, cos, sin, tm=1024, tn=PA_TN)
        pb = norm_matmul(xs, norm_mix[layer][None, :], w_plain, tm=1024, tn=PB_TN, name="in_proj_plain")
        o_a = swa_attention(pa, pb, attn_sink[layer], batch, seq)
        o_b = dsa_attention(pa, pb, batch, seq)
        o_c = retention(pa, pb, log_gamma, batch, seq)
        o_d = multiscale_pool(pb, pool_w[layer].astype(BF16), pool_scale[layer][None, :], batch, seq)
        xs = merge_branches(xs, (o_a, o_b, o_c, o_d), pb, w_branch[layer].astype(BF16), w_out[layer].astype(BF16))

        kv = norm_matmul(mem2, norm_mem[layer][None, :], w_xkv[layer].astype(BF16),
                         tm=batch * mem_len, tn=512, name="mem_kv_proj")
        xs = cross_attention(xs, norm_cross[layer][None, :], (w_xq[layer] * x_scale).astype(BF16), kv,
                             w_xo[layer].astype(BF16), batch, seq)

        li = layer // 2
        if layer % 2 == 0:
            xs = ffn_dense(xs, norm_ffn[layer][None, :], dense_w_gu[li].astype(BF16), dense_w_down[li].astype(BF16))
        else:
            last = layer == depth - 1
            xs = moe_ffn(xs, norm_ffn[layer][None, :], router_w[li], moe_w_gu[li].astype(BF16),
                         moe_w_down[li].astype(BF16), final_gain=norm_final[None, :] if last else None)

    if depth % 2 == 1:
        xs = final_norm(xs, norm_final[None, :])
    return xs.reshape(batch, seq, d)
```
